```python
import math
import jax
import jax.numpy as jnp
from jax import lax
import numpy as np

D_MODEL = 1024
BATCH = 2
SEQ = 16384
DEPTH = 2

MLA_HEADS = 8
MLA_NOPE = 64
MLA_ROPE = 32
MLA_V = 64
MLA_Q_RANK = 256
MLA_KV_RANK = 128
ROPE_THETA = 10000.0
Q_BLOCK = 128
GLA_HEADS = 4
GLA_DK = 64
GLA_DV = 128
GLA_GATE_RANK = 16
GLA_TAU = 16.0
GDN_HEADS = 4
GDN_DK = 128
GDN_DV = 128
GDN_CONV = 4
RWKV_HEADS = 8
RWKV_N = 64
RWKV_W_RANK = 64
RWKV_A_RANK = 64
RWKV_V_RANK = 32
RWKV_G_RANK = 128
RWKV_LN_EPS = 64e-5
CHUNK = 64
N_BRANCH = 4
BRANCH_WIDTH = 512
N_EXPERTS = 16
N_GROUPS = 4
TOP_K = 2
D_EXPERT = 512
DEEPNORM_ALPHA = (2.0 * DEPTH) ** 0.25
DEEPNORM_BETA = (8.0 * DEPTH) ** -0.25
LN_EPS = 1e-5
NORM_EPS = 1e-6

RWKV_WIDTH = RWKV_HEADS * RWKV_N
RWKV_FEAT = 3 * RWKV_WIDTH + RWKV_W_RANK + RWKV_A_RANK + RWKV_G_RANK
IN_WIDTHS = (
    MLA_Q_RANK, MLA_KV_RANK + MLA_ROPE,
    GLA_HEADS * GLA_DK, GLA_HEADS * GLA_DK, GLA_HEADS * GLA_DV,
    GLA_GATE_RANK, GLA_HEADS * GLA_DV,
    GDN_HEADS * (2 * GDN_DK + GDN_DV), GDN_HEADS, GDN_HEADS, GDN_HEADS * GDN_DV,
    RWKV_FEAT,
    N_BRANCH * D_MODEL,
)
IN_COLS = sum(IN_WIDTHS)

kernel_name = 'hybrid_mla_gla_gdn_rwkv7_grouped_moe_deepnorm'


def _split(t, widths):
    return jnp.split(t, np.cumsum(widths)[:-1].tolist(), axis=-1)


def _rmsnorm(x, g, eps=NORM_EPS):
    xf = x.astype(jnp.float32)
    y = xf * lax.rsqrt(jnp.mean(xf * xf, axis=-1, keepdims=True) + eps)
    return (y * g.astype(jnp.float32)).astype(x.dtype)


def _layernorm(x, g, b, eps=LN_EPS):
    xf = x.astype(jnp.float32)
    mu = jnp.mean(xf, axis=-1, keepdims=True)
    var = jnp.mean(jnp.square(xf - mu), axis=-1, keepdims=True)
    y = (xf - mu) * lax.rsqrt(var + eps)
    return (y * g.astype(jnp.float32) + b.astype(jnp.float32)).astype(x.dtype)


def _l2norm(x, eps=NORM_EPS):
    xf = x.astype(jnp.float32)
    return xf * lax.rsqrt(jnp.sum(xf * xf, axis=-1, keepdims=True) + eps)


def _rope(x, positions):
    half = x.shape[-1] // 2
    inv_freq = ROPE_THETA ** (-jnp.arange(half, dtype=jnp.float32) / half)
    ang = positions.astype(jnp.float32)[:, :, None] * inv_freq
    cos = jnp.cos(ang)[:, :, None, :]
    sin = jnp.sin(ang)[:, :, None, :]
    xf = x.astype(jnp.float32)
    x1, x2 = xf[..., :half], xf[..., half:]
    return jnp.concatenate([x1 * cos - x2 * sin, x1 * sin + x2 * cos], axis=-1).astype(x.dtype)


def _chunks(t):
    B, S, H, d = t.shape
    return t.reshape(B, S // CHUNK, CHUNK, H, d).transpose(1, 0, 3, 2, 4)


def _unchunk(t):
    nc, B, H, C, d = t.shape
    return t.transpose(1, 0, 3, 2, 4).reshape(B, nc * C, H, d)


def _causal_conv(x, w):
    K, C = w.shape
    return lax.conv_general_dilated(x, w[:, None, :].astype(x.dtype), window_strides=(1,),
                                    padding=((K - 1, 0),), dimension_numbers=('NWC', 'WIO', 'NWC'),
                                    feature_group_count=C)


def _token_shift_mix(p, mu):
    prev = jnp.pad(p, ((0, 0), (1, 0), (0, 0)))[:, :-1]
    return p + (prev - p) * mu


def _causal_attention(q, k, v, scale):
    B, S, H, Dh = q.shape
    nb = S // Q_BLOCK
    qb = q.reshape(B, nb, Q_BLOCK, H, Dh).transpose(1, 0, 2, 3, 4)
    key_pos = jnp.arange(S)

    def one_block(args):
        q_i, idx = args
        s = jnp.einsum('bqhd,bkhd->bhqk', q_i, k, preferred_element_type=jnp.float32) * scale
        q_pos = idx * Q_BLOCK + jnp.arange(Q_BLOCK)
        s = jnp.where(key_pos[None, :] <= q_pos[:, None], s, -jnp.inf)
        p = jax.nn.softmax(s, axis=-1).astype(v.dtype)
        return jnp.einsum('bhqk,bkhd->bqhd', p, v)

    out = lax.map(one_block, (qb, jnp.arange(nb)))
    return out.transpose(1, 0, 2, 3, 4).reshape(B, S, H, v.shape[-1])


def _mla_branch(q_lat, kv_lat, positions, q_norm, w_uq, kv_norm, w_ukv):
    B, S, _ = q_lat.shape
    q = (_rmsnorm(q_lat, q_norm) @ w_uq).reshape(B, S, MLA_HEADS, MLA_NOPE + MLA_ROPE)
    c_kv, k_rope = kv_lat[..., :MLA_KV_RANK], kv_lat[..., MLA_KV_RANK:]
    kv = (_rmsnorm(c_kv, kv_norm) @ w_ukv).reshape(B, S, MLA_HEADS, MLA_NOPE + MLA_V)
    k_nope, v = kv[..., :MLA_NOPE], kv[..., MLA_NOPE:]
    q = jnp.concatenate([q[..., :MLA_NOPE], _rope(q[..., MLA_NOPE:], positions)], axis=-1)
    k_rope = jnp.broadcast_to(_rope(k_rope[:, :, None, :], positions), (B, S, MLA_HEADS, MLA_ROPE))
    k = jnp.concatenate([k_nope, k_rope], axis=-1)
    o = _causal_attention(q, k, v, (MLA_NOPE + MLA_ROPE) ** -0.5)
    return o.reshape(B, S, MLA_HEADS * MLA_V)


def _gla_branch(q, k, v, gate_lat, out_gate, w_gate_up, b_gate, norm_g):
    B, S, _ = q.shape
    f32 = jnp.float32
    q = q.reshape(B, S, GLA_HEADS, GLA_DK).astype(f32) * GLA_DK ** -0.5
    k = k.reshape(B, S, GLA_HEADS, GLA_DK).astype(f32)
    v = v.reshape(B, S, GLA_HEADS, GLA_DV).astype(f32)
    log_a = jax.nn.log_sigmoid((gate_lat @ w_gate_up + b_gate).astype(f32)) / GLA_TAU
    log_a = log_a.reshape(B, S, GLA_HEADS, GLA_DK)
    qc, kc, vc, gc = _chunks(q), _chunks(k), _chunks(v), _chunks(log_a)
    b = jnp.cumsum(gc, axis=3)
    b_last = b[:, :, :, -1:, :]
    q_dec = qc * jnp.exp(b)
    k_inv = kc * jnp.exp(-b)
    k_end = kc * jnp.exp(b_last - b)
    causal = jnp.tril(jnp.ones((CHUNK, CHUNK), dtype=bool))
    att = jnp.where(causal, jnp.einsum('nbhid,nbhjd->nbhij', q_dec, k_inv), 0.0)
    o_intra = jnp.einsum('nbhij,nbhjv->nbhiv', att, vc)

    def step(state, xs):
        q_n, k_n, v_n, a_n = xs
        o = jnp.einsum('bhid,bhdv->bhiv', q_n, state)
        state = state * a_n[:, :, 0, :, None] + jnp.einsum('bhjd,bhjv->bhdv', k_n, v_n)
        return state, o

    s0 = jnp.zeros((B, GLA_HEADS, GLA_DK, GLA_DV), f32)
    _, o_inter = lax.scan(step, s0, (q_dec, k_end, vc, jnp.exp(b_last)))
    o = _unchunk(o_intra + o_inter)
    o = _rmsnorm(o, norm_g) * jax.nn.silu(out_gate.astype(f32)).reshape(B, S, GLA_HEADS, GLA_DV)
    return o.reshape(B, S, GLA_HEADS * GLA_DV).astype(out_gate.dtype)


def _gdn_branch(qkv, beta_logit, a_logit, z, conv_w, a_log, dt_bias, norm_g):
    B, S, _ = qkv.shape
    f32 = jnp.float32
    qkv = jax.nn.silu(_causal_conv(qkv, conv_w)).astype(f32)
    q, k, v = _split(qkv, (GDN_HEADS * GDN_DK, GDN_HEADS * GDN_DK, GDN_HEADS * GDN_DV))
    q = _l2norm(q.reshape(B, S, GDN_HEADS, GDN_DK)) * GDN_DK ** -0.5
    k = _l2norm(k.reshape(B, S, GDN_HEADS, GDN_DK))
    v = v.reshape(B, S, GDN_HEADS, GDN_DV)
    beta = jax.nn.sigmoid(beta_logit.astype(f32))
    g = -jnp.exp(a_log.astype(f32)) * jax.nn.softplus(a_logit.astype(f32) + dt_bias)
    qc, kc, vc = _chunks(q), _chunks(k), _chunks(v)
    bc = _chunks(beta[..., None])[..., 0]
    gc = jnp.cumsum(_chunks(g[..., None])[..., 0], axis=-1)
    causal = jnp.tril(jnp.ones((CHUNK, CHUNK), dtype=bool))
    strict = jnp.tril(jnp.ones((CHUNK, CHUNK), dtype=bool), k=-1)
    decay = jnp.exp(jnp.where(causal, gc[..., :, None] - gc[..., None, :], -jnp.inf))
    k_beta = kc * bc[..., None]
    v_beta = vc * bc[..., None]
    l_mat = jnp.where(strict, jnp.einsum('nbhid,nbhjd->nbhij', k_beta, kc) * decay, 0.0)
    rhs = jnp.concatenate([v_beta, k_beta * jnp.exp(gc)[..., None]], axis=-1)
    sol = lax.linalg.triangular_solve(l_mat + jnp.eye(CHUNK, dtype=f32), rhs, left_side=True,
                                      lower=True, unit_diagonal=True)
    u, w = sol[..., :GDN_DV], sol[..., GDN_DV:]
    att = jnp.einsum('nbhid,nbhjd->nbhij', qc, kc) * decay
    q_dec = qc * jnp.exp(gc)[..., None]
    g_last = gc[..., -1:]
    k_end = kc * jnp.exp(g_last - gc)[..., None]

    def step(state, xs):
        u_n, w_n, att_n, q_n, k_n, gl_n = xs
        v_new = u_n - jnp.einsum('bhik,bhkv->bhiv', w_n, state)
        o = jnp.einsum('bhik,bhkv->bhiv', q_n, state) + jnp.einsum('bhij,bhjv->bhiv', att_n, v_new)
        state = state * jnp.exp(gl_n)[..., None] + jnp.einsum('bhjk,bhjv->bhkv', k_n, v_new)
        return state, o

    s0 = jnp.zeros((B, GDN_HEADS, GDN_DK, GDN_DV), f32)
    _, o = lax.scan(step, s0, (u, w, att, q_dec, k_end, g_last))
    o = _unchunk(o)
    o = _rmsnorm(o, norm_g) * jax.nn.silu(z.astype(f32)).reshape(B, S, GDN_HEADS, GDN_DV)
    return o.reshape(B, S, GDN_HEADS * GDN_DV).astype(z.dtype)


def _rwkv7_branch(feat, mu, w0, w_up, a0, a_up, g_up, k_k, k_a, r_k, ln_g, ln_b, v_first, v_gate):
    B, S, _ = feat.shape
    f32 = jnp.float32
    feat = _token_shift_mix(feat, mu)
    r, k, v, w_lat, a_lat, g_lat = _split(feat, (RWKV_WIDTH, RWKV_WIDTH, RWKV_WIDTH,
                                                 RWKV_W_RANK, RWKV_A_RANK, RWKV_G_RANK))
    w = -jax.nn.softplus(-(w0 + jnp.tanh(w_lat) @ w_up).astype(f32)) - 0.5
    decay = jnp.exp(-jnp.exp(w))
    a = jax.nn.sigmoid(a0 + a_lat @ a_up)
    g = jax.nn.sigmoid(g_lat) @ g_up
    if v_first is not None:
        v = v + (v_first - v) * v_gate
    kk = _l2norm((k * k_k).reshape(B, S, RWKV_HEADS, RWKV_N))
    k = k * (1.0 + (a - 1.0) * k_a)

    def heads(t):
        return t.reshape(B, S, RWKV_HEADS, RWKV_N).astype(f32)

    r_h, k_h, v_h, a_h, w_h = heads(r), heads(k), heads(v), heads(a), heads(decay)
    b_h = kk * a_h

    def step(state, xs):
        r_t, w_t, k_t, v_t, kk_t, b_t = xs
        sa = jnp.einsum('bhij,bhj->bhi', state, -kk_t)
        state = (state * w_t[:, :, None, :] + sa[..., None] * b_t[:, :, None, :]
                 + v_t[..., None] * k_t[:, :, None, :])
        return state, jnp.einsum('bhij,bhj->bhi', state, r_t)

    def time_major(t):
        return jnp.moveaxis(t, 1, 0)

    s0 = jnp.zeros((B, RWKV_HEADS, RWKV_N, RWKV_N), f32)
    _, y = lax.scan(step, s0, (time_major(r_h), time_major(w_h), time_major(k_h),
                               time_major(v_h), time_major(kk), time_major(b_h)))
    y = jnp.moveaxis(y, 0, 1)
    y = _layernorm(y, ln_g.reshape(RWKV_HEADS, RWKV_N), ln_b.reshape(RWKV_HEADS, RWKV_N), eps=RWKV_LN_EPS)
    bonus = jnp.sum(r_h * k_h * r_k, axis=-1, keepdims=True) * v_h
    out = (y + bonus).reshape(B, S, RWKV_WIDTH) * g.astype(f32)
    return out.astype(feat.dtype), v


def _merge_branches(branches, gate_logits, w_branch, w_out):
    B, S, _ = gate_logits.shape
    gates = jax.nn.sigmoid(gate_logits.astype(jnp.float32)).astype(gate_logits.dtype)
    gates = gates.reshape(B, S, N_BRANCH, D_MODEL)
    merged = gates[:, :, 0] * (branches[0] @ w_branch[0])
    for n in range(1, N_BRANCH):
        merged = merged + gates[:, :, n] * (branches[n] @ w_branch[n])
    return merged @ w_out


def _grouped_moe(h, router_w, router_bias, w_gate, w_up, w_down):
    B, S, D = h.shape
    t = h.reshape(B * S, D)
    scores = jax.nn.sigmoid(jnp.matmul(t, router_w, preferred_element_type=jnp.float32))
    biased = (scores + router_bias.astype(jnp.float32)).reshape(-1, N_GROUPS, N_EXPERTS // N_GROUPS)
    group_score = jnp.sum(lax.top_k(biased, TOP_K)[0], axis=-1)
    best_group = jnp.argmax(group_score, axis=-1)
    in_group = jnp.arange(N_GROUPS)[None, :] == best_group[:, None]
    masked = jnp.where(in_group[:, :, None], biased, -jnp.inf).reshape(-1, N_EXPERTS)
    _, top_idx = lax.top_k(masked, TOP_K)
    top_scores = jnp.take_along_axis(scores, top_idx, axis=-1)
    weights = top_scores / jnp.sum(top_scores, axis=-1, keepdims=True)
    combine = jnp.sum(jax.nn.one_hot(top_idx, N_EXPERTS, dtype=jnp.float32) * weights[..., None],
                      axis=1).astype(h.dtype)
    out = jnp.zeros_like(t)
    for e in range(N_EXPERTS):
        hid = jax.nn.silu(t @ w_gate[e]) * (t @ w_up[e])
        out = out + combine[:, e:e + 1] * (hid @ w_down[e])
    return out.reshape(B, S, D)


def setup_inputs(seed: int = 0) -> dict:
    key = jax.random.key(seed)
    ks = jax.random.split(key, 40)
    f32 = jnp.float32
    L, LR = DEPTH, DEPTH - 1

    def nrm(k, shape, scale):
        return jax.random.normal(k, shape, f32) * scale

    def gain(k, shape):
        return 1.0 + 0.02 * jax.random.normal(k, shape, f32)

    x = jax.random.normal(ks[0], (BATCH, SEQ, D_MODEL), f32)
    start = jax.random.randint(ks[1], (BATCH, 1), 0, 4096, dtype=jnp.int32)
    positions = start + jnp.arange(SEQ, dtype=jnp.int32)[None, :]
    dt = jnp.exp(jax.random.uniform(ks[14], (L, GDN_HEADS), f32, math.log(1e-3), math.log(1e-1)))
    return {
        'x': x,
        'positions': positions,
        'router_w': nrm(ks[2], (D_MODEL, N_EXPERTS), D_MODEL ** -0.5),
        'w_in': nrm(ks[3], (L, D_MODEL, IN_COLS), D_MODEL ** -0.5),
        'w_in_vres': nrm(ks[4], (LR, D_MODEL, RWKV_V_RANK), D_MODEL ** -0.5),
        'mla_q_norm': gain(ks[5], (L, MLA_Q_RANK)),
        'mla_w_uq': nrm(ks[6], (L, MLA_Q_RANK, MLA_HEADS * (MLA_NOPE + MLA_ROPE)), MLA_Q_RANK ** -0.5),
        'mla_kv_norm': gain(ks[7], (L, MLA_KV_RANK)),
        'mla_w_ukv': nrm(ks[8], (L, MLA_KV_RANK, MLA_HEADS * (MLA_NOPE + MLA_V)), MLA_KV_RANK ** -0.5),
        'gla_w_gate_up': nrm(ks[9], (L, GLA_GATE_RANK, GLA_HEADS * GLA_DK), GLA_GATE_RANK ** -0.5),
        'gla_b_gate': nrm(ks[10], (L, GLA_HEADS * GLA_DK), 0.1),
        'gla_norm': gain(ks[11], (L, GLA_DV)),
        'gdn_conv': nrm(ks[12], (L, GDN_CONV, GDN_HEADS * (2 * GDN_DK + GDN_DV)), GDN_CONV ** -0.5),
        'gdn_a_log': jnp.log(jax.random.uniform(ks[13], (L, GDN_HEADS), f32, 1.0, 16.0)),
        'gdn_dt_bias': dt + jnp.log(-jnp.expm1(-dt)),
        'gdn_norm': gain(ks[15], (L, GDN_DV)),
        'rwkv_mu': jax.random.uniform(ks[16], (L, RWKV_FEAT), f32),
        'rwkv_w0': jnp.linspace(-6.0, -1.0, RWKV_WIDTH, dtype=f32)[None, :] + nrm(ks[17], (L, RWKV_WIDTH), 0.1),
        'rwkv_w_up': nrm(ks[18], (L, RWKV_W_RANK, RWKV_WIDTH), 0.1),
        'rwkv_a0': nrm(ks[19], (L, RWKV_WIDTH), 0.1),
        'rwkv_a_up': nrm(ks[20], (L, RWKV_A_RANK, RWKV_WIDTH), RWKV_A_RANK ** -0.5),
        'rwkv_g_up': nrm(ks[21], (L, RWKV_G_RANK, RWKV_WIDTH), RWKV_G_RANK ** -0.5),
        'rwkv_k_k': 0.85 + nrm(ks[22], (L, RWKV_WIDTH), 0.05),
        'rwkv_k_a': 1.0 + nrm(ks[23], (L, RWKV_WIDTH), 0.05),
        'rwkv_r_k': nrm(ks[24], (L, RWKV_HEADS, RWKV_N), 0.1),
        'rwkv_ln_g': gain(ks[25], (L, RWKV_WIDTH)),
        'rwkv_ln_b': nrm(ks[26], (L, RWKV_WIDTH), 0.02),
        'rwkv_vres_mu': jax.random.uniform(ks[27], (LR, RWKV_V_RANK), f32),
        'rwkv_v0': nrm(ks[28], (LR, RWKV_WIDTH), 0.1),
        'rwkv_v_up': nrm(ks[29], (LR, RWKV_V_RANK, RWKV_WIDTH), RWKV_V_RANK ** -0.5),
        'w_branch': nrm(ks[30], (L, N_BRANCH, BRANCH_WIDTH, D_MODEL), DEEPNORM_BETA * BRANCH_WIDTH ** -0.5),
        'w_out': nrm(ks[31], (L, D_MODEL, D_MODEL), DEEPNORM_BETA * D_MODEL ** -0.5),
        'ln1_g': gain(ks[32], (L, D_MODEL)),
        'ln1_b': nrm(ks[33], (L, D_MODEL), 0.02),
        'ln2_g': gain(ks[34], (L, D_MODEL)),
        'ln2_b': nrm(ks[35], (L, D_MODEL), 0.02),
        'router_bias': nrm(ks[36], (L, N_EXPERTS), 0.01),
        'moe_w_gate': nrm(ks[37], (L, N_EXPERTS, D_MODEL, D_EXPERT), D_MODEL ** -0.5),
        'moe_w_up': nrm(ks[38], (L, N_EXPERTS, D_MODEL, D_EXPERT), D_MODEL ** -0.5),
        'moe_w_down': nrm(ks[39], (L, N_EXPERTS, D_EXPERT, D_MODEL), DEEPNORM_BETA * D_EXPERT ** -0.5),
    }


def reference(x, positions, router_w, w_in, w_in_vres, mla_q_norm, mla_w_uq, mla_kv_norm, mla_w_ukv,
              gla_w_gate_up, gla_b_gate, gla_norm, gdn_conv, gdn_a_log, gdn_dt_bias, gdn_norm,
              rwkv_mu, rwkv_w0, rwkv_w_up, rwkv_a0, rwkv_a_up, rwkv_g_up, rwkv_k_k, rwkv_k_a, rwkv_r_k,
              rwkv_ln_g, rwkv_ln_b, rwkv_vres_mu, rwkv_v0, rwkv_v_up, w_branch, w_out,
              ln1_g, ln1_b, ln2_g, ln2_b, router_bias, moe_w_gate, moe_w_up, moe_w_down):
    v_first = None
    for l in range(DEPTH):
        if l == 0:
            w_proj = w_in[l]
        else:
            w_proj = jnp.concatenate([w_in[l], w_in_vres[l - 1]], axis=1)
        proj = x @ w_proj
        (mla_q_lat, mla_kv_lat, gla_q, gla_k, gla_v, gla_gate_lat, gla_out_gate,
         gdn_qkv, gdn_beta, gdn_a, gdn_z, rwkv_feat, gate_logits) = _split(proj[..., :IN_COLS], IN_WIDTHS)

        y_a = _mla_branch(mla_q_lat, mla_kv_lat, positions, mla_q_norm[l], mla_w_uq[l],
                          mla_kv_norm[l], mla_w_ukv[l])
        y_b = _gla_branch(gla_q, gla_k, gla_v, gla_gate_lat, gla_out_gate, gla_w_gate_up[l],
                          gla_b_gate[l], gla_norm[l])
        y_c = _gdn_branch(gdn_qkv, gdn_beta, gdn_a, gdn_z, gdn_conv[l], gdn_a_log[l],
                          gdn_dt_bias[l], gdn_norm[l])
        if l == 0:
            v_gate = None
        else:
            vres_lat = _token_shift_mix(proj[..., IN_COLS:], rwkv_vres_mu[l - 1])
            v_gate = jax.nn.sigmoid(rwkv_v0[l - 1] + vres_lat @ rwkv_v_up[l - 1])
        y_d, v_d = _rwkv7_branch(rwkv_feat, rwkv_mu[l], rwkv_w0[l], rwkv_w_up[l], rwkv_a0[l],
                                 rwkv_a_up[l], rwkv_g_up[l], rwkv_k_k[l], rwkv_k_a[l], rwkv_r_k[l],
                                 rwkv_ln_g[l], rwkv_ln_b[l], v_first, v_gate)
        if l == 0:
            v_first = v_d

        mix = _merge_branches((y_a, y_b, y_c, y_d), gate_logits, w_branch[l], w_out[l])
        x = _layernorm(DEEPNORM_ALPHA * x + mix, ln1_g[l], ln1_b[l])
        moe = _grouped_moe(x, router_w, router_bias[l], moe_w_gate[l], moe_w_up[l], moe_w_down[l])
        x = _layernorm(DEEPNORM_ALPHA * x + moe, ln2_g[l], ln2_b[l])
    return x
```

```python
import functools
import math

import jax
import jax.numpy as jnp
import numpy as np
from jax import lax
from jax.experimental import pallas as pl
from jax.experimental.pallas import tpu as pltpu

F32 = jnp.float32
BF16 = jnp.bfloat16

LANES = 128
VMEM_LIMIT = 56 * 1024 * 1024

D_MODEL = 1024
DEPTH = 2
MLA_HEADS, MLA_NOPE, MLA_ROPE, MLA_V = 8, 64, 32, 64
MLA_Q_RANK, MLA_KV_RANK = 256, 128
ROPE_THETA = 10000.0
GLA_HEADS, GLA_DK, GLA_DV, GLA_GATE_RANK, GLA_TAU = 4, 64, 128, 16, 16.0
GDN_HEADS, GDN_DK, GDN_DV, GDN_CONV = 4, 128, 128, 4
RWKV_HEADS, RWKV_N = 8, 64
RWKV_W_RANK, RWKV_A_RANK, RWKV_V_RANK, RWKV_G_RANK = 64, 64, 32, 128
RWKV_LN_EPS = 64e-5
CHUNK = 64
N_BRANCH, BRANCH_WIDTH = 4, 512
N_EXPERTS, N_GROUPS, TOP_K, D_EXPERT = 16, 4, 2, 512
GROUP_SIZE = N_EXPERTS // N_GROUPS
DEEPNORM_ALPHA = (2.0 * DEPTH) ** 0.25
LN_EPS = 1e-5
NORM_EPS = 1e-6
RWKV_WIDTH = RWKV_HEADS * RWKV_N

_OFF = {}
_o = 0
for _name, _w in (("q_lat", MLA_Q_RANK), ("c_kv", MLA_KV_RANK), ("k_rope", MLA_ROPE),
                  ("gla_q", 256), ("gla_k", 256), ("gla_v", 512), ("gla_gl", GLA_GATE_RANK), ("gla_og", 512),
                  ("gdn_qkv", 1536), ("gdn_beta", GDN_HEADS), ("gdn_a", GDN_HEADS), ("gdn_z", 512),
                  ("rwkv_rkv", 1536), ("rwkv_lat", 256), ("gates", N_BRANCH * D_MODEL)):
    _OFF[_name] = (_o, _w)
    _o += _w
IN_COLS = _o

MISC_KROPE = 0
MISC_GLA_GL = 32
MISC_GDN_BETA = 48
MISC_GDN_A = 52
MISC_VRES = 56

P16_WIDTH = 9216
P16_GATES = 0
P16_GLA_QK = 8
P16_GLA_V = 9
P16_GLA_OG = 10
P16_GDN_Q = 11
P16_GDN_Z = 14
P16_RWKV_R = 15
P32_WIDTH = 768
P32_QLAT = 0
P32_RWKV_LAT = 1
P32_CKV = 4
P32_MISC = 5


def _cparams(sem):
    return pltpu.CompilerParams(dimension_semantics=sem, vmem_limit_bytes=VMEM_LIMIT)


def _dot(a, b):
    return jnp.dot(a, b, preferred_element_type=F32)


def _dot_nt(a, b):
    return lax.dot_general(a, b, (((1,), (1,)), ((), ())), preferred_element_type=F32)


def _dot_tn(a, b):
    return lax.dot_general(a, b, (((0,), (0,)), ((), ())), preferred_element_type=F32)


def _bmm(a, b):
    return jnp.einsum('bij,bjk->bik', a, b, preferred_element_type=F32)


def _bf(x):
    return x.astype(BF16)


def _split2(x):
    hi = x.astype(BF16)
    lo = (x - hi.astype(F32)).astype(BF16)
    return hi, lo


def _dot_sel(m_bf, x):
    hi, lo = _split2(x)
    return _dot(m_bf, hi) + _dot(m_bf, lo)


def _dot_sel_r(x, m_bf):
    hi, lo = _split2(x)
    return _dot(hi, m_bf) + _dot(lo, m_bf)


def _dot3(a, b):
    ah, al = _split2(a)
    bh, bl = _split2(b)
    return _dot(ah, bh) + _dot(al, bh) + _dot(ah, bl)


def _dot3_nt(a, b):
    ah, al = _split2(a)
    bh, bl = _split2(b)
    return _dot_nt(ah, bh) + _dot_nt(al, bh) + _dot_nt(ah, bl)


def _sigmoid(x):
    return 1.0 / (1.0 + jnp.exp(-x))


def _silu(x):
    return x * _sigmoid(x)


def _softplus(x):
    return jnp.maximum(x, 0.0) + jnp.log(1.0 + jnp.exp(-jnp.abs(x)))


def _iota2(shape, dim):
    return lax.broadcasted_iota(jnp.int32, shape, dim)


def _tri_masks(c):
    r = _iota2((c, c), 0)
    q = _iota2((c, c), 1)
    return r >= q, r > q


def _unit_lower_inverse(l_mat):
    nb, c, _ = l_mat.shape
    r = _iota2((c, c), 0)
    q = _iota2((c, c), 1)
    t = jnp.broadcast_to(jnp.where(r == q, 1.0, 0.0).astype(F32)[None], (nb, c, c))
    s = 1
    while s < c:
        sh = s.bit_length()
        same = (r >> sh) == (q >> sh)
        m = jnp.where(same, jnp.where((r & s) != 0, jnp.where((q & s) == 0, 1.0, 0.0), 0.0), 0.0)
        ls = l_mat * m[None]
        tb = _bf(t)
        x = _bmm(_bf(_bmm(tb, _bf(ls))), tb)
        t = t - x
        s *= 2
    return t


def _inproj_body(x_ref, w_ref, o_ref, xb_ref):
    @pl.when(pl.program_id(2) == 0)
    def _():
        xb_ref[...] = x_ref[0].astype(BF16)

    o_ref[0] = _dot(xb_ref[...], w_ref[...]).astype(o_ref.dtype)


def _inproj(x, w, out_dtype, tm, tn):
    b, s, d = x.shape
    n = w.shape[1]
    return pl.pallas_call(
        _inproj_body,
        grid=(b, s // tm, n // tn),
        in_specs=[pl.BlockSpec((1, tm, d), lambda bi, i, j: (bi, i, 0)),
                  pl.BlockSpec((d, tn), lambda bi, i, j: (0, j))],
        out_specs=pl.BlockSpec((1, tm, tn), lambda bi, i, j: (bi, i, j)),
        out_shape=jax.ShapeDtypeStruct((b, s, n), out_dtype),
        scratch_shapes=[pltpu.VMEM((tm, d), BF16)],
        compiler_params=_cparams(("parallel", "parallel", "arbitrary")),
        name="inproj",
    )(x, w)


def _mla_prep_body(qlat_ref, ckv_ref, misc_ref, pos_ref, qnorm_ref, wqa_ref, wqb_ref, kvnorm_ref,
                   wuk_ref, wuv_ref, wkra_ref, wkrb_ref, freq_ref, q_ref, k_ref, v_ref):
    scale = (MLA_NOPE + MLA_ROPE) ** -0.5
    ang = pos_ref[0].astype(F32) * freq_ref[...]
    cos_t = jnp.cos(ang)
    sin_t = jnp.sin(ang)

    ql = qlat_ref[0]
    qn = ql * lax.rsqrt(jnp.mean(ql * ql, axis=-1, keepdims=True) + NORM_EPS) * qnorm_ref[...]
    qnb = _bf(qn)
    qa = _dot(qnb, wqa_ref[...])
    qb = _dot(qnb, wqb_ref[...])

    ck = ckv_ref[0]
    kvn = ck * lax.rsqrt(jnp.mean(ck * ck, axis=-1, keepdims=True) + NORM_EPS) * kvnorm_ref[...]
    kvb = _bf(kvn)
    kn = _dot(kvb, wuk_ref[...])
    vv = _dot(kvb, wuv_ref[...])
    misc = misc_ref[0]
    kr = _dot_sel_r(misc, wkra_ref[...]) * cos_t + _dot_sel_r(misc, wkrb_ref[...]) * sin_t

    for h in range(MLA_HEADS):
        sl = slice(h * LANES, (h + 1) * LANES)
        q_ref[0, h] = ((qa[:, sl] * cos_t + qb[:, sl] * sin_t) * scale).astype(BF16)
        k_ref[0, h] = (kn[:, sl] + kr).astype(BF16)
    for p in range(MLA_HEADS // 2):
        v_ref[0, p] = vv[:, p * LANES:(p + 1) * LANES].astype(BF16)


def _mla_prep(p32, pos3, qnorm, wqa, wqb, kvnorm, wuk, wuv, wkra, wkrb, freq, tt):
    b, s, _ = p32.shape
    full = lambda shape: pl.BlockSpec(shape, lambda bi, i: (0,) * len(shape))
    return pl.pallas_call(
        _mla_prep_body,
        grid=(b, s // tt),
        in_specs=[pl.BlockSpec((1, tt, 256), lambda bi, i: (bi, i, P32_QLAT)),
                  pl.BlockSpec((1, tt, 128), lambda bi, i: (bi, i, P32_CKV)),
                  pl.BlockSpec((1, tt, 128), lambda bi, i: (bi, i, P32_MISC)),
                  pl.BlockSpec((1, tt, 1), lambda bi, i: (bi, i, 0)),
                  full((1, 256)), full((256, 1024)), full((256, 1024)), full((1, 128)),
                  full((128, 1024)), full((128, 512)), full((128, 128)), full((128, 128)), full((1, 128))],
        out_specs=[pl.BlockSpec((1, MLA_HEADS, tt, 128), lambda bi, i: (bi, 0, i, 0)),
                   pl.BlockSpec((1, MLA_HEADS, tt, 128), lambda bi, i: (bi, 0, i, 0)),
                   pl.BlockSpec((1, MLA_HEADS // 2, tt, 128), lambda bi, i: (bi, 0, i, 0))],
        out_shape=[jax.ShapeDtypeStruct((b, MLA_HEADS, s, 128), BF16),
                   jax.ShapeDtypeStruct((b, MLA_HEADS, s, 128), BF16),
                   jax.ShapeDtypeStruct((b, MLA_HEADS // 2, s, 128), BF16)],
        compiler_params=_cparams(("parallel", "parallel")),
        name="mla_prep",
    )(p32, p32, p32, pos3, qnorm, wqa, wqb, kvnorm, wuk, wuv, wkra, wkrb, freq)


def _mla_attn_body(q_ref, k_ref, v_ref, o_ref, *, tq):
    i = pl.program_id(2)
    lane = _iota2((tq, LANES), 1)
    causal, _ = _tri_masks(tq)
    qs = [q_ref[0, hh] for hh in range(2)]

    def step(j, carry, masked):
        start = pl.multiple_of(j * tq, tq)
        vb = v_ref[0, 0, pl.ds(start, tq), :]
        out = []
        for hh in range(2):
            m, l, acc = carry[hh]
            kb = k_ref[0, hh, pl.ds(start, tq), :]
            s = _dot_nt(qs[hh], kb)
            if masked:
                s = jnp.where(causal, s, -jnp.inf)
            m_new = jnp.maximum(m, jnp.max(s, axis=-1, keepdims=True))
            p = jnp.exp(s - m_new)
            corr = jnp.exp(m - m_new)
            l = l * corr + jnp.sum(p, axis=-1, keepdims=True)
            acc = acc * corr + _dot(_bf(p), vb)
            out.append((m_new, l, acc))
        return tuple(out)

    init = tuple((jnp.full((tq, 1), -jnp.inf, F32), jnp.zeros((tq, 1), F32), jnp.zeros((tq, LANES), F32))
                 for _ in range(2))
    carry = lax.fori_loop(0, i, lambda j, c: step(j, c, False), init)
    carry = step(i, carry, True)
    o0 = carry[0][2] / carry[0][1]
    o1 = carry[1][2] / carry[1][1]
    o_ref[0] = jnp.where(lane < MLA_V, o0, o1).astype(o_ref.dtype)


def _mla_attn(q, k, v, tq):
    b, h, s, _ = q.shape
    return pl.pallas_call(
        functools.partial(_mla_attn_body, tq=tq),
        grid=(b, h // 2, s // tq),
        in_specs=[pl.BlockSpec((1, 2, tq, 128), lambda bi, p, i: (bi, p, i, 0)),
                  pl.BlockSpec((1, 2, s, 128), lambda bi, p, i: (bi, p, 0, 0)),
                  pl.BlockSpec((1, 1, s, 128), lambda bi, p, i: (bi, p, 0, 0))],
        out_specs=pl.BlockSpec((1, tq, 128), lambda bi, p, i: (bi, i, p)),
        out_shape=jax.ShapeDtypeStruct((b, s, h * MLA_V), BF16),
        compiler_params=_cparams(("parallel", "parallel", "arbitrary")),
        name="mla_attn",
    )(q, k, v)


def _gla_body(qk_ref, v_ref, og_ref, misc_ref, wgu_ref, bgate_ref, norm_ref, o_ref, st_ref, *, tt):
    c = CHUNK

    @pl.when(pl.program_id(1) == 0)
    def _():
        st_ref[...] = jnp.zeros_like(st_ref)

    causal, _ = _tri_masks(c)
    tri = jnp.where(causal, 1.0, 0.0).astype(BF16)
    lane = _iota2((1, LANES), 1)
    qk = qk_ref[0].astype(F32)
    q_all = qk[:, :256] * (GLA_DK ** -0.5)
    k_all = qk[:, 256:]
    ga = _dot3(misc_ref[0], wgu_ref[...]) + bgate_ref[...]
    log_a = (jnp.minimum(ga, 0.0) - jnp.log(1.0 + jnp.exp(-jnp.abs(ga)))) * (1.0 / GLA_TAU)

    for ci in range(tt // c):
        rows = slice(ci * c, (ci + 1) * c)
        bcum = _dot_sel(tri, log_a[rows])
        b_last = bcum[c - 1:c]
        q_dec = q_all[rows] * jnp.exp(bcum)
        k_inv = k_all[rows] * jnp.exp(-bcum)
        k_end = k_all[rows] * jnp.exp(b_last - bcum)
        e_last = jnp.exp(b_last)
        for p in range(2):
            ls = slice(p * LANES, (p + 1) * LANES)
            qd_p, ki_p, ke_p, el_p = q_dec[:, ls], k_inv[:, ls], k_end[:, ls], e_last[:, ls]
            for hh in range(2):
                h = 2 * p + hh
                own = (lane >= hh * GLA_DK) & (lane < (hh + 1) * GLA_DK)
                qd_h = _bf(jnp.where(own, qd_p, 0.0))
                v_h = v_ref[0, rows, h * GLA_DV:(h + 1) * GLA_DV]
                att = jnp.where(causal, _dot_nt(qd_h, _bf(ki_p)), 0.0)
                st = st_ref[h]
                o = _dot(_bf(att), v_h) + _dot_nt(qd_h, _bf(st))
                st_ref[h] = st * el_p + jnp.where(own, _dot_tn(v_h, _bf(ke_p)), 0.0)
                o = o * lax.rsqrt(jnp.mean(o * o, axis=-1, keepdims=True) + NORM_EPS) * norm_ref[...]
                og = og_ref[0, rows, h * GLA_DV:(h + 1) * GLA_DV].astype(F32)
                o_ref[0, rows, h * GLA_DV:(h + 1) * GLA_DV] = (o * _silu(og)).astype(o_ref.dtype)


def _gla(p16, p32, wgu, bgate, norm, tt):
    b, s, _ = p16.shape
    full = lambda shape: pl.BlockSpec(shape, lambda bi, i: (0,) * len(shape))
    return pl.pallas_call(
        functools.partial(_gla_body, tt=tt),
        grid=(b, s // tt),
        in_specs=[pl.BlockSpec((1, tt, 512), lambda bi, i: (bi, i, P16_GLA_QK)),
                  pl.BlockSpec((1, tt, 512), lambda bi, i: (bi, i, P16_GLA_V)),
                  pl.BlockSpec((1, tt, 512), lambda bi, i: (bi, i, P16_GLA_OG)),
                  pl.BlockSpec((1, tt, 128), lambda bi, i: (bi, i, P32_MISC)),
                  full((128, 256)), full((1, 256)), full((1, 128))],
        out_specs=pl.BlockSpec((1, tt, 512), lambda bi, i: (bi, i, 0)),
        out_shape=jax.ShapeDtypeStruct((b, s, 512), BF16),
        scratch_shapes=[pltpu.VMEM((GLA_HEADS, GLA_DV, LANES), F32)],
        compiler_params=_cparams(("parallel", "arbitrary")),
        name="gla",
    )(p16, p16, p16, p32, wgu, bgate, norm)


def _gdn_body(q_ref, k_ref, v_ref, z_ref, misc_ref, conv_ref, alog_ref, dtb_ref, norm_ref, o_ref,
              ext_ref, st_ref, *, tt):
    c = CHUNK
    nc = tt // c

    @pl.when(pl.program_id(1) == 0)
    def _():
        st_ref[...] = jnp.zeros_like(st_ref)
        ext_ref[0:8, :] = jnp.zeros((8, 1536), F32)

    causal, strict = _tri_masks(c)
    tri = jnp.where(causal, 1.0, 0.0).astype(BF16)
    ones_cc = jnp.ones((c, c), BF16)
    eye = jnp.where(_iota2((c, c), 0) == _iota2((c, c), 1), 1.0, 0.0).astype(F32)

    ext_ref[8:8 + tt, 0:512] = q_ref[0].astype(F32)
    ext_ref[8:8 + tt, 512:1024] = k_ref[0].astype(F32)
    ext_ref[8:8 + tt, 1024:1536] = v_ref[0].astype(F32)
    conv = None
    for j in range(GDN_CONV):
        term = ext_ref[8 - (GDN_CONV - 1) + j:8 - (GDN_CONV - 1) + j + tt, :] * conv_ref[j:j + 1, :]
        conv = term if conv is None else conv + term
    ext_ref[0:8, :] = ext_ref[tt:tt + 8, :]
    qkv = _silu(conv)

    misc = misc_ref[0]
    beta_all = _sigmoid(misc)
    g_all = -jnp.exp(alog_ref[...]) * _softplus(misc + dtb_ref[...])

    l_list, pre = [], []
    for ci in range(nc):
        rows = slice(ci * c, (ci + 1) * c)
        gc_all = _dot_sel(tri, g_all[rows])
        for h in range(GDN_HEADS):
            ls = slice(h * LANES, (h + 1) * LANES)
            q = qkv[rows, ls]
            k = qkv[rows, 512 + h * LANES:512 + (h + 1) * LANES]
            v = qkv[rows, 1024 + h * LANES:1024 + (h + 1) * LANES]
            q = q * lax.rsqrt(jnp.sum(q * q, axis=-1, keepdims=True) + NORM_EPS) * (GDN_DK ** -0.5)
            k = k * lax.rsqrt(jnp.sum(k * k, axis=-1, keepdims=True) + NORM_EPS)
            beta = beta_all[rows, MISC_GDN_BETA + h:MISC_GDN_BETA + h + 1]
            gc = gc_all[:, MISC_GDN_A + h:MISC_GDN_A + h + 1]
            gc_row = _dot_sel(ones_cc, eye * gc)
            decay = jnp.where(causal, jnp.exp(jnp.minimum(gc - gc_row, 0.0)), 0.0)
            kb = k * beta
            kbf = _bf(k)
            l_list.append(jnp.where(strict, _dot_nt(_bf(kb), kbf) * decay, 0.0))
            att = _dot_nt(_bf(q), kbf) * decay
            eg = jnp.exp(gc)
            g_last = gc[c - 1:c]
            pre.append(dict(vb=v * beta, kbe=kb * eg, att=_bf(att), q_dec=_bf(q * eg),
                            k_end=_bf(k * jnp.exp(g_last - gc)), e_last=jnp.exp(g_last)))
    t_all = _unit_lower_inverse(jnp.stack(l_list, axis=0))

    for ci in range(nc):
        rows = slice(ci * c, (ci + 1) * c)
        for h in range(GDN_HEADS):
            n = ci * GDN_HEADS + h
            d = pre[n]
            tb = _bf(t_all[n])
            u = _dot(tb, _bf(d['vb']))
            w = _dot(tb, _bf(d['kbe']))
            st = st_ref[h]
            stb = _bf(st)
            v_new = u - _dot(_bf(w), stb)
            vnb = _bf(v_new)
            o = _dot(d['q_dec'], stb) + _dot(d['att'], vnb)
            st_ref[h] = st * d['e_last'] + _dot_tn(d['k_end'], vnb)
            o = o * lax.rsqrt(jnp.mean(o * o, axis=-1, keepdims=True) + NORM_EPS) * norm_ref[...]
            z = z_ref[0, rows, h * LANES:(h + 1) * LANES].astype(F32)
            o_ref[0, rows, h * LANES:(h + 1) * LANES] = (o * _silu(z)).astype(o_ref.dtype)


def _gdn(p16, p32, conv_w, alog, dtb, norm, tt):
    b, s, _ = p16.shape
    full = lambda shape: pl.BlockSpec(shape, lambda bi, i: (0,) * len(shape))
    return pl.pallas_call(
        functools.partial(_gdn_body, tt=tt),
        grid=(b, s // tt),
        in_specs=[pl.BlockSpec((1, tt, 512), lambda bi, i: (bi, i, P16_GDN_Q)),
                  pl.BlockSpec((1, tt, 512), lambda bi, i: (bi, i, P16_GDN_Q + 1)),
                  pl.BlockSpec((1, tt, 512), lambda bi, i: (bi, i, P16_GDN_Q + 2)),
                  pl.BlockSpec((1, tt, 512), lambda bi, i: (bi, i, P16_GDN_Z)),
                  pl.BlockSpec((1, tt, 128), lambda bi, i: (bi, i, P32_MISC)),
                  full((GDN_CONV, 1536)), full((1, 128)), full((1, 128)), full((1, 128))],
        out_specs=pl.BlockSpec((1, tt, 512), lambda bi, i: (bi, i, 0)),
        out_shape=jax.ShapeDtypeStruct((b, s, 512), BF16),
        scratch_shapes=[pltpu.VMEM((tt + 8, 1536), F32), pltpu.VMEM((GDN_HEADS, GDN_DK, GDN_DV), F32)],
        compiler_params=_cparams(("parallel", "arbitrary")),
        name="gdn",
    )(p16, p16, p16, p16, p32, conv_w, alog, dtb, norm)


def _rwkv_body(*refs, tt, has_vres):
    if has_vres:
        (r_ref, k_ref, v_ref, lat_ref, misc_ref, vfirst_ref, mu_ref, mulat_ref, w0_ref, wup_ref, a0_ref, aup_ref,
         gup_ref, kk_ref, ka_ref, rk_ref, lng_ref, lnb_ref, vmu_ref, v0_ref, vup_ref, hsum_ref,
         o_ref, ext_ref, extl_ref, extm_ref, st_ref) = refs
        vout_ref = None
    else:
        (r_ref, k_ref, v_ref, lat_ref, mu_ref, mulat_ref, w0_ref, wup_ref, a0_ref, aup_ref,
         gup_ref, kk_ref, ka_ref, rk_ref, lng_ref, lnb_ref, hsum_ref,
         o_ref, vout_ref, ext_ref, extl_ref, st_ref) = refs
    c = CHUNK
    nc = tt // c
    n_pairs = RWKV_HEADS // 2

    @pl.when(pl.program_id(1) == 0)
    def _():
        st_ref[...] = jnp.zeros_like(st_ref)
        ext_ref[0:8, :] = jnp.zeros((8, 1536), F32)
        extl_ref[0:8, :] = jnp.zeros((8, 256), F32)
        if has_vres:
            extm_ref[0:8, :] = jnp.zeros((8, 128), F32)

    ext_ref[8:8 + tt, 0:512] = r_ref[0].astype(F32)
    ext_ref[8:8 + tt, 512:1024] = k_ref[0].astype(F32)
    ext_ref[8:8 + tt, 1024:1536] = v_ref[0].astype(F32)
    cur = ext_ref[8:8 + tt, :]
    rkv = cur + (ext_ref[7:7 + tt, :] - cur) * mu_ref[...]
    ext_ref[0:8, :] = ext_ref[tt:tt + 8, :]
    extl_ref[8:8 + tt, :] = lat_ref[0]
    curl = extl_ref[8:8 + tt, :]
    lat = curl + (extl_ref[7:7 + tt, :] - curl) * mulat_ref[...]
    extl_ref[0:8, :] = extl_ref[tt:tt + 8, :]

    r = rkv[:, 0:512]
    k = rkv[:, 512:1024]
    v = rkv[:, 1024:1536]
    lat_wa = lat[:, 0:128]
    lat_g = lat[:, 128:256]
    w_pre = w0_ref[...] + _dot(_bf(jnp.tanh(lat_wa)), wup_ref[...])
    w = -_softplus(-w_pre) - 0.5
    logw = -jnp.exp(w)
    a = _sigmoid(a0_ref[...] + _dot(_bf(lat_wa), aup_ref[...]))
    g = _dot(_bf(_sigmoid(lat_g)), gup_ref[...])
    if has_vres:
        extm_ref[8:8 + tt, :] = misc_ref[0]
        curm = extm_ref[8:8 + tt, :]
        vlat = curm + (extm_ref[7:7 + tt, :] - curm) * vmu_ref[...]
        extm_ref[0:8, :] = extm_ref[tt:tt + 8, :]
        v_gate = _sigmoid(v0_ref[...] + _dot(_bf(vlat), vup_ref[...]))
        v = v + (vfirst_ref[0] - v) * v_gate
    else:
        vout_ref[0] = v

    hsum = hsum_ref[...]
    kkv = k * kk_ref[...]
    kk = kkv * lax.rsqrt(_dot_sel_r(kkv * kkv, hsum) + NORM_EPS)
    k = k * (1.0 + (a - 1.0) * ka_ref[...])
    bonus = _dot_sel_r(r * k * rk_ref[...], hsum) * v
    bvec = kk * a

    lane = _iota2((1, LANES), 1)
    first = lane < RWKV_N
    rowh = _iota2((LANES, LANES), 0) < RWKV_N
    colh = _iota2((LANES, LANES), 1) < RWKV_N
    blockdiag = rowh == colh
    causal, strict = _tri_masks(c)
    tri = jnp.where(causal, 1.0, 0.0).astype(BF16)

    l_list, pre = [], []
    for ci in range(nc):
        rows = slice(ci * c, (ci + 1) * c)
        lw = logw[rows]
        cum = _dot_sel(tri, lw)
        gam = jnp.exp(cum)
        inv = jnp.exp(-cum)
        gam_ex = jnp.exp(cum - lw)
        gam_c = gam[c - 1:c]
        a_t = -kk[rows] * gam_ex
        r_t = r[rows] * gam
        b_t = bvec[rows] * inv
        k_t = k[rows] * inv
        for p in range(n_pairs):
            ls = slice(p * LANES, (p + 1) * LANES)
            a_p, r_p, b_p, k_p = a_t[:, ls], r_t[:, ls], b_t[:, ls], k_t[:, ls]
            lhs = jnp.concatenate([jnp.where(first, a_p, 0.0), jnp.where(first, 0.0, a_p),
                                   jnp.where(first, r_p, 0.0), jnp.where(first, 0.0, r_p)], axis=0)
            lhs = _bf(lhs)
            m_b = _dot_nt(lhs, _bf(b_p))
            m_k = _dot_nt(lhs, _bf(k_p))
            heads = []
            for hh in range(2):
                l_list.append(jnp.where(strict, -m_b[hh * c:(hh + 1) * c], 0.0))
                heads.append(dict(a_ak=_bf(jnp.where(strict, m_k[hh * c:(hh + 1) * c], 0.0)),
                                  a_rb=_bf(jnp.where(causal, m_b[(2 + hh) * c:(3 + hh) * c], 0.0)),
                                  a_rk=_bf(jnp.where(causal, m_k[(2 + hh) * c:(3 + hh) * c], 0.0))))
            pre.append(dict(heads=heads, a_p=_bf(a_p), r_p=_bf(r_p), gam_c=gam_c[:, ls],
                            b_end=_bf(b_p * gam_c[:, ls]), k_end=_bf(k_p * gam_c[:, ls]),
                            v_p=_bf(v[rows, ls])))
    t_all = _unit_lower_inverse(jnp.stack(l_list, axis=0))

    for ci in range(nc):
        rows = slice(ci * c, (ci + 1) * c)
        for p in range(n_pairs):
            ls = slice(p * LANES, (p + 1) * LANES)
            d = pre[ci * n_pairs + p]
            st = st_ref[p]
            stb = _bf(st)
            x = _dot_nt(d['a_p'], stb)
            y0 = _dot_nt(d['r_p'], stb)
            us, ys = [], []
            for hh in range(2):
                hd = d['heads'][hh]
                tb = _bf(t_all[(ci * n_pairs + p) * 2 + hh])
                us.append(_dot(tb, _bf(x + _dot(hd['a_ak'], d['v_p']))))
            u_p = _bf(jnp.where(first, us[0], us[1]))
            for hh in range(2):
                hd = d['heads'][hh]
                ys.append(_dot(hd['a_rb'], u_p) + _dot(hd['a_rk'], d['v_p']))
            y = y0 + jnp.where(first, ys[0], ys[1])
            upd = _dot_tn(u_p, d['b_end']) + _dot_tn(d['v_p'], d['k_end'])
            st_ref[p] = st * d['gam_c'] + jnp.where(blockdiag, upd, 0.0)
            ext_ref[8 + ci * c:8 + (ci + 1) * c, p * LANES:(p + 1) * LANES] = y

    y = ext_ref[8:8 + tt, 0:512]
    mean = _dot_sel_r(y, hsum) * (1.0 / RWKV_N)
    yc = y - mean
    var = _dot_sel_r(yc * yc, hsum) * (1.0 / RWKV_N)
    yn = yc * lax.rsqrt(var + RWKV_LN_EPS) * lng_ref[...] + lnb_ref[...]
    o_ref[0] = ((yn + bonus) * g).astype(o_ref.dtype)


def _rwkv(p16, p32, v_first, weights, tt, has_vres):
    b, s, _ = p16.shape
    full = lambda shape: pl.BlockSpec(shape, lambda bi, i: (0,) * len(shape))
    tok = lambda w, blk: pl.BlockSpec((1, tt, w), lambda bi, i: (bi, i, blk))
    in_specs = [tok(512, P16_RWKV_R), tok(512, P16_RWKV_R + 1), tok(512, P16_RWKV_R + 2), tok(256, P32_RWKV_LAT)]
    args = [p16, p16, p16, p32]
    if has_vres:
        in_specs += [tok(128, P32_MISC), tok(512, 0)]
        args += [p32, v_first]
    in_specs += [full(w.shape) for w in weights]
    args += list(weights)
    out_specs = [pl.BlockSpec((1, tt, 512), lambda bi, i: (bi, i, 0))]
    out_shape = [jax.ShapeDtypeStruct((b, s, 512), BF16)]
    scratch = [pltpu.VMEM((tt + 8, 1536), F32), pltpu.VMEM((tt + 8, 256), F32)]
    if has_vres:
        scratch.append(pltpu.VMEM((tt + 8, 128), F32))
    else:
        out_specs.append(pl.BlockSpec((1, tt, 512), lambda bi, i: (bi, i, 0)))
        out_shape.append(jax.ShapeDtypeStruct((b, s, 512), F32))
    scratch.append(pltpu.VMEM((RWKV_HEADS // 2, LANES, LANES), F32))
    return pl.pallas_call(
        functools.partial(_rwkv_body, tt=tt, has_vres=has_vres),
        grid=(b, s // tt),
        in_specs=in_specs,
        out_specs=out_specs,
        out_shape=out_shape,
        scratch_shapes=scratch,
        compiler_params=_cparams(("parallel", "arbitrary")),
        name="rwkv7",
    )(*args)


def _layernorm(h, g, b):
    mu = jnp.mean(h, axis=-1, keepdims=True)
    hc = h - mu
    var = jnp.mean(hc * hc, axis=-1, keepdims=True)
    return hc * lax.rsqrt(var + LN_EPS) * g + b


def _merge_body(ya_ref, yb_ref, yc_ref, yd_ref, gate_ref, x_ref, wb_ref, wo_ref, g_ref, b_ref, o_ref):
    merged = None
    for n, y_ref in enumerate((ya_ref, yb_ref, yc_ref, yd_ref)):
        gate = _sigmoid(gate_ref[0, :, n * D_MODEL:(n + 1) * D_MODEL].astype(F32))
        term = gate * _dot(y_ref[0], wb_ref[n])
        merged = term if merged is None else merged + term
    mix = _dot(_bf(merged), wo_ref[...])
    o_ref[0] = _layernorm(DEEPNORM_ALPHA * x_ref[0] + mix, g_ref[...], b_ref[...])


def _merge(ys, p16, x, wb, wo, g, bb, tt):
    b, s, d = x.shape
    full = lambda shape: pl.BlockSpec(shape, lambda bi, i: (0,) * len(shape))
    ytok = pl.BlockSpec((1, tt, 512), lambda bi, i: (bi, i, 0))
    return pl.pallas_call(
        _merge_body,
        grid=(b, s // tt),
        in_specs=[ytok, ytok, ytok, ytok,
                  pl.BlockSpec((1, tt, N_BRANCH * D_MODEL), lambda bi, i: (bi, i, P16_GATES)),
                  pl.BlockSpec((1, tt, d), lambda bi, i: (bi, i, 0)),
                  full((N_BRANCH, BRANCH_WIDTH, D_MODEL)), full((D_MODEL, D_MODEL)), full((1, d)), full((1, d))],
        out_specs=pl.BlockSpec((1, tt, d), lambda bi, i: (bi, i, 0)),
        out_shape=jax.ShapeDtypeStruct((b, s, d), F32),
        compiler_params=_cparams(("parallel", "parallel")),
        name="merge_ln1",
    )(*ys, p16, x, wb, wo, g, bb)


def _moe_body(x_ref, rwt_ref, rbias_ref, wg_ref, wu_ref, wd_ref, g_ref, b_ref, o_ref, xb_ref, comb_ref, acc_ref):
    e = pl.program_id(2)
    tt = x_ref.shape[1]

    @pl.when(e == 0)
    def _():
        x = x_ref[0]
        xb_ref[...] = x.astype(BF16)
        acc_ref[...] = jnp.zeros_like(acc_ref)
        scores = _sigmoid(_dot3_nt(rwt_ref[...], x))
        biased = scores + rbias_ref[...]
        row = _iota2((N_EXPERTS, tt), 0).astype(F32)
        row_group = (_iota2((N_EXPERTS, tt), 0) >> 2).astype(F32)
        neg = -jnp.inf
        gs = []
        for gi in range(N_GROUPS):
            rows = [biased[gi * GROUP_SIZE + j:gi * GROUP_SIZE + j + 1] for j in range(GROUP_SIZE)]
            best = None
            for i0 in range(GROUP_SIZE):
                for i1 in range(i0 + 1, GROUP_SIZE):
                    pair = rows[i0] + rows[i1]
                    best = pair if best is None else jnp.maximum(best, pair)
            gs.append(best)
        gmax = functools.reduce(jnp.maximum, gs)
        best_group = jnp.full((1, tt), float(N_GROUPS), F32)
        for gi in reversed(range(N_GROUPS)):
            best_group = jnp.where(gs[gi] == gmax, float(gi), best_group)
        masked = jnp.where(row_group == best_group, biased, neg)
        m1 = jnp.max(masked, axis=0, keepdims=True)
        i1 = jnp.min(jnp.where(masked == m1, row, float(N_EXPERTS)), axis=0, keepdims=True)
        masked2 = jnp.where(row == i1, neg, masked)
        m2 = jnp.max(masked2, axis=0, keepdims=True)
        i2 = jnp.min(jnp.where(masked2 == m2, row, float(N_EXPERTS)), axis=0, keepdims=True)
        s1 = jnp.sum(jnp.where(row == i1, scores, 0.0), axis=0, keepdims=True)
        s2 = jnp.sum(jnp.where(row == i2, scores, 0.0), axis=0, keepdims=True)
        tot = s1 + s2
        comb_t = jnp.where(row == i1, s1 / tot, 0.0) + jnp.where(row == i2, s2 / tot, 0.0)
        eye = jnp.where(_iota2((N_EXPERTS, N_EXPERTS), 0) == _iota2((N_EXPERTS, N_EXPERTS), 1), 1.0, 0.0).astype(BF16)
        hi, lo = _split2(comb_t)
        comb_ref[...] = _dot_tn(hi, eye) + _dot_tn(lo, eye)

    xb = xb_ref[...]
    hid = _silu(_dot(xb, wg_ref[0])) * _dot(xb, wu_ref[0])
    ce = jnp.sum(jnp.where(_iota2((1, N_EXPERTS), 1) == e, comb_ref[...], 0.0), axis=-1, keepdims=True)
    acc_ref[...] += ce * _dot(_bf(hid), wd_ref[0])

    @pl.when(e == N_EXPERTS - 1)
    def _():
        o_ref[0] = _layernorm(DEEPNORM_ALPHA * x_ref[0] + acc_ref[...], g_ref[...], b_ref[...])


def _moe(x, rwt, rbias, wg, wu, wd, g, bb, tt):
    b, s, d = x.shape
    full = lambda shape: pl.BlockSpec(shape, lambda bi, i, e: (0,) * len(shape))
    return pl.pallas_call(
        _moe_body,
        grid=(b, s // tt, N_EXPERTS),
        in_specs=[pl.BlockSpec((1, tt, d), lambda bi, i, e: (bi, i, 0)),
                  full((N_EXPERTS, d)), full((N_EXPERTS, 1)),
                  pl.BlockSpec((1, d, D_EXPERT), lambda bi, i, e: (e, 0, 0)),
                  pl.BlockSpec((1, d, D_EXPERT), lambda bi, i, e: (e, 0, 0)),
                  pl.BlockSpec((1, D_EXPERT, d), lambda bi, i, e: (e, 0, 0)),
                  full((1, d)), full((1, d))],
        out_specs=pl.BlockSpec((1, tt, d), lambda bi, i, e: (bi, i, 0)),
        out_shape=jax.ShapeDtypeStruct((b, s, d), F32),
        scratch_shapes=[pltpu.VMEM((tt, d), BF16), pltpu.VMEM((tt, N_EXPERTS), F32), pltpu.VMEM((tt, d), F32)],
        compiler_params=_cparams(("parallel", "parallel", "arbitrary")),
        name="moe_ln2",
    )(x, rwt, rbias, wg, wu, wd, g, bb)


def _cols(w, name):
    o, n = _OFF[name]
    return w[:, o:o + n]


def _pack_inproj(w_in_l, w_vres_l):
    d = w_in_l.shape[0]
    w16 = jnp.concatenate([_cols(w_in_l, "gates"), _cols(w_in_l, "gla_q"), _cols(w_in_l, "gla_k"),
                           _cols(w_in_l, "gla_v"), _cols(w_in_l, "gla_og"), _cols(w_in_l, "gdn_qkv"),
                           _cols(w_in_l, "gdn_z"), _cols(w_in_l, "rwkv_rkv")], axis=1).astype(BF16)
    vres = jnp.zeros((d, RWKV_V_RANK), F32) if w_vres_l is None else w_vres_l
    misc = jnp.concatenate([_cols(w_in_l, "k_rope"), _cols(w_in_l, "gla_gl"), _cols(w_in_l, "gdn_beta"),
                            _cols(w_in_l, "gdn_a"), vres], axis=1)
    misc = jnp.pad(misc, ((0, 0), (0, LANES - misc.shape[1])))
    w32 = jnp.concatenate([_cols(w_in_l, "q_lat"), _cols(w_in_l, "rwkv_lat"), _cols(w_in_l, "c_kv"), misc],
                          axis=1).astype(BF16)
    return w16, w32


def _row_pad(w, start, total=LANES):
    return jnp.pad(w, ((start, total - start - w.shape[0]), (0, 0)))


def _lane_row(vec, start, total=LANES):
    return jnp.pad(vec, (start, total - start - vec.shape[0])).reshape(1, total)


def _pack_mla(w_uq, w_ukv):
    half = MLA_ROPE // 2
    wq = w_uq.reshape(MLA_Q_RANK, MLA_HEADS, MLA_NOPE + MLA_ROPE)
    nope, r1, r2 = wq[..., :MLA_NOPE], wq[..., MLA_NOPE:MLA_NOPE + half], wq[..., MLA_NOPE + half:]
    pad = jnp.zeros((MLA_Q_RANK, MLA_HEADS, LANES - MLA_NOPE - MLA_ROPE), F32)
    wqa = jnp.concatenate([nope, r1, r2, pad], axis=-1).reshape(MLA_Q_RANK, MLA_HEADS * LANES)
    wqb = jnp.concatenate([jnp.zeros_like(nope), -r2, r1, pad], axis=-1).reshape(MLA_Q_RANK, MLA_HEADS * LANES)
    wkv = w_ukv.reshape(MLA_KV_RANK, MLA_HEADS, MLA_NOPE + MLA_V)
    wuk = jnp.pad(wkv[..., :MLA_NOPE], ((0, 0), (0, 0), (0, LANES - MLA_NOPE))).reshape(MLA_KV_RANK, MLA_HEADS * LANES)
    wuv = wkv[..., MLA_NOPE:].reshape(MLA_KV_RANK, MLA_HEADS * MLA_V)
    ra = np.zeros((LANES, LANES), np.float32)
    rb = np.zeros((LANES, LANES), np.float32)
    for j in range(half):
        ra[MISC_KROPE + j, MLA_NOPE + j] = 1.0
        ra[MISC_KROPE + half + j, MLA_NOPE + half + j] = 1.0
        rb[MISC_KROPE + half + j, MLA_NOPE + j] = -1.0
        rb[MISC_KROPE + j, MLA_NOPE + half + j] = 1.0
    inv_freq = ROPE_THETA ** (-jnp.arange(half, dtype=F32) / half)
    freq = jnp.concatenate([jnp.zeros((MLA_NOPE,), F32), inv_freq, inv_freq,
                            jnp.zeros((LANES - MLA_NOPE - MLA_ROPE,), F32)]).reshape(1, LANES)
    return (wqa.astype(BF16), wqb.astype(BF16), wuk.astype(BF16), wuv.astype(BF16),
            jnp.asarray(ra, BF16), jnp.asarray(rb, BF16), freq)


def _head_sum_matrix():
    idx = np.arange(RWKV_WIDTH) // RWKV_N
    return jnp.asarray((idx[:, None] == idx[None, :]).astype(np.float32), BF16)


def _tile(s, pref):
    t = min(pref, s)
    assert s % t == 0 and t % CHUNK == 0
    return t


def kernel(x, positions, router_w, w_in, w_in_vres, mla_q_norm, mla_w_uq, mla_kv_norm, mla_w_ukv,
           gla_w_gate_up, gla_b_gate, gla_norm, gdn_conv, gdn_a_log, gdn_dt_bias, gdn_norm,
           rwkv_mu, rwkv_w0, rwkv_w_up, rwkv_a0, rwkv_a_up, rwkv_g_up, rwkv_k_k, rwkv_k_a, rwkv_r_k,
           rwkv_ln_g, rwkv_ln_b, rwkv_vres_mu, rwkv_v0, rwkv_v_up, w_branch, w_out,
           ln1_g, ln1_b, ln2_g, ln2_b, router_bias, moe_w_gate, moe_w_up, moe_w_down):
    b, s, d = x.shape
    assert d == D_MODEL and w_in.shape[2] == IN_COLS
    pos3 = positions.reshape(b, s, 1)
    rwt = router_w.T
    hsum = _head_sum_matrix()
    row = lambda v: v.reshape(1, -1)
    t_proj = _tile(s, 1024)
    t_scan = _tile(s, 256)
    t_attn = _tile(s, 256)
    t_merge = _tile(s, 256)
    t_moe = _tile(s, 1024)

    v_first = None
    for l in range(DEPTH):
        w16, w32 = _pack_inproj(w_in[l], None if l == 0 else w_in_vres[l - 1])
        p16 = _inproj(x, w16, BF16, t_proj, 1024)
        p32 = _inproj(x, w32, F32, t_proj, P32_WIDTH)

        wqa, wqb, wuk, wuv, wkra, wkrb, freq = _pack_mla(mla_w_uq[l], mla_w_ukv[l])
        q, k, v = _mla_prep(p32, pos3, row(mla_q_norm[l]), wqa, wqb, row(mla_kv_norm[l]), wuk, wuv, wkra, wkrb,
                            freq, t_scan)
        y_a = _mla_attn(q, k, v, t_attn)

        wgu = _row_pad(gla_w_gate_up[l], MISC_GLA_GL)
        y_b = _gla(p16, p32, wgu, row(gla_b_gate[l]), row(gla_norm[l]), t_scan)

        y_c = _gdn(p16, p32, gdn_conv[l], _lane_row(gdn_a_log[l], MISC_GDN_A), _lane_row(gdn_dt_bias[l], MISC_GDN_A),
                   row(gdn_norm[l]), t_scan)

        mu = rwkv_mu[l]
        weights = [row(mu[:1536]), row(mu[1536:]), row(rwkv_w0[l]),
                   _row_pad(rwkv_w_up[l], 0).astype(BF16), row(rwkv_a0[l]),
                   _row_pad(rwkv_a_up[l], RWKV_W_RANK).astype(BF16), rwkv_g_up[l].astype(BF16),
                   row(rwkv_k_k[l]), row(rwkv_k_a[l]), row(rwkv_r_k[l]), row(rwkv_ln_g[l]), row(rwkv_ln_b[l])]
        if l == 0:
            y_d, v_first = _rwkv(p16, p32, None, weights + [hsum], t_scan, False)
        else:
            weights += [_lane_row(rwkv_vres_mu[l - 1], MISC_VRES), row(rwkv_v0[l - 1]),
                        _row_pad(rwkv_v_up[l - 1], MISC_VRES).astype(BF16)]
            (y_d,) = _rwkv(p16, p32, v_first, weights + [hsum], t_scan, True)

        x = _merge((y_a, y_b, y_c, y_d), p16, x, w_branch[l].astype(BF16), w_out[l].astype(BF16),
                   row(ln1_g[l]), row(ln1_b[l]), t_merge)
        x = _moe(x, rwt, router_bias[l].reshape(N_EXPERTS, 1), moe_w_gate[l].astype(BF16),
                 moe_w_up[l].astype(BF16), moe_w_down[l].astype(BF16), row(ln2_g[l]), row(ln2_b[l]), t_moe)
    return x
```

```python
import functools
import math

import jax
import jax.numpy as jnp
import numpy as np
from jax import lax
from jax.experimental import pallas as pl
from jax.experimental.pallas import tpu as pltpu

F32 = jnp.float32
BF16 = jnp.bfloat16

LANES = 128
VMEM_LIMIT = 56 * 1024 * 1024

D_MODEL = 1024
DEPTH = 2
MLA_HEADS, MLA_NOPE, MLA_ROPE, MLA_V = 8, 64, 32, 64
MLA_Q_RANK, MLA_KV_RANK = 256, 128
MLA_VT_ROWS = MLA_V + 16
ROPE_THETA = 10000.0
GLA_HEADS, GLA_DK, GLA_DV, GLA_GATE_RANK, GLA_TAU = 4, 64, 128, 16, 16.0
GDN_HEADS, GDN_DK, GDN_DV, GDN_CONV = 4, 128, 128, 4
RWKV_HEADS, RWKV_N = 8, 64
RWKV_W_RANK, RWKV_A_RANK, RWKV_V_RANK, RWKV_G_RANK = 64, 64, 32, 128
RWKV_LN_EPS = 64e-5
CHUNK = 64
N_BRANCH, BRANCH_WIDTH = 4, 512
N_EXPERTS, N_GROUPS, TOP_K, D_EXPERT = 16, 4, 2, 512
GROUP_SIZE = N_EXPERTS // N_GROUPS
DEEPNORM_ALPHA = (2.0 * DEPTH) ** 0.25
LN_EPS = 1e-5
NORM_EPS = 1e-6
RWKV_WIDTH = RWKV_HEADS * RWKV_N

_OFF = {}
_o = 0
for _name, _w in (("q_lat", MLA_Q_RANK), ("c_kv", MLA_KV_RANK), ("k_rope", MLA_ROPE),
                  ("gla_q", 256), ("gla_k", 256), ("gla_v", 512), ("gla_gl", GLA_GATE_RANK), ("gla_og", 512),
                  ("gdn_qkv", 1536), ("gdn_beta", GDN_HEADS), ("gdn_a", GDN_HEADS), ("gdn_z", 512),
                  ("rwkv_rkv", 1536), ("rwkv_lat", 256), ("gates", N_BRANCH * D_MODEL)):
    _OFF[_name] = (_o, _w)
    _o += _w
IN_COLS = _o

MISC_KROPE = 0
MISC_GLA_GL = 32
MISC_GDN_BETA = 48
MISC_GDN_A = 52
MISC_VRES = 56

P16_WIDTH = 9216
P16_GATES = 0
P16_GLA_QK = 8
P16_GLA_V = 9
P16_GLA_OG = 10
P16_GDN_Q = 11
P16_GDN_Z = 14
P16_RWKV_R = 15
P32_WIDTH = 768
P32_QLAT = 0
P32_RWKV_LAT = 1
P32_CKV = 4
P32_MISC = 5


def _cparams(sem):
    return pltpu.CompilerParams(dimension_semantics=sem, vmem_limit_bytes=VMEM_LIMIT)


def _dot(a, b):
    return jnp.dot(a, b, preferred_element_type=F32)


def _dot_nt(a, b):
    return lax.dot_general(a, b, (((1,), (1,)), ((), ())), preferred_element_type=F32)


def _dot_tn(a, b):
    return lax.dot_general(a, b, (((0,), (0,)), ((), ())), preferred_element_type=F32)


def _bmm(a, b):
    return jnp.einsum('bij,bjk->bik', a, b, preferred_element_type=F32)


def _bf(x):
    return x.astype(BF16)


def _split2(x):
    hi = x.astype(BF16)
    lo = (x - hi.astype(F32)).astype(BF16)
    return hi, lo


def _dot_sel(m_bf, x):
    hi, lo = _split2(x)
    return _dot(m_bf, hi) + _dot(m_bf, lo)


def _dot_sel_r(x, m_bf):
    hi, lo = _split2(x)
    return _dot(hi, m_bf) + _dot(lo, m_bf)


def _dot3(a, b):
    ah, al = _split2(a)
    bh, bl = _split2(b)
    return _dot(ah, bh) + _dot(al, bh) + _dot(ah, bl)


def _dot3_nt(a, b):
    ah, al = _split2(a)
    bh, bl = _split2(b)
    return _dot_nt(ah, bh) + _dot_nt(al, bh) + _dot_nt(ah, bl)


def _sigmoid(x):
    return 1.0 / (1.0 + jnp.exp(-x))


def _silu(x):
    return x * _sigmoid(x)


def _softplus(x):
    return jnp.maximum(x, 0.0) + jnp.log(1.0 + jnp.exp(-jnp.abs(x)))


def _iota2(shape, dim):
    return lax.broadcasted_iota(jnp.int32, shape, dim)


def _tri_masks(c):
    r = _iota2((c, c), 0)
    q = _iota2((c, c), 1)
    return r >= q, r > q


def _unit_lower_inverse(l_mat):
    nb, c, _ = l_mat.shape
    r = _iota2((c, c), 0)
    q = _iota2((c, c), 1)
    t = jnp.broadcast_to(jnp.where(r == q, 1.0, 0.0).astype(F32)[None], (nb, c, c))
    s = 1
    while s < c:
        sh = s.bit_length()
        same = (r >> sh) == (q >> sh)
        m = jnp.where(same, jnp.where((r & s) != 0, jnp.where((q & s) == 0, 1.0, 0.0), 0.0), 0.0)
        ls = l_mat * m[None]
        tb = _bf(t)
        x = _bmm(_bf(_bmm(tb, _bf(ls))), tb)
        t = t - x
        s *= 2
    return t


def _inproj_body(x_ref, w_ref, o_ref, xb_ref):
    @pl.when(pl.program_id(2) == 0)
    def _():
        xb_ref[...] = x_ref[0].astype(BF16)

    o_ref[0] = _dot(xb_ref[...], w_ref[...]).astype(o_ref.dtype)


def _inproj(x, w, out_dtype, tm, tn):
    b, s, d = x.shape
    n = w.shape[1]
    return pl.pallas_call(
        _inproj_body,
        grid=(b, s // tm, n // tn),
        in_specs=[pl.BlockSpec((1, tm, d), lambda bi, i, j: (bi, i, 0)),
                  pl.BlockSpec((d, tn), lambda bi, i, j: (0, j))],
        out_specs=pl.BlockSpec((1, tm, tn), lambda bi, i, j: (bi, i, j)),
        out_shape=jax.ShapeDtypeStruct((b, s, n), out_dtype),
        scratch_shapes=[pltpu.VMEM((tm, d), BF16)],
        compiler_params=_cparams(("parallel", "parallel", "arbitrary")),
        name="inproj",
    )(x, w)


def _mla_prep_body(qlat_ref, ckv_ref, misc_ref, pos_ref, qnorm_ref, wqa_ref, wqb_ref, kvnorm_ref,
                   wuk_ref, wuv_ref, wkra_ref, wkrb_ref, freq_ref, q_ref, k_ref, v_ref):
    scale = (MLA_NOPE + MLA_ROPE) ** -0.5 * math.log2(math.e)
    ang = pos_ref[0].astype(F32) * freq_ref[...]
    cos_t = jnp.cos(ang)
    sin_t = jnp.sin(ang)

    ql = qlat_ref[0]
    qn = ql * lax.rsqrt(jnp.mean(ql * ql, axis=-1, keepdims=True) + NORM_EPS) * qnorm_ref[...]
    qnb = _bf(qn)
    qa = _dot(qnb, wqa_ref[...])
    qb = _dot(qnb, wqb_ref[...])

    ck = ckv_ref[0]
    kvn = ck * lax.rsqrt(jnp.mean(ck * ck, axis=-1, keepdims=True) + NORM_EPS) * kvnorm_ref[...]
    kvb = _bf(kvn)
    kn = _dot(kvb, wuk_ref[...])
    vt = _dot_nt(wuv_ref[...], kvb)
    misc = misc_ref[0]
    kr = _dot_sel_r(misc, wkra_ref[...]) * cos_t + _dot_sel_r(misc, wkrb_ref[...]) * sin_t

    for h in range(MLA_HEADS):
        sl = slice(h * LANES, (h + 1) * LANES)
        q_ref[0, h] = ((qa[:, sl] * cos_t + qb[:, sl] * sin_t) * scale).astype(BF16)
        k_ref[0, h] = (kn[:, sl] + kr).astype(BF16)
    t = vt.shape[1]
    ones = jnp.ones((MLA_VT_ROWS - MLA_V, t), F32)
    for h in range(MLA_HEADS):
        v_ref[0, h, 0] = jnp.concatenate([vt[h * MLA_V:(h + 1) * MLA_V], ones], axis=0).astype(BF16)


def _mla_prep(p32, pos3, qnorm, wqa, wqb, kvnorm, wuk, wuv, wkra, wkrb, freq, tt):
    b, s, _ = p32.shape
    full = lambda shape: pl.BlockSpec(shape, lambda bi, i: (0,) * len(shape))
    return pl.pallas_call(
        _mla_prep_body,
        grid=(b, s // tt),
        in_specs=[pl.BlockSpec((1, tt, 256), lambda bi, i: (bi, i, P32_QLAT)),
                  pl.BlockSpec((1, tt, 128), lambda bi, i: (bi, i, P32_CKV)),
                  pl.BlockSpec((1, tt, 128), lambda bi, i: (bi, i, P32_MISC)),
                  pl.BlockSpec((1, tt, 1), lambda bi, i: (bi, i, 0)),
                  full((1, 256)), full((256, 1024)), full((256, 1024)), full((1, 128)),
                  full((128, 1024)), full((512, 128)), full((128, 128)), full((128, 128)), full((1, 128))],
        out_specs=[pl.BlockSpec((1, MLA_HEADS, tt, 128), lambda bi, i: (bi, 0, i, 0)),
                   pl.BlockSpec((1, MLA_HEADS, tt, 128), lambda bi, i: (bi, 0, i, 0)),
                   pl.BlockSpec((1, MLA_HEADS, 1, MLA_VT_ROWS, tt), lambda bi, i: (bi, 0, i, 0, 0))],
        out_shape=[jax.ShapeDtypeStruct((b, MLA_HEADS, s, 128), BF16),
                   jax.ShapeDtypeStruct((b, MLA_HEADS, s, 128), BF16),
                   jax.ShapeDtypeStruct((b, MLA_HEADS, s // tt, MLA_VT_ROWS, tt), BF16)],
        compiler_params=_cparams(("parallel", "parallel")),
        name="mla_prep",
    )(p32, p32, p32, pos3, qnorm, wqa, wqb, kvnorm, wuk, wuv, wkra, wkrb, freq)


def _mla_attn_body(q_ref, k_ref, v_ref, o_ref, m_ref, acc_ref, s_ref, *, tq):
    i = pl.program_id(2)
    for hh in range(2):
        m_ref[hh] = jnp.full((1, tq), -jnp.inf, F32)
        acc_ref[hh] = jnp.zeros((MLA_VT_ROWS, tq), F32)

    def scores(hh, j):
        start = pl.multiple_of(j * tq, tq)
        return _dot_nt(k_ref[0, hh, pl.ds(start, tq), :], q_ref[0, hh])

    def consume(hh, j, masked):
        st = s_ref[hh]
        if masked:
            st = jnp.where(_iota2((tq, tq), 0) <= _iota2((tq, tq), 1), st, -jnp.inf)
        m_old = m_ref[hh]
        m_new = jnp.maximum(m_old, jnp.max(st, axis=0, keepdims=True))
        p = jnp.exp2(st - m_new)
        acc_ref[hh] = acc_ref[hh] * jnp.exp2(m_old - m_new) + _dot(v_ref[0, hh, j], _bf(p))
        m_ref[hh] = m_new

    s_ref[0] = scores(0, 0)

    def loop_body(j, carry):
        s_ref[1] = scores(1, j)
        consume(0, j, False)
        s_ref[0] = scores(0, j + 1)
        consume(1, j, False)
        return carry

    lax.fori_loop(0, i, loop_body, 0)
    s_ref[1] = scores(1, i)
    consume(0, i, True)
    consume(1, i, True)
    outs = []
    for hh in range(2):
        a = acc_ref[hh]
        outs.append(a[0:MLA_V] / a[MLA_V:MLA_V + 1])
    o_ref[0] = jnp.concatenate(outs, axis=0).T.astype(o_ref.dtype)


def _mla_attn(q, k, v, tq):
    b, h, s, _ = q.shape
    return pl.pallas_call(
        functools.partial(_mla_attn_body, tq=tq),
        grid=(b, h // 2, s // tq),
        in_specs=[pl.BlockSpec((1, 2, tq, 128), lambda bi, p, i: (bi, p, i, 0)),
                  pl.BlockSpec((1, 2, s, 128), lambda bi, p, i: (bi, p, 0, 0)),
                  pl.BlockSpec((1, 2, s // tq, MLA_VT_ROWS, tq), lambda bi, p, i: (bi, p, 0, 0, 0))],
        out_specs=pl.BlockSpec((1, tq, 128), lambda bi, p, i: (bi, i, p)),
        out_shape=jax.ShapeDtypeStruct((b, s, h * MLA_V), BF16),
        scratch_shapes=[pltpu.VMEM((2, 1, tq), F32), pltpu.VMEM((2, MLA_VT_ROWS, tq), F32),
                        pltpu.VMEM((2, tq, tq), F32)],
        compiler_params=_cparams(("parallel", "parallel", "arbitrary")),
        name="mla_attn",
    )(q, k, v)


def _gla_body(qk_ref, v_ref, og_ref, misc_ref, wgu_ref, bgate_ref, norm_ref, o_ref, st_ref, *, tt):
    c = CHUNK

    @pl.when(pl.program_id(1) == 0)
    def _():
        st_ref[...] = jnp.zeros_like(st_ref)

    causal, _ = _tri_masks(c)
    tri = jnp.where(causal, 1.0, 0.0).astype(BF16)
    lane = _iota2((1, LANES), 1)
    qk = qk_ref[0].astype(F32)
    q_all = qk[:, :256] * (GLA_DK ** -0.5)
    k_all = qk[:, 256:]
    ga = _dot3(misc_ref[0], wgu_ref[...]) + bgate_ref[...]
    log_a = (jnp.minimum(ga, 0.0) - jnp.log(1.0 + jnp.exp(-jnp.abs(ga)))) * (1.0 / GLA_TAU)

    for ci in range(tt // c):
        rows = slice(ci * c, (ci + 1) * c)
        bcum = _dot_sel(tri, log_a[rows])
        b_last = bcum[c - 1:c]
        q_dec = q_all[rows] * jnp.exp(bcum)
        k_inv = k_all[rows] * jnp.exp(-bcum)
        k_end = k_all[rows] * jnp.exp(b_last - bcum)
        e_last = jnp.exp(b_last)
        for p in range(2):
            ls = slice(p * LANES, (p + 1) * LANES)
            qd_p, ki_p, ke_p, el_p = q_dec[:, ls], k_inv[:, ls], k_end[:, ls], e_last[:, ls]
            for hh in range(2):
                h = 2 * p + hh
                own = (lane >= hh * GLA_DK) & (lane < (hh + 1) * GLA_DK)
                qd_h = _bf(jnp.where(own, qd_p, 0.0))
                v_h = v_ref[0, rows, h * GLA_DV:(h + 1) * GLA_DV]
                att = jnp.where(causal, _dot_nt(qd_h, _bf(ki_p)), 0.0)
                st = st_ref[h]
                o = _dot(_bf(att), v_h) + _dot_nt(qd_h, _bf(st))
                st_ref[h] = st * el_p + jnp.where(own, _dot_tn(v_h, _bf(ke_p)), 0.0)
                o = o * lax.rsqrt(jnp.mean(o * o, axis=-1, keepdims=True) + NORM_EPS) * norm_ref[...]
                og = og_ref[0, rows, h * GLA_DV:(h + 1) * GLA_DV].astype(F32)
                o_ref[0, rows, h * GLA_DV:(h + 1) * GLA_DV] = (o * _silu(og)).astype(o_ref.dtype)


def _gla(p16, p32, wgu, bgate, norm, tt):
    b, s, _ = p16.shape
    full = lambda shape: pl.BlockSpec(shape, lambda bi, i: (0,) * len(shape))
    return pl.pallas_call(
        functools.partial(_gla_body, tt=tt),
        grid=(b, s // tt),
        in_specs=[pl.BlockSpec((1, tt, 512), lambda bi, i: (bi, i, P16_GLA_QK)),
                  pl.BlockSpec((1, tt, 512), lambda bi, i: (bi, i, P16_GLA_V)),
                  pl.BlockSpec((1, tt, 512), lambda bi, i: (bi, i, P16_GLA_OG)),
                  pl.BlockSpec((1, tt, 128), lambda bi, i: (bi, i, P32_MISC)),
                  full((128, 256)), full((1, 256)), full((1, 128))],
        out_specs=pl.BlockSpec((1, tt, 512), lambda bi, i: (bi, i, 0)),
        out_shape=jax.ShapeDtypeStruct((b, s, 512), BF16),
        scratch_shapes=[pltpu.VMEM((GLA_HEADS, GLA_DV, LANES), F32)],
        compiler_params=_cparams(("parallel", "arbitrary")),
        name="gla",
    )(p16, p16, p16, p32, wgu, bgate, norm)


def _gdn_body(q_ref, k_ref, v_ref, z_ref, misc_ref, conv_ref, alog_ref, dtb_ref, norm_ref, o_ref,
              ext_ref, st_ref, *, tt):
    c = CHUNK
    nc = tt // c

    @pl.when(pl.program_id(1) == 0)
    def _():
        st_ref[...] = jnp.zeros_like(st_ref)
        ext_ref[0:8, :] = jnp.zeros((8, 1536), F32)

    causal, strict = _tri_masks(c)
    tri = jnp.where(causal, 1.0, 0.0).astype(BF16)
    ones_cc = jnp.ones((c, c), BF16)
    eye = jnp.where(_iota2((c, c), 0) == _iota2((c, c), 1), 1.0, 0.0).astype(F32)

    ext_ref[8:8 + tt, 0:512] = q_ref[0].astype(F32)
    ext_ref[8:8 + tt, 512:1024] = k_ref[0].astype(F32)
    ext_ref[8:8 + tt, 1024:1536] = v_ref[0].astype(F32)
    conv = None
    for j in range(GDN_CONV):
        term = ext_ref[8 - (GDN_CONV - 1) + j:8 - (GDN_CONV - 1) + j + tt, :] * conv_ref[j:j + 1, :]
        conv = term if conv is None else conv + term
    ext_ref[0:8, :] = ext_ref[tt:tt + 8, :]
    qkv = _silu(conv)

    misc = misc_ref[0]
    beta_all = _sigmoid(misc)
    g_all = -jnp.exp(alog_ref[...]) * _softplus(misc + dtb_ref[...])

    l_list, pre = [], []
    for ci in range(nc):
        rows = slice(ci * c, (ci + 1) * c)
        gc_all = _dot_sel(tri, g_all[rows])
        for h in range(GDN_HEADS):
            ls = slice(h * LANES, (h + 1) * LANES)
            q = qkv[rows, ls]
            k = qkv[rows, 512 + h * LANES:512 + (h + 1) * LANES]
            v = qkv[rows, 1024 + h * LANES:1024 + (h + 1) * LANES]
            q = q * lax.rsqrt(jnp.sum(q * q, axis=-1, keepdims=True) + NORM_EPS) * (GDN_DK ** -0.5)
            k = k * lax.rsqrt(jnp.sum(k * k, axis=-1, keepdims=True) + NORM_EPS)
            beta = beta_all[rows, MISC_GDN_BETA + h:MISC_GDN_BETA + h + 1]
            gc = gc_all[:, MISC_GDN_A + h:MISC_GDN_A + h + 1]
            gc_row = _dot_sel(ones_cc, eye * gc)
            decay = jnp.where(causal, jnp.exp(jnp.minimum(gc - gc_row, 0.0)), 0.0)
            kb = k * beta
            kbf = _bf(k)
            l_list.append(jnp.where(strict, _dot_nt(_bf(kb), kbf) * decay, 0.0))
            att = _dot_nt(_bf(q), kbf) * decay
            eg = jnp.exp(gc)
            g_last = gc[c - 1:c]
            pre.append(dict(vb=v * beta, kbe=kb * eg, att=_bf(att), q_dec=_bf(q * eg),
                            k_end=_bf(k * jnp.exp(g_last - gc)), e_last=jnp.exp(g_last)))
    t_all = _unit_lower_inverse(jnp.stack(l_list, axis=0))

    for ci in range(nc):
        rows = slice(ci * c, (ci + 1) * c)
        for h in range(GDN_HEADS):
            n = ci * GDN_HEADS + h
            d = pre[n]
            tb = _bf(t_all[n])
            u = _dot(tb, _bf(d['vb']))
            w = _dot(tb, _bf(d['kbe']))
            st = st_ref[h]
            stb = _bf(st)
            v_new = u - _dot(_bf(w), stb)
            vnb = _bf(v_new)
            o = _dot(d['q_dec'], stb) + _dot(d['att'], vnb)
            st_ref[h] = st * d['e_last'] + _dot_tn(d['k_end'], vnb)
            o = o * lax.rsqrt(jnp.mean(o * o, axis=-1, keepdims=True) + NORM_EPS) * norm_ref[...]
            z = z_ref[0, rows, h * LANES:(h + 1) * LANES].astype(F32)
            o_ref[0, rows, h * LANES:(h + 1) * LANES] = (o * _silu(z)).astype(o_ref.dtype)


def _gdn(p16, p32, conv_w, alog, dtb, norm, tt):
    b, s, _ = p16.shape
    full = lambda shape: pl.BlockSpec(shape, lambda bi, i: (0,) * len(shape))
    return pl.pallas_call(
        functools.partial(_gdn_body, tt=tt),
        grid=(b, s // tt),
        in_specs=[pl.BlockSpec((1, tt, 512), lambda bi, i: (bi, i, P16_GDN_Q)),
                  pl.BlockSpec((1, tt, 512), lambda bi, i: (bi, i, P16_GDN_Q + 1)),
                  pl.BlockSpec((1, tt, 512), lambda bi, i: (bi, i, P16_GDN_Q + 2)),
                  pl.BlockSpec((1, tt, 512), lambda bi, i: (bi, i, P16_GDN_Z)),
                  pl.BlockSpec((1, tt, 128), lambda bi, i: (bi, i, P32_MISC)),
                  full((GDN_CONV, 1536)), full((1, 128)), full((1, 128)), full((1, 128))],
        out_specs=pl.BlockSpec((1, tt, 512), lambda bi, i: (bi, i, 0)),
        out_shape=jax.ShapeDtypeStruct((b, s, 512), BF16),
        scratch_shapes=[pltpu.VMEM((tt + 8, 1536), F32), pltpu.VMEM((GDN_HEADS, GDN_DK, GDN_DV), F32)],
        compiler_params=_cparams(("parallel", "arbitrary")),
        name="gdn",
    )(p16, p16, p16, p16, p32, conv_w, alog, dtb, norm)


def _rwkv_body(*refs, tt, has_vres):
    if has_vres:
        (r_ref, k_ref, v_ref, lat_ref, misc_ref, vfirst_ref, mu_ref, mulat_ref, w0_ref, wup_ref, a0_ref, aup_ref,
         gup_ref, kk_ref, ka_ref, rk_ref, lng_ref, lnb_ref, vmu_ref, v0_ref, vup_ref, hsum_ref,
         o_ref, ext_ref, extl_ref, extm_ref, st_ref) = refs
        vout_ref = None
    else:
        (r_ref, k_ref, v_ref, lat_ref, mu_ref, mulat_ref, w0_ref, wup_ref, a0_ref, aup_ref,
         gup_ref, kk_ref, ka_ref, rk_ref, lng_ref, lnb_ref, hsum_ref,
         o_ref, vout_ref, ext_ref, extl_ref, st_ref) = refs
    c = CHUNK
    nc = tt // c
    n_pairs = RWKV_HEADS // 2

    @pl.when(pl.program_id(1) == 0)
    def _():
        st_ref[...] = jnp.zeros_like(st_ref)
        ext_ref[0:8, :] = jnp.zeros((8, 1536), F32)
        extl_ref[0:8, :] = jnp.zeros((8, 256), F32)
        if has_vres:
            extm_ref[0:8, :] = jnp.zeros((8, 128), F32)

    ext_ref[8:8 + tt, 0:512] = r_ref[0].astype(F32)
    ext_ref[8:8 + tt, 512:1024] = k_ref[0].astype(F32)
    ext_ref[8:8 + tt, 1024:1536] = v_ref[0].astype(F32)
    cur = ext_ref[8:8 + tt, :]
    rkv = cur + (ext_ref[7:7 + tt, :] - cur) * mu_ref[...]
    ext_ref[0:8, :] = ext_ref[tt:tt + 8, :]
    extl_ref[8:8 + tt, :] = lat_ref[0]
    curl = extl_ref[8:8 + tt, :]
    lat = curl + (extl_ref[7:7 + tt, :] - curl) * mulat_ref[...]
    extl_ref[0:8, :] = extl_ref[tt:tt + 8, :]

    r = rkv[:, 0:512]
    k = rkv[:, 512:1024]
    v = rkv[:, 1024:1536]
    lat_wa = lat[:, 0:128]
    lat_g = lat[:, 128:256]
    w_pre = w0_ref[...] + _dot(_bf(jnp.tanh(lat_wa)), wup_ref[...])
    w = -_softplus(-w_pre) - 0.5
    logw = -jnp.exp(w)
    a = _sigmoid(a0_ref[...] + _dot(_bf(lat_wa), aup_ref[...]))
    g = _dot(_bf(_sigmoid(lat_g)), gup_ref[...])
    if has_vres:
        extm_ref[8:8 + tt, :] = misc_ref[0]
        curm = extm_ref[8:8 + tt, :]
        vlat = curm + (extm_ref[7:7 + tt, :] - curm) * vmu_ref[...]
        extm_ref[0:8, :] = extm_ref[tt:tt + 8, :]
        v_gate = _sigmoid(v0_ref[...] + _dot(_bf(vlat), vup_ref[...]))
        v = v + (vfirst_ref[0] - v) * v_gate
    else:
        vout_ref[0] = v

    hsum = hsum_ref[...]
    kkv = k * kk_ref[...]
    kk = kkv * lax.rsqrt(_dot_sel_r(kkv * kkv, hsum) + NORM_EPS)
    k = k * (1.0 + (a - 1.0) * ka_ref[...])
    bonus = _dot_sel_r(r * k * rk_ref[...], hsum) * v
    bvec = kk * a

    lane = _iota2((1, LANES), 1)
    first = lane < RWKV_N
    rowh = _iota2((LANES, LANES), 0) < RWKV_N
    colh = _iota2((LANES, LANES), 1) < RWKV_N
    blockdiag = rowh == colh
    causal, strict = _tri_masks(c)
    tri = jnp.where(causal, 1.0, 0.0).astype(BF16)

    l_list, pre = [], []
    for ci in range(nc):
        rows = slice(ci * c, (ci + 1) * c)
        lw = logw[rows]
        cum = _dot_sel(tri, lw)
        gam = jnp.exp(cum)
        inv = jnp.exp(-cum)
        gam_ex = jnp.exp(cum - lw)
        gam_c = gam[c - 1:c]
        a_t = -kk[rows] * gam_ex
        r_t = r[rows] * gam
        b_t = bvec[rows] * inv
        k_t = k[rows] * inv
        for p in range(n_pairs):
            ls = slice(p * LANES, (p + 1) * LANES)
            a_p, r_p, b_p, k_p = a_t[:, ls], r_t[:, ls], b_t[:, ls], k_t[:, ls]
            lhs = jnp.concatenate([jnp.where(first, a_p, 0.0), jnp.where(first, 0.0, a_p),
                                   jnp.where(first, r_p, 0.0), jnp.where(first, 0.0, r_p)], axis=0)
            lhs = _bf(lhs)
            m_b = _dot_nt(lhs, _bf(b_p))
            m_k = _dot_nt(lhs, _bf(k_p))
            heads = []
            for hh in range(2):
                l_list.append(jnp.where(strict, -m_b[hh * c:(hh + 1) * c], 0.0))
                heads.append(dict(a_ak=_bf(jnp.where(strict, m_k[hh * c:(hh + 1) * c], 0.0)),
                                  a_rb=_bf(jnp.where(causal, m_b[(2 + hh) * c:(3 + hh) * c], 0.0)),
                                  a_rk=_bf(jnp.where(causal, m_k[(2 + hh) * c:(3 + hh) * c], 0.0))))
            pre.append(dict(heads=heads, a_p=_bf(a_p), r_p=_bf(r_p), gam_c=gam_c[:, ls],
                            b_end=_bf(b_p * gam_c[:, ls]), k_end=_bf(k_p * gam_c[:, ls]),
                            v_p=_bf(v[rows, ls])))
    t_all = _unit_lower_inverse(jnp.stack(l_list, axis=0))

    for ci in range(nc):
        rows = slice(ci * c, (ci + 1) * c)
        for p in range(n_pairs):
            ls = slice(p * LANES, (p + 1) * LANES)
            d = pre[ci * n_pairs + p]
            st = st_ref[p]
            stb = _bf(st)
            x = _dot_nt(d['a_p'], stb)
            y0 = _dot_nt(d['r_p'], stb)
            us, ys = [], []
            for hh in range(2):
                hd = d['heads'][hh]
                tb = _bf(t_all[(ci * n_pairs + p) * 2 + hh])
                us.append(_dot(tb, _bf(x + _dot(hd['a_ak'], d['v_p']))))
            u_p = _bf(jnp.where(first, us[0], us[1]))
            for hh in range(2):
                hd = d['heads'][hh]
                ys.append(_dot(hd['a_rb'], u_p) + _dot(hd['a_rk'], d['v_p']))
            y = y0 + jnp.where(first, ys[0], ys[1])
            upd = _dot_tn(u_p, d['b_end']) + _dot_tn(d['v_p'], d['k_end'])
            st_ref[p] = st * d['gam_c'] + jnp.where(blockdiag, upd, 0.0)
            ext_ref[8 + ci * c:8 + (ci + 1) * c, p * LANES:(p + 1) * LANES] = y

    y = ext_ref[8:8 + tt, 0:512]
    mean = _dot_sel_r(y, hsum) * (1.0 / RWKV_N)
    yc = y - mean
    var = _dot_sel_r(yc * yc, hsum) * (1.0 / RWKV_N)
    yn = yc * lax.rsqrt(var + RWKV_LN_EPS) * lng_ref[...] + lnb_ref[...]
    o_ref[0] = ((yn + bonus) * g).astype(o_ref.dtype)


def _rwkv(p16, p32, v_first, weights, tt, has_vres):
    b, s, _ = p16.shape
    full = lambda shape: pl.BlockSpec(shape, lambda bi, i: (0,) * len(shape))
    tok = lambda w, blk: pl.BlockSpec((1, tt, w), lambda bi, i: (bi, i, blk))
    in_specs = [tok(512, P16_RWKV_R), tok(512, P16_RWKV_R + 1), tok(512, P16_RWKV_R + 2), tok(256, P32_RWKV_LAT)]
    args = [p16, p16, p16, p32]
    if has_vres:
        in_specs += [tok(128, P32_MISC), tok(512, 0)]
        args += [p32, v_first]
    in_specs += [full(w.shape) for w in weights]
    args += list(weights)
    out_specs = [pl.BlockSpec((1, tt, 512), lambda bi, i: (bi, i, 0))]
    out_shape = [jax.ShapeDtypeStruct((b, s, 512), BF16)]
    scratch = [pltpu.VMEM((tt + 8, 1536), F32), pltpu.VMEM((tt + 8, 256), F32)]
    if has_vres:
        scratch.append(pltpu.VMEM((tt + 8, 128), F32))
    else:
        out_specs.append(pl.BlockSpec((1, tt, 512), lambda bi, i: (bi, i, 0)))
        out_shape.append(jax.ShapeDtypeStruct((b, s, 512), F32))
    scratch.append(pltpu.VMEM((RWKV_HEADS // 2, LANES, LANES), F32))
    return pl.pallas_call(
        functools.partial(_rwkv_body, tt=tt, has_vres=has_vres),
        grid=(b, s // tt),
        in_specs=in_specs,
        out_specs=out_specs,
        out_shape=out_shape,
        scratch_shapes=scratch,
        compiler_params=_cparams(("parallel", "arbitrary")),
        name="rwkv7",
    )(*args)


def _layernorm(h, g, b):
    mu = jnp.mean(h, axis=-1, keepdims=True)
    hc = h - mu
    var = jnp.mean(hc * hc, axis=-1, keepdims=True)
    return hc * lax.rsqrt(var + LN_EPS) * g + b


def _merge_body(ya_ref, yb_ref, yc_ref, yd_ref, gate_ref, x_ref, wb_ref, wo_ref, g_ref, b_ref, o_ref):
    merged = None
    for n, y_ref in enumerate((ya_ref, yb_ref, yc_ref, yd_ref)):
        gate = _sigmoid(gate_ref[0, :, n * D_MODEL:(n + 1) * D_MODEL].astype(F32))
        term = gate * _dot(y_ref[0], wb_ref[n])
        merged = term if merged is None else merged + term
    mix = _dot(_bf(merged), wo_ref[...])
    o_ref[0] = _layernorm(DEEPNORM_ALPHA * x_ref[0] + mix, g_ref[...], b_ref[...])


def _merge(ys, p16, x, wb, wo, g, bb, tt):
    b, s, d = x.shape
    full = lambda shape: pl.BlockSpec(shape, lambda bi, i: (0,) * len(shape))
    ytok = pl.BlockSpec((1, tt, 512), lambda bi, i: (bi, i, 0))
    return pl.pallas_call(
        _merge_body,
        grid=(b, s // tt),
        in_specs=[ytok, ytok, ytok, ytok,
                  pl.BlockSpec((1, tt, N_BRANCH * D_MODEL), lambda bi, i: (bi, i, P16_GATES)),
                  pl.BlockSpec((1, tt, d), lambda bi, i: (bi, i, 0)),
                  full((N_BRANCH, BRANCH_WIDTH, D_MODEL)), full((D_MODEL, D_MODEL)), full((1, d)), full((1, d))],
        out_specs=pl.BlockSpec((1, tt, d), lambda bi, i: (bi, i, 0)),
        out_shape=jax.ShapeDtypeStruct((b, s, d), F32),
        compiler_params=_cparams(("parallel", "parallel")),
        name="merge_ln1",
    )(*ys, p16, x, wb, wo, g, bb)


def _moe_body(x_ref, rwt_ref, rbias_ref, wg_ref, wu_ref, wd_ref, g_ref, b_ref, o_ref, xb_ref, comb_ref, acc_ref):
    e = pl.program_id(2)
    tt = x_ref.shape[1]

    @pl.when(e == 0)
    def _():
        x = x_ref[0]
        xb_ref[...] = x.astype(BF16)
        acc_ref[...] = jnp.zeros_like(acc_ref)
        scores = _sigmoid(_dot3_nt(rwt_ref[...], x))
        biased = scores + rbias_ref[...]
        row = _iota2((N_EXPERTS, tt), 0).astype(F32)
        row_group = (_iota2((N_EXPERTS, tt), 0) >> 2).astype(F32)
        neg = -jnp.inf
        gs = []
        for gi in range(N_GROUPS):
            rows = [biased[gi * GROUP_SIZE + j:gi * GROUP_SIZE + j + 1] for j in range(GROUP_SIZE)]
            best = None
            for i0 in range(GROUP_SIZE):
                for i1 in range(i0 + 1, GROUP_SIZE):
                    pair = rows[i0] + rows[i1]
                    best = pair if best is None else jnp.maximum(best, pair)
            gs.append(best)
        gmax = functools.reduce(jnp.maximum, gs)
        best_group = jnp.full((1, tt), float(N_GROUPS), F32)
        for gi in reversed(range(N_GROUPS)):
            best_group = jnp.where(gs[gi] == gmax, float(gi), best_group)
        masked = jnp.where(row_group == best_group, biased, neg)
        m1 = jnp.max(masked, axis=0, keepdims=True)
        i1 = jnp.min(jnp.where(masked == m1, row, float(N_EXPERTS)), axis=0, keepdims=True)
        masked2 = jnp.where(row == i1, neg, masked)
        m2 = jnp.max(masked2, axis=0, keepdims=True)
        i2 = jnp.min(jnp.where(masked2 == m2, row, float(N_EXPERTS)), axis=0, keepdims=True)
        s1 = jnp.sum(jnp.where(row == i1, scores, 0.0), axis=0, keepdims=True)
        s2 = jnp.sum(jnp.where(row == i2, scores, 0.0), axis=0, keepdims=True)
        tot = s1 + s2
        comb_t = jnp.where(row == i1, s1 / tot, 0.0) + jnp.where(row == i2, s2 / tot, 0.0)
        eye = jnp.where(_iota2((N_EXPERTS, N_EXPERTS), 0) == _iota2((N_EXPERTS, N_EXPERTS), 1), 1.0, 0.0).astype(BF16)
        hi, lo = _split2(comb_t)
        comb_ref[...] = _dot_tn(hi, eye) + _dot_tn(lo, eye)

    xb = xb_ref[...]
    hid = _silu(_dot(xb, wg_ref[0])) * _dot(xb, wu_ref[0])
    ce = jnp.sum(jnp.where(_iota2((1, N_EXPERTS), 1) == e, comb_ref[...], 0.0), axis=-1, keepdims=True)
    acc_ref[...] += ce * _dot(_bf(hid), wd_ref[0])

    @pl.when(e == N_EXPERTS - 1)
    def _():
        o_ref[0] = _layernorm(DEEPNORM_ALPHA * x_ref[0] + acc_ref[...], g_ref[...], b_ref[...])


def _moe(x, rwt, rbias, wg, wu, wd, g, bb, tt):
    b, s, d = x.shape
    full = lambda shape: pl.BlockSpec(shape, lambda bi, i, e: (0,) * len(shape))
    return pl.pallas_call(
        _moe_body,
        grid=(b, s // tt, N_EXPERTS),
        in_specs=[pl.BlockSpec((1, tt, d), lambda bi, i, e: (bi, i, 0)),
                  full((N_EXPERTS, d)), full((N_EXPERTS, 1)),
                  pl.BlockSpec((1, d, D_EXPERT), lambda bi, i, e: (e, 0, 0)),
                  pl.BlockSpec((1, d, D_EXPERT), lambda bi, i, e: (e, 0, 0)),
                  pl.BlockSpec((1, D_EXPERT, d), lambda bi, i, e: (e, 0, 0)),
                  full((1, d)), full((1, d))],
        out_specs=pl.BlockSpec((1, tt, d), lambda bi, i, e: (bi, i, 0)),
        out_shape=jax.ShapeDtypeStruct((b, s, d), F32),
        scratch_shapes=[pltpu.VMEM((tt, d), BF16), pltpu.VMEM((tt, N_EXPERTS), F32), pltpu.VMEM((tt, d), F32)],
        compiler_params=_cparams(("parallel", "parallel", "arbitrary")),
        name="moe_ln2",
    )(x, rwt, rbias, wg, wu, wd, g, bb)


def _cols(w, name):
    o, n = _OFF[name]
    return w[:, o:o + n]


def _pack_inproj(w_in_l, w_vres_l):
    d = w_in_l.shape[0]
    w16 = jnp.concatenate([_cols(w_in_l, "gates"), _cols(w_in_l, "gla_q"), _cols(w_in_l, "gla_k"),
                           _cols(w_in_l, "gla_v"), _cols(w_in_l, "gla_og"), _cols(w_in_l, "gdn_qkv"),
                           _cols(w_in_l, "gdn_z"), _cols(w_in_l, "rwkv_rkv")], axis=1).astype(BF16)
    vres = jnp.zeros((d, RWKV_V_RANK), F32) if w_vres_l is None else w_vres_l
    misc = jnp.concatenate([_cols(w_in_l, "k_rope"), _cols(w_in_l, "gla_gl"), _cols(w_in_l, "gdn_beta"),
                            _cols(w_in_l, "gdn_a"), vres], axis=1)
    misc = jnp.pad(misc, ((0, 0), (0, LANES - misc.shape[1])))
    w32 = jnp.concatenate([_cols(w_in_l, "q_lat"), _cols(w_in_l, "rwkv_lat"), _cols(w_in_l, "c_kv"), misc],
                          axis=1).astype(BF16)
    return w16, w32


def _row_pad(w, start, total=LANES):
    return jnp.pad(w, ((start, total - start - w.shape[0]), (0, 0)))


def _lane_row(vec, start, total=LANES):
    return jnp.pad(vec, (start, total - start - vec.shape[0])).reshape(1, total)


def _pack_mla(w_uq, w_ukv):
    half = MLA_ROPE // 2
    wq = w_uq.reshape(MLA_Q_RANK, MLA_HEADS, MLA_NOPE + MLA_ROPE)
    nope, r1, r2 = wq[..., :MLA_NOPE], wq[..., MLA_NOPE:MLA_NOPE + half], wq[..., MLA_NOPE + half:]
    pad = jnp.zeros((MLA_Q_RANK, MLA_HEADS, LANES - MLA_NOPE - MLA_ROPE), F32)
    wqa = jnp.concatenate([nope, r1, r2, pad], axis=-1).reshape(MLA_Q_RANK, MLA_HEADS * LANES)
    wqb = jnp.concatenate([jnp.zeros_like(nope), -r2, r1, pad], axis=-1).reshape(MLA_Q_RANK, MLA_HEADS * LANES)
    wkv = w_ukv.reshape(MLA_KV_RANK, MLA_HEADS, MLA_NOPE + MLA_V)
    wuk = jnp.pad(wkv[..., :MLA_NOPE], ((0, 0), (0, 0), (0, LANES - MLA_NOPE))).reshape(MLA_KV_RANK, MLA_HEADS * LANES)
    wuv = wkv[..., MLA_NOPE:].reshape(MLA_KV_RANK, MLA_HEADS * MLA_V).T
    ra = np.zeros((LANES, LANES), np.float32)
    rb = np.zeros((LANES, LANES), np.float32)
    for j in range(half):
        ra[MISC_KROPE + j, MLA_NOPE + j] = 1.0
        ra[MISC_KROPE + half + j, MLA_NOPE + half + j] = 1.0
        rb[MISC_KROPE + half + j, MLA_NOPE + j] = -1.0
        rb[MISC_KROPE + j, MLA_NOPE + half + j] = 1.0
    inv_freq = ROPE_THETA ** (-jnp.arange(half, dtype=F32) / half)
    freq = jnp.concatenate([jnp.zeros((MLA_NOPE,), F32), inv_freq, inv_freq,
                            jnp.zeros((LANES - MLA_NOPE - MLA_ROPE,), F32)]).reshape(1, LANES)
    return (wqa.astype(BF16), wqb.astype(BF16), wuk.astype(BF16), wuv.astype(BF16),
            jnp.asarray(ra, BF16), jnp.asarray(rb, BF16), freq)


def _head_sum_matrix():
    idx = np.arange(RWKV_WIDTH) // RWKV_N
    return jnp.asarray((idx[:, None] == idx[None, :]).astype(np.float32), BF16)


def _tile(s, pref):
    t = min(pref, s)
    assert s % t == 0 and t % CHUNK == 0
    return t


def kernel(x, positions, router_w, w_in, w_in_vres, mla_q_norm, mla_w_uq, mla_kv_norm, mla_w_ukv,
           gla_w_gate_up, gla_b_gate, gla_norm, gdn_conv, gdn_a_log, gdn_dt_bias, gdn_norm,
           rwkv_mu, rwkv_w0, rwkv_w_up, rwkv_a0, rwkv_a_up, rwkv_g_up, rwkv_k_k, rwkv_k_a, rwkv_r_k,
           rwkv_ln_g, rwkv_ln_b, rwkv_vres_mu, rwkv_v0, rwkv_v_up, w_branch, w_out,
           ln1_g, ln1_b, ln2_g, ln2_b, router_bias, moe_w_gate, moe_w_up, moe_w_down):
    b, s, d = x.shape
    assert d == D_MODEL and w_in.shape[2] == IN_COLS
    pos3 = positions.reshape(b, s, 1)
    rwt = router_w.T
    hsum = _head_sum_matrix()
    row = lambda v: v.reshape(1, -1)
    t_proj = _tile(s, 1024)
    t_scan = _tile(s, 256)
    t_attn = _tile(s, 512)
    t_merge = _tile(s, 256)
    t_moe = _tile(s, 1024)

    v_first = None
    for l in range(DEPTH):
        w16, w32 = _pack_inproj(w_in[l], None if l == 0 else w_in_vres[l - 1])
        p16 = _inproj(x, w16, BF16, t_proj, 1024)
        p32 = _inproj(x, w32, F32, t_proj, P32_WIDTH)

        wqa, wqb, wuk, wuv, wkra, wkrb, freq = _pack_mla(mla_w_uq[l], mla_w_ukv[l])
        q, k, v = _mla_prep(p32, pos3, row(mla_q_norm[l]), wqa, wqb, row(mla_kv_norm[l]), wuk, wuv, wkra, wkrb,
                            freq, t_attn)
        y_a = _mla_attn(q, k, v, t_attn)

        wgu = _row_pad(gla_w_gate_up[l], MISC_GLA_GL)
        y_b = _gla(p16, p32, wgu, row(gla_b_gate[l]), row(gla_norm[l]), t_scan)

        y_c = _gdn(p16, p32, gdn_conv[l], _lane_row(gdn_a_log[l], MISC_GDN_A), _lane_row(gdn_dt_bias[l], MISC_GDN_A),
                   row(gdn_norm[l]), t_scan)

        mu = rwkv_mu[l]
        weights = [row(mu[:1536]), row(mu[1536:]), row(rwkv_w0[l]),
                   _row_pad(rwkv_w_up[l], 0).astype(BF16), row(rwkv_a0[l]),
                   _row_pad(rwkv_a_up[l], RWKV_W_RANK).astype(BF16), rwkv_g_up[l].astype(BF16),
                   row(rwkv_k_k[l]), row(rwkv_k_a[l]), row(rwkv_r_k[l]), row(rwkv_ln_g[l]), row(rwkv_ln_b[l])]
        if l == 0:
            y_d, v_first = _rwkv(p16, p32, None, weights + [hsum], t_scan, False)
        else:
            weights += [_lane_row(rwkv_vres_mu[l - 1], MISC_VRES), row(rwkv_v0[l - 1]),
                        _row_pad(rwkv_v_up[l - 1], MISC_VRES).astype(BF16)]
            (y_d,) = _rwkv(p16, p32, v_first, weights + [hsum], t_scan, True)

        x = _merge((y_a, y_b, y_c, y_d), p16, x, w_branch[l].astype(BF16), w_out[l].astype(BF16),
                   row(ln1_g[l]), row(ln1_b[l]), t_merge)
        x = _moe(x, rwt, router_bias[l].reshape(N_EXPERTS, 1), moe_w_gate[l].astype(BF16),
                 moe_w_up[l].astype(BF16), moe_w_down[l].astype(BF16), row(ln2_g[l]), row(ln2_b[l]), t_moe)
    return x
```

```python
import functools
import math

import jax
import jax.numpy as jnp
import numpy as np
from jax import lax
from jax.experimental import pallas as pl
from jax.experimental.pallas import tpu as pltpu

F32 = jnp.float32
BF16 = jnp.bfloat16

LANES = 128
VMEM_LIMIT = 56 * 1024 * 1024

D_MODEL = 1024
DEPTH = 2
MLA_HEADS, MLA_NOPE, MLA_ROPE, MLA_V = 8, 64, 32, 64
MLA_Q_RANK, MLA_KV_RANK = 256, 128
MLA_VT_ROWS = MLA_V + 16
ROPE_THETA = 10000.0
GLA_HEADS, GLA_DK, GLA_DV, GLA_GATE_RANK, GLA_TAU = 4, 64, 128, 16, 16.0
GDN_HEADS, GDN_DK, GDN_DV, GDN_CONV = 4, 128, 128, 4
RWKV_HEADS, RWKV_N = 8, 64
RWKV_W_RANK, RWKV_A_RANK, RWKV_V_RANK, RWKV_G_RANK = 64, 64, 32, 128
RWKV_LN_EPS = 64e-5
CHUNK = 64
N_BRANCH, BRANCH_WIDTH = 4, 512
N_EXPERTS, N_GROUPS, TOP_K, D_EXPERT = 16, 4, 2, 512
GROUP_SIZE = N_EXPERTS // N_GROUPS
DEEPNORM_ALPHA = (2.0 * DEPTH) ** 0.25
LN_EPS = 1e-5
NORM_EPS = 1e-6
RWKV_WIDTH = RWKV_HEADS * RWKV_N

_OFF = {}
_o = 0
for _name, _w in (("q_lat", MLA_Q_RANK), ("c_kv", MLA_KV_RANK), ("k_rope", MLA_ROPE),
                  ("gla_q", 256), ("gla_k", 256), ("gla_v", 512), ("gla_gl", GLA_GATE_RANK), ("gla_og", 512),
                  ("gdn_qkv", 1536), ("gdn_beta", GDN_HEADS), ("gdn_a", GDN_HEADS), ("gdn_z", 512),
                  ("rwkv_rkv", 1536), ("rwkv_lat", 256), ("gates", N_BRANCH * D_MODEL)):
    _OFF[_name] = (_o, _w)
    _o += _w
IN_COLS = _o

MISC_KROPE = 0
MISC_GLA_GL = 32
MISC_GDN_BETA = 48
MISC_GDN_A = 52
MISC_VRES = 56

P16_WIDTH = 9216
P16_GATES = 0
P16_GLA_QK = 8
P16_GLA_V = 9
P16_GLA_OG = 10
P16_GDN_Q = 11
P16_GDN_Z = 14
P16_RWKV_R = 15
P32_WIDTH = 768
P32_QLAT = 0
P32_RWKV_LAT = 1
P32_CKV = 4
P32_MISC = 5


def _cparams(sem):
    return pltpu.CompilerParams(dimension_semantics=sem, vmem_limit_bytes=VMEM_LIMIT)


def _dot(a, b):
    return jnp.dot(a, b, preferred_element_type=F32)


def _dot_nt(a, b):
    return lax.dot_general(a, b, (((1,), (1,)), ((), ())), preferred_element_type=F32)


def _dot_tn(a, b):
    return lax.dot_general(a, b, (((0,), (0,)), ((), ())), preferred_element_type=F32)


def _bmm(a, b):
    return jnp.einsum('bij,bjk->bik', a, b, preferred_element_type=F32)


def _bf(x):
    return x.astype(BF16)


def _split2(x):
    hi = x.astype(BF16)
    lo = (x - hi.astype(F32)).astype(BF16)
    return hi, lo


def _dot_sel(m_bf, x):
    hi, lo = _split2(x)
    return _dot(m_bf, hi) + _dot(m_bf, lo)


def _dot_sel_r(x, m_bf):
    hi, lo = _split2(x)
    return _dot(hi, m_bf) + _dot(lo, m_bf)


def _dot3(a, b):
    ah, al = _split2(a)
    bh, bl = _split2(b)
    return _dot(ah, bh) + _dot(al, bh) + _dot(ah, bl)


def _dot3_nt(a, b):
    ah, al = _split2(a)
    bh, bl = _split2(b)
    return _dot_nt(ah, bh) + _dot_nt(al, bh) + _dot_nt(ah, bl)


def _sigmoid(x):
    return 1.0 / (1.0 + jnp.exp(-x))


def _silu(x):
    return x * _sigmoid(x)


def _softplus(x):
    return jnp.maximum(x, 0.0) + jnp.log(1.0 + jnp.exp(-jnp.abs(x)))


def _iota2(shape, dim):
    return lax.broadcasted_iota(jnp.int32, shape, dim)


def _tri_masks(c):
    r = _iota2((c, c), 0)
    q = _iota2((c, c), 1)
    return r >= q, r > q


def _unit_lower_inverse(l_mat):
    nb, c, _ = l_mat.shape
    r = _iota2((c, c), 0)
    q = _iota2((c, c), 1)
    pair = jnp.where((r >> 1) == (q >> 1), 1.0, 0.0).astype(F32)
    t = jnp.where(r == q, 1.0, 0.0).astype(F32)[None] - l_mat * pair[None]
    s = 2
    while s < c:
        sh = s.bit_length()
        same = (r >> sh) == (q >> sh)
        m = jnp.where(same, jnp.where((r & s) != 0, jnp.where((q & s) == 0, 1.0, 0.0), 0.0), 0.0)
        ls = l_mat * m[None]
        tb = _bf(t)
        x = _bmm(_bf(_bmm(tb, _bf(ls))), tb)
        t = t - x
        s *= 2
    return t


def _inproj_body(x_ref, w_ref, o_ref, xb_ref):
    @pl.when(pl.program_id(2) == 0)
    def _():
        xb_ref[...] = x_ref[0].astype(BF16)

    o_ref[0] = _dot(xb_ref[...], w_ref[...]).astype(o_ref.dtype)


def _inproj(x, w, out_dtype, tm, tn):
    b, s, d = x.shape
    n = w.shape[1]
    return pl.pallas_call(
        _inproj_body,
        grid=(b, s // tm, n // tn),
        in_specs=[pl.BlockSpec((1, tm, d), lambda bi, i, j: (bi, i, 0)),
                  pl.BlockSpec((d, tn), lambda bi, i, j: (0, j))],
        out_specs=pl.BlockSpec((1, tm, tn), lambda bi, i, j: (bi, i, j)),
        out_shape=jax.ShapeDtypeStruct((b, s, n), out_dtype),
        scratch_shapes=[pltpu.VMEM((tm, d), BF16)],
        compiler_params=_cparams(("parallel", "parallel", "arbitrary")),
        name="inproj",
    )(x, w)


def _mla_prep_body(qlat_ref, ckv_ref, misc_ref, pos_ref, qnorm_ref, wqa_ref, wqb_ref, kvnorm_ref,
                   wuk_ref, wuv_ref, wkra_ref, wkrb_ref, freq_ref, q_ref, k_ref, v_ref):
    scale = (MLA_NOPE + MLA_ROPE) ** -0.5 * math.log2(math.e)
    ang = pos_ref[0].astype(F32) * freq_ref[...]
    cos_t = jnp.cos(ang)
    sin_t = jnp.sin(ang)

    ql = qlat_ref[0]
    qn = ql * lax.rsqrt(jnp.mean(ql * ql, axis=-1, keepdims=True) + NORM_EPS) * qnorm_ref[...]
    qnb = _bf(qn)
    qa = _dot(qnb, wqa_ref[...])
    qb = _dot(qnb, wqb_ref[...])

    ck = ckv_ref[0]
    kvn = ck * lax.rsqrt(jnp.mean(ck * ck, axis=-1, keepdims=True) + NORM_EPS) * kvnorm_ref[...]
    kvb = _bf(kvn)
    kn = _dot(kvb, wuk_ref[...])
    vt = _dot_nt(wuv_ref[...], kvb)
    misc = misc_ref[0]
    kr = _dot_sel_r(misc, wkra_ref[...]) * cos_t + _dot_sel_r(misc, wkrb_ref[...]) * sin_t

    for h in range(MLA_HEADS):
        sl = slice(h * LANES, (h + 1) * LANES)
        q_ref[0, h] = ((qa[:, sl] * cos_t + qb[:, sl] * sin_t) * scale).astype(BF16)
        k_ref[0, h] = (kn[:, sl] + kr).astype(BF16)
    t = vt.shape[1]
    ones = jnp.ones((MLA_VT_ROWS - MLA_V, t), F32)
    for h in range(MLA_HEADS):
        v_ref[0, h, 0] = jnp.concatenate([vt[h * MLA_V:(h + 1) * MLA_V], ones], axis=0).astype(BF16)


def _mla_prep(p32, pos3, qnorm, wqa, wqb, kvnorm, wuk, wuv, wkra, wkrb, freq, tt):
    b, s, _ = p32.shape
    full = lambda shape: pl.BlockSpec(shape, lambda bi, i: (0,) * len(shape))
    return pl.pallas_call(
        _mla_prep_body,
        grid=(b, s // tt),
        in_specs=[pl.BlockSpec((1, tt, 256), lambda bi, i: (bi, i, P32_QLAT)),
                  pl.BlockSpec((1, tt, 128), lambda bi, i: (bi, i, P32_CKV)),
                  pl.BlockSpec((1, tt, 128), lambda bi, i: (bi, i, P32_MISC)),
                  pl.BlockSpec((1, tt, 1), lambda bi, i: (bi, i, 0)),
                  full((1, 256)), full((256, 1024)), full((256, 1024)), full((1, 128)),
                  full((128, 1024)), full((512, 128)), full((128, 128)), full((128, 128)), full((1, 128))],
        out_specs=[pl.BlockSpec((1, MLA_HEADS, tt, 128), lambda bi, i: (bi, 0, i, 0)),
                   pl.BlockSpec((1, MLA_HEADS, tt, 128), lambda bi, i: (bi, 0, i, 0)),
                   pl.BlockSpec((1, MLA_HEADS, 1, MLA_VT_ROWS, tt), lambda bi, i: (bi, 0, i, 0, 0))],
        out_shape=[jax.ShapeDtypeStruct((b, MLA_HEADS, s, 128), BF16),
                   jax.ShapeDtypeStruct((b, MLA_HEADS, s, 128), BF16),
                   jax.ShapeDtypeStruct((b, MLA_HEADS, s // tt, MLA_VT_ROWS, tt), BF16)],
        compiler_params=_cparams(("parallel", "parallel")),
        name="mla_prep",
    )(p32, p32, p32, pos3, qnorm, wqa, wqb, kvnorm, wuk, wuv, wkra, wkrb, freq)


def _mla_attn_body(q_ref, k_ref, v_ref, o_ref, m_ref, acc_ref, s_ref, *, tq):
    i = pl.program_id(2)
    for hh in range(2):
        m_ref[hh] = jnp.full((1, tq), -jnp.inf, F32)
        acc_ref[hh] = jnp.zeros((MLA_VT_ROWS, tq), F32)

    def scores(hh, j):
        start = pl.multiple_of(j * tq, tq)
        return _dot_nt(k_ref[0, hh, pl.ds(start, tq), :], q_ref[0, hh])

    def consume(hh, j, masked):
        st = s_ref[hh]
        if masked:
            st = jnp.where(_iota2((tq, tq), 0) <= _iota2((tq, tq), 1), st, -jnp.inf)
        m_old = m_ref[hh]
        m_new = jnp.maximum(m_old, jnp.max(st, axis=0, keepdims=True))
        p = jnp.exp2(st - m_new)
        acc_ref[hh] = acc_ref[hh] * jnp.exp2(m_old - m_new) + _dot(v_ref[0, hh, j], _bf(p))
        m_ref[hh] = m_new

    s_ref[0] = scores(0, 0)

    def loop_body(j, carry):
        s_ref[1] = scores(1, j)
        consume(0, j, False)
        s_ref[0] = scores(0, j + 1)
        consume(1, j, False)
        return carry

    lax.fori_loop(0, i, loop_body, 0)
    s_ref[1] = scores(1, i)
    consume(0, i, True)
    consume(1, i, True)
    outs = []
    for hh in range(2):
        a = acc_ref[hh]
        outs.append(a[0:MLA_V] / a[MLA_V:MLA_V + 1])
    o_ref[0] = jnp.concatenate(outs, axis=0).T.astype(o_ref.dtype)


def _mla_attn(q, k, v, tq):
    b, h, s, _ = q.shape
    return pl.pallas_call(
        functools.partial(_mla_attn_body, tq=tq),
        grid=(b, h // 2, s // tq),
        in_specs=[pl.BlockSpec((1, 2, tq, 128), lambda bi, p, i: (bi, p, i, 0)),
                  pl.BlockSpec((1, 2, s, 128), lambda bi, p, i: (bi, p, 0, 0)),
                  pl.BlockSpec((1, 2, s // tq, MLA_VT_ROWS, tq), lambda bi, p, i: (bi, p, 0, 0, 0))],
        out_specs=pl.BlockSpec((1, tq, 128), lambda bi, p, i: (bi, i, p)),
        out_shape=jax.ShapeDtypeStruct((b, s, h * MLA_V), BF16),
        scratch_shapes=[pltpu.VMEM((2, 1, tq), F32), pltpu.VMEM((2, MLA_VT_ROWS, tq), F32),
                        pltpu.VMEM((2, tq, tq), F32)],
        compiler_params=_cparams(("parallel", "parallel", "arbitrary")),
        name="mla_attn",
    )(q, k, v)


def _gla_body(qk_ref, v_ref, og_ref, misc_ref, wgu_ref, bgate_ref, norm_ref, o_ref, st_ref, *, tt):
    c = CHUNK

    @pl.when(pl.program_id(1) == 0)
    def _():
        st_ref[...] = jnp.zeros_like(st_ref)

    causal, _ = _tri_masks(c)
    tri = jnp.where(causal, 1.0, 0.0).astype(BF16)
    lane = _iota2((1, LANES), 1)
    qk = qk_ref[0].astype(F32)
    q_all = qk[:, :256] * (GLA_DK ** -0.5)
    k_all = qk[:, 256:]
    ga = _dot3(misc_ref[0], wgu_ref[...]) + bgate_ref[...]
    log_a = (jnp.minimum(ga, 0.0) - jnp.log(1.0 + jnp.exp(-jnp.abs(ga)))) * (1.0 / GLA_TAU)

    for ci in range(tt // c):
        rows = slice(ci * c, (ci + 1) * c)
        bcum = _dot_sel(tri, log_a[rows])
        b_last = bcum[c - 1:c]
        q_dec = q_all[rows] * jnp.exp(bcum)
        k_inv = k_all[rows] * jnp.exp(-bcum)
        k_end = k_all[rows] * jnp.exp(b_last - bcum)
        e_last = jnp.exp(b_last)
        for p in range(2):
            ls = slice(p * LANES, (p + 1) * LANES)
            qd_p, ki_p, ke_p, el_p = q_dec[:, ls], k_inv[:, ls], k_end[:, ls], e_last[:, ls]
            for hh in range(2):
                h = 2 * p + hh
                own = (lane >= hh * GLA_DK) & (lane < (hh + 1) * GLA_DK)
                qd_h = _bf(jnp.where(own, qd_p, 0.0))
                v_h = v_ref[0, rows, h * GLA_DV:(h + 1) * GLA_DV]
                att = jnp.where(causal, _dot_nt(qd_h, _bf(ki_p)), 0.0)
                st = st_ref[h]
                o = _dot(_bf(att), v_h) + _dot_nt(qd_h, _bf(st))
                st_ref[h] = st * el_p + jnp.where(own, _dot_tn(v_h, _bf(ke_p)), 0.0)
                o = o * lax.rsqrt(jnp.mean(o * o, axis=-1, keepdims=True) + NORM_EPS) * norm_ref[...]
                og = og_ref[0, rows, h * GLA_DV:(h + 1) * GLA_DV].astype(F32)
                o_ref[0, rows, h * GLA_DV:(h + 1) * GLA_DV] = (o * _silu(og)).astype(o_ref.dtype)


def _gla(p16, p32, wgu, bgate, norm, tt):
    b, s, _ = p16.shape
    full = lambda shape: pl.BlockSpec(shape, lambda bi, i: (0,) * len(shape))
    return pl.pallas_call(
        functools.partial(_gla_body, tt=tt),
        grid=(b, s // tt),
        in_specs=[pl.BlockSpec((1, tt, 512), lambda bi, i: (bi, i, P16_GLA_QK)),
                  pl.BlockSpec((1, tt, 512), lambda bi, i: (bi, i, P16_GLA_V)),
                  pl.BlockSpec((1, tt, 512), lambda bi, i: (bi, i, P16_GLA_OG)),
                  pl.BlockSpec((1, tt, 128), lambda bi, i: (bi, i, P32_MISC)),
                  full((128, 256)), full((1, 256)), full((1, 128))],
        out_specs=pl.BlockSpec((1, tt, 512), lambda bi, i: (bi, i, 0)),
        out_shape=jax.ShapeDtypeStruct((b, s, 512), BF16),
        scratch_shapes=[pltpu.VMEM((GLA_HEADS, GLA_DV, LANES), F32)],
        compiler_params=_cparams(("parallel", "arbitrary")),
        name="gla",
    )(p16, p16, p16, p32, wgu, bgate, norm)


def _gdn_body(q_ref, k_ref, v_ref, z_ref, misc_ref, conv_ref, alog_ref, dtb_ref, norm_ref, o_ref,
              ext_ref, st_ref, *, tt):
    c = CHUNK
    nc = tt // c

    @pl.when(pl.program_id(1) == 0)
    def _():
        st_ref[...] = jnp.zeros_like(st_ref)
        ext_ref[0:8, :] = jnp.zeros((8, 1536), F32)

    causal, strict = _tri_masks(c)
    tri = jnp.where(causal, 1.0, 0.0).astype(BF16)
    ones_cc = jnp.ones((c, c), BF16)
    eye = jnp.where(_iota2((c, c), 0) == _iota2((c, c), 1), 1.0, 0.0).astype(F32)

    ext_ref[8:8 + tt, 0:512] = q_ref[0].astype(F32)
    ext_ref[8:8 + tt, 512:1024] = k_ref[0].astype(F32)
    ext_ref[8:8 + tt, 1024:1536] = v_ref[0].astype(F32)
    conv = None
    for j in range(GDN_CONV):
        term = ext_ref[8 - (GDN_CONV - 1) + j:8 - (GDN_CONV - 1) + j + tt, :] * conv_ref[j:j + 1, :]
        conv = term if conv is None else conv + term
    ext_ref[0:8, :] = ext_ref[tt:tt + 8, :]
    qkv = _silu(conv)

    misc = misc_ref[0]
    beta_all = _sigmoid(misc)
    g_all = -jnp.exp(alog_ref[...]) * _softplus(misc + dtb_ref[...])

    l_list, pre = [], []
    for ci in range(nc):
        rows = slice(ci * c, (ci + 1) * c)
        gc_all = _dot_sel(tri, g_all[rows])
        for h in range(GDN_HEADS):
            ls = slice(h * LANES, (h + 1) * LANES)
            q = qkv[rows, ls]
            k = qkv[rows, 512 + h * LANES:512 + (h + 1) * LANES]
            v = qkv[rows, 1024 + h * LANES:1024 + (h + 1) * LANES]
            q = q * lax.rsqrt(jnp.sum(q * q, axis=-1, keepdims=True) + NORM_EPS) * (GDN_DK ** -0.5)
            k = k * lax.rsqrt(jnp.sum(k * k, axis=-1, keepdims=True) + NORM_EPS)
            beta = beta_all[rows, MISC_GDN_BETA + h:MISC_GDN_BETA + h + 1]
            gc = gc_all[:, MISC_GDN_A + h:MISC_GDN_A + h + 1]
            gc_row = _dot_sel(ones_cc, eye * gc)
            decay = jnp.where(causal, jnp.exp(jnp.minimum(gc - gc_row, 0.0)), 0.0)
            kb = k * beta
            kbf = _bf(k)
            l_list.append(jnp.where(strict, _dot_nt(_bf(kb), kbf) * decay, 0.0))
            att = _dot_nt(_bf(q), kbf) * decay
            eg = jnp.exp(gc)
            g_last = gc[c - 1:c]
            pre.append(dict(rhs=_bf(jnp.concatenate([v * beta, kb * eg], axis=1)), att=_bf(att), q_dec=q * eg,
                            k_end=_bf(k * jnp.exp(g_last - gc)), e_last=jnp.exp(g_last)))
    t_all = _unit_lower_inverse(jnp.stack(l_list, axis=0))

    stack = lambda key: jnp.stack([d[key] for d in pre], axis=0)
    uw = _bf(_bmm(_bf(t_all), stack('rhs')))
    ab = _bmm(stack('att'), uw)
    seq = []
    for n, d in enumerate(pre):
        kw = _dot_tn(d['k_end'], uw[n])
        seq.append(dict(h=_bf(d['q_dec'] - ab[n, :, LANES:]), z=ab[n, :, :LANES], m=_bf(kw[:, LANES:]),
                        n=kw[:, :LANES], e_last=d['e_last']))

    for ci in range(nc):
        rows = slice(ci * c, (ci + 1) * c)
        for h in range(GDN_HEADS):
            d = seq[ci * GDN_HEADS + h]
            st = st_ref[h]
            stb = _bf(st)
            o = _dot(d['h'], stb) + d['z']
            st_ref[h] = st * d['e_last'] - _dot(d['m'], stb) + d['n']
            o = o * lax.rsqrt(jnp.mean(o * o, axis=-1, keepdims=True) + NORM_EPS) * norm_ref[...]
            z = z_ref[0, rows, h * LANES:(h + 1) * LANES].astype(F32)
            o_ref[0, rows, h * LANES:(h + 1) * LANES] = (o * _silu(z)).astype(o_ref.dtype)


def _gdn(p16, p32, conv_w, alog, dtb, norm, tt):
    b, s, _ = p16.shape
    full = lambda shape: pl.BlockSpec(shape, lambda bi, i: (0,) * len(shape))
    return pl.pallas_call(
        functools.partial(_gdn_body, tt=tt),
        grid=(b, s // tt),
        in_specs=[pl.BlockSpec((1, tt, 512), lambda bi, i: (bi, i, P16_GDN_Q)),
                  pl.BlockSpec((1, tt, 512), lambda bi, i: (bi, i, P16_GDN_Q + 1)),
                  pl.BlockSpec((1, tt, 512), lambda bi, i: (bi, i, P16_GDN_Q + 2)),
                  pl.BlockSpec((1, tt, 512), lambda bi, i: (bi, i, P16_GDN_Z)),
                  pl.BlockSpec((1, tt, 128), lambda bi, i: (bi, i, P32_MISC)),
                  full((GDN_CONV, 1536)), full((1, 128)), full((1, 128)), full((1, 128))],
        out_specs=pl.BlockSpec((1, tt, 512), lambda bi, i: (bi, i, 0)),
        out_shape=jax.ShapeDtypeStruct((b, s, 512), BF16),
        scratch_shapes=[pltpu.VMEM((tt + 8, 1536), F32), pltpu.VMEM((GDN_HEADS, GDN_DK, GDN_DV), F32)],
        compiler_params=_cparams(("parallel", "arbitrary")),
        name="gdn",
    )(p16, p16, p16, p16, p32, conv_w, alog, dtb, norm)


def _rwkv_body(*refs, tt, has_vres):
    if has_vres:
        (r_ref, k_ref, v_ref, lat_ref, misc_ref, vfirst_ref, mu_ref, mulat_ref, w0_ref, wup_ref, a0_ref, aup_ref,
         gup_ref, kk_ref, ka_ref, rk_ref, lng_ref, lnb_ref, vmu_ref, v0_ref, vup_ref, hsum_ref,
         o_ref, ext_ref, extl_ref, extm_ref, st_ref) = refs
        vout_ref = None
    else:
        (r_ref, k_ref, v_ref, lat_ref, mu_ref, mulat_ref, w0_ref, wup_ref, a0_ref, aup_ref,
         gup_ref, kk_ref, ka_ref, rk_ref, lng_ref, lnb_ref, hsum_ref,
         o_ref, vout_ref, ext_ref, extl_ref, st_ref) = refs
    c = CHUNK
    nc = tt // c
    n_pairs = RWKV_HEADS // 2

    @pl.when(pl.program_id(1) == 0)
    def _():
        st_ref[...] = jnp.zeros_like(st_ref)
        ext_ref[0:8, :] = jnp.zeros((8, 1536), F32)
        extl_ref[0:8, :] = jnp.zeros((8, 256), F32)
        if has_vres:
            extm_ref[0:8, :] = jnp.zeros((8, 128), F32)

    ext_ref[8:8 + tt, 0:512] = r_ref[0].astype(F32)
    ext_ref[8:8 + tt, 512:1024] = k_ref[0].astype(F32)
    ext_ref[8:8 + tt, 1024:1536] = v_ref[0].astype(F32)
    cur = ext_ref[8:8 + tt, :]
    rkv = cur + (ext_ref[7:7 + tt, :] - cur) * mu_ref[...]
    ext_ref[0:8, :] = ext_ref[tt:tt + 8, :]
    extl_ref[8:8 + tt, :] = lat_ref[0]
    curl = extl_ref[8:8 + tt, :]
    lat = curl + (extl_ref[7:7 + tt, :] - curl) * mulat_ref[...]
    extl_ref[0:8, :] = extl_ref[tt:tt + 8, :]

    r = rkv[:, 0:512]
    k = rkv[:, 512:1024]
    v = rkv[:, 1024:1536]
    lat_wa = lat[:, 0:128]
    lat_g = lat[:, 128:256]
    w_pre = w0_ref[...] + _dot(_bf(jnp.tanh(lat_wa)), wup_ref[...])
    w = -_softplus(-w_pre) - 0.5
    logw = -jnp.exp(w)
    a = _sigmoid(a0_ref[...] + _dot(_bf(lat_wa), aup_ref[...]))
    g = _dot(_bf(_sigmoid(lat_g)), gup_ref[...])
    if has_vres:
        extm_ref[8:8 + tt, :] = misc_ref[0]
        curm = extm_ref[8:8 + tt, :]
        vlat = curm + (extm_ref[7:7 + tt, :] - curm) * vmu_ref[...]
        extm_ref[0:8, :] = extm_ref[tt:tt + 8, :]
        v_gate = _sigmoid(v0_ref[...] + _dot(_bf(vlat), vup_ref[...]))
        v = v + (vfirst_ref[0] - v) * v_gate
    else:
        vout_ref[0] = v

    hsum = hsum_ref[...]
    kkv = k * kk_ref[...]
    kk = kkv * lax.rsqrt(_dot_sel_r(kkv * kkv, hsum) + NORM_EPS)
    k = k * (1.0 + (a - 1.0) * ka_ref[...])
    bonus = _dot_sel_r(r * k * rk_ref[...], hsum) * v
    bvec = kk * a

    lane = _iota2((1, LANES), 1)
    first = lane < RWKV_N
    rowh = _iota2((LANES, LANES), 0) < RWKV_N
    colh = _iota2((LANES, LANES), 1) < RWKV_N
    blockdiag = rowh == colh
    causal, strict = _tri_masks(c)
    tri = jnp.where(causal, 1.0, 0.0).astype(BF16)

    l_list, pre = [], []
    for ci in range(nc):
        rows = slice(ci * c, (ci + 1) * c)
        lw = logw[rows]
        cum = _dot_sel(tri, lw)
        gam = jnp.exp(cum)
        inv = jnp.exp(-cum)
        gam_ex = jnp.exp(cum - lw)
        gam_c = gam[c - 1:c]
        a_t = -kk[rows] * gam_ex
        r_t = r[rows] * gam
        b_t = bvec[rows] * inv
        k_t = k[rows] * inv
        for p in range(n_pairs):
            ls = slice(p * LANES, (p + 1) * LANES)
            a_p, r_p, b_p, k_p = a_t[:, ls], r_t[:, ls], b_t[:, ls], k_t[:, ls]
            a_h = [jnp.where(first, a_p, 0.0), jnp.where(first, 0.0, a_p)]
            r_h = [jnp.where(first, r_p, 0.0), jnp.where(first, 0.0, r_p)]
            lhs = _bf(jnp.concatenate(a_h + r_h, axis=0))
            m_b = _dot_nt(lhs, _bf(b_p))
            m_k = _dot_nt(lhs, _bf(k_p))
            heads = []
            for hh in range(2):
                l_list.append(jnp.where(strict, -m_b[hh * c:(hh + 1) * c], 0.0))
                heads.append(dict(a_h=a_h[hh], r_h=r_h[hh],
                                  a_ak=_bf(jnp.where(strict, m_k[hh * c:(hh + 1) * c], 0.0)),
                                  a_rb=_bf(jnp.where(causal, m_b[(2 + hh) * c:(3 + hh) * c], 0.0)),
                                  a_rk=_bf(jnp.where(causal, m_k[(2 + hh) * c:(3 + hh) * c], 0.0))))
            pre.append(dict(heads=heads, gam_c=gam_c[:, ls],
                            b_end=_bf(b_p * gam_c[:, ls]), k_end=_bf(k_p * gam_c[:, ls]),
                            v_p=_bf(v[rows, ls])))
    t_all = _unit_lower_inverse(jnp.stack(l_list, axis=0))

    heads_all = [hd for d in pre for hd in d['heads']]
    stack = lambda key: jnp.stack([hd[key] for hd in heads_all], axis=0)
    v_all = jnp.stack([d['v_p'] for d in pre for _ in range(2)], axis=0)
    akv = _bmm(stack('a_ak'), v_all)
    gq = _bf(_bmm(_bf(t_all), _bf(jnp.concatenate([stack('a_h'), akv], axis=2))))
    rb = _bmm(stack('a_rb'), gq)
    h_all = stack('r_h') + rb[:, :, :LANES]
    z_all = rb[:, :, LANES:] + _bmm(stack('a_rk'), v_all)
    seq = []
    for n, d in enumerate(pre):
        g_p = gq[2 * n, :, :LANES] + gq[2 * n + 1, :, :LANES]
        q1_p = jnp.where(first, gq[2 * n, :, LANES:], gq[2 * n + 1, :, LANES:])
        m_p = jnp.where(blockdiag, _dot_tn(g_p, d['b_end']), 0.0)
        n_p = jnp.where(blockdiag, _dot_tn(q1_p, d['b_end']) + _dot_tn(d['v_p'], d['k_end']), 0.0)
        seq.append(dict(h=_bf(h_all[2 * n] + h_all[2 * n + 1]), z=jnp.where(first, z_all[2 * n], z_all[2 * n + 1]),
                        m=_bf(m_p), n=n_p, gam_c=d['gam_c']))

    for ci in range(nc):
        for p in range(n_pairs):
            d = seq[ci * n_pairs + p]
            st = st_ref[p]
            stb = _bf(st)
            y = _dot_nt(d['h'], stb) + d['z']
            st_ref[p] = st * d['gam_c'] + _dot(stb, d['m']) + d['n']
            ext_ref[8 + ci * c:8 + (ci + 1) * c, p * LANES:(p + 1) * LANES] = y

    y = ext_ref[8:8 + tt, 0:512]
    mean = _dot_sel_r(y, hsum) * (1.0 / RWKV_N)
    yc = y - mean
    var = _dot_sel_r(yc * yc, hsum) * (1.0 / RWKV_N)
    yn = yc * lax.rsqrt(var + RWKV_LN_EPS) * lng_ref[...] + lnb_ref[...]
    o_ref[0] = ((yn + bonus) * g).astype(o_ref.dtype)


def _rwkv(p16, p32, v_first, weights, tt, has_vres):
    b, s, _ = p16.shape
    full = lambda shape: pl.BlockSpec(shape, lambda bi, i: (0,) * len(shape))
    tok = lambda w, blk: pl.BlockSpec((1, tt, w), lambda bi, i: (bi, i, blk))
    in_specs = [tok(512, P16_RWKV_R), tok(512, P16_RWKV_R + 1), tok(512, P16_RWKV_R + 2), tok(256, P32_RWKV_LAT)]
    args = [p16, p16, p16, p32]
    if has_vres:
        in_specs += [tok(128, P32_MISC), tok(512, 0)]
        args += [p32, v_first]
    in_specs += [full(w.shape) for w in weights]
    args += list(weights)
    out_specs = [pl.BlockSpec((1, tt, 512), lambda bi, i: (bi, i, 0))]
    out_shape = [jax.ShapeDtypeStruct((b, s, 512), BF16)]
    scratch = [pltpu.VMEM((tt + 8, 1536), F32), pltpu.VMEM((tt + 8, 256), F32)]
    if has_vres:
        scratch.append(pltpu.VMEM((tt + 8, 128), F32))
    else:
        out_specs.append(pl.BlockSpec((1, tt, 512), lambda bi, i: (bi, i, 0)))
        out_shape.append(jax.ShapeDtypeStruct((b, s, 512), F32))
    scratch.append(pltpu.VMEM((RWKV_HEADS // 2, LANES, LANES), F32))
    return pl.pallas_call(
        functools.partial(_rwkv_body, tt=tt, has_vres=has_vres),
        grid=(b, s // tt),
        in_specs=in_specs,
        out_specs=out_specs,
        out_shape=out_shape,
        scratch_shapes=scratch,
        compiler_params=_cparams(("parallel", "arbitrary")),
        name="rwkv7",
    )(*args)


def _layernorm(h, g, b):
    mu = jnp.mean(h, axis=-1, keepdims=True)
    hc = h - mu
    var = jnp.mean(hc * hc, axis=-1, keepdims=True)
    return hc * lax.rsqrt(var + LN_EPS) * g + b


def _merge_body(ya_ref, yb_ref, yc_ref, yd_ref, gate_ref, x_ref, wb_ref, wo_ref, g_ref, b_ref, o_ref):
    merged = None
    for n, y_ref in enumerate((ya_ref, yb_ref, yc_ref, yd_ref)):
        gate = _sigmoid(gate_ref[0, :, n * D_MODEL:(n + 1) * D_MODEL].astype(F32))
        term = gate * _dot(y_ref[0], wb_ref[n])
        merged = term if merged is None else merged + term
    mix = _dot(_bf(merged), wo_ref[...])
    o_ref[0] = _layernorm(DEEPNORM_ALPHA * x_ref[0] + mix, g_ref[...], b_ref[...])


def _merge(ys, p16, x, wb, wo, g, bb, tt):
    b, s, d = x.shape
    full = lambda shape: pl.BlockSpec(shape, lambda bi, i: (0,) * len(shape))
    ytok = pl.BlockSpec((1, tt, 512), lambda bi, i: (bi, i, 0))
    return pl.pallas_call(
        _merge_body,
        grid=(b, s // tt),
        in_specs=[ytok, ytok, ytok, ytok,
                  pl.BlockSpec((1, tt, N_BRANCH * D_MODEL), lambda bi, i: (bi, i, P16_GATES)),
                  pl.BlockSpec((1, tt, d), lambda bi, i: (bi, i, 0)),
                  full((N_BRANCH, BRANCH_WIDTH, D_MODEL)), full((D_MODEL, D_MODEL)), full((1, d)), full((1, d))],
        out_specs=pl.BlockSpec((1, tt, d), lambda bi, i: (bi, i, 0)),
        out_shape=jax.ShapeDtypeStruct((b, s, d), F32),
        compiler_params=_cparams(("parallel", "parallel")),
        name="merge_ln1",
    )(*ys, p16, x, wb, wo, g, bb)


def _moe_body(x_ref, rwt_ref, rbias_ref, wg_ref, wu_ref, wd_ref, g_ref, b_ref, o_ref, xb_ref, comb_ref, acc_ref):
    e = pl.program_id(2)
    tt = x_ref.shape[1]

    @pl.when(e == 0)
    def _():
        x = x_ref[0]
        xb_ref[...] = x.astype(BF16)
        acc_ref[...] = jnp.zeros_like(acc_ref)
        scores = _sigmoid(_dot3_nt(rwt_ref[...], x))
        biased = scores + rbias_ref[...]
        row = _iota2((N_EXPERTS, tt), 0).astype(F32)
        row_group = (_iota2((N_EXPERTS, tt), 0) >> 2).astype(F32)
        neg = -jnp.inf
        gs = []
        for gi in range(N_GROUPS):
            rows = [biased[gi * GROUP_SIZE + j:gi * GROUP_SIZE + j + 1] for j in range(GROUP_SIZE)]
            best = None
            for i0 in range(GROUP_SIZE):
                for i1 in range(i0 + 1, GROUP_SIZE):
                    pair = rows[i0] + rows[i1]
                    best = pair if best is None else jnp.maximum(best, pair)
            gs.append(best)
        gmax = functools.reduce(jnp.maximum, gs)
        best_group = jnp.full((1, tt), float(N_GROUPS), F32)
        for gi in reversed(range(N_GROUPS)):
            best_group = jnp.where(gs[gi] == gmax, float(gi), best_group)
        masked = jnp.where(row_group == best_group, biased, neg)
        m1 = jnp.max(masked, axis=0, keepdims=True)
        i1 = jnp.min(jnp.where(masked == m1, row, float(N_EXPERTS)), axis=0, keepdims=True)
        masked2 = jnp.where(row == i1, neg, masked)
        m2 = jnp.max(masked2, axis=0, keepdims=True)
        i2 = jnp.min(jnp.where(masked2 == m2, row, float(N_EXPERTS)), axis=0, keepdims=True)
        s1 = jnp.sum(jnp.where(row == i1, scores, 0.0), axis=0, keepdims=True)
        s2 = jnp.sum(jnp.where(row == i2, scores, 0.0), axis=0, keepdims=True)
        tot = s1 + s2
        comb_t = jnp.where(row == i1, s1 / tot, 0.0) + jnp.where(row == i2, s2 / tot, 0.0)
        eye = jnp.where(_iota2((N_EXPERTS, N_EXPERTS), 0) == _iota2((N_EXPERTS, N_EXPERTS), 1), 1.0, 0.0).astype(BF16)
        hi, lo = _split2(comb_t)
        comb_ref[...] = _dot_tn(hi, eye) + _dot_tn(lo, eye)

    xb = xb_ref[...]
    hid = _silu(_dot(xb, wg_ref[0])) * _dot(xb, wu_ref[0])
    ce = jnp.sum(jnp.where(_iota2((1, N_EXPERTS), 1) == e, comb_ref[...], 0.0), axis=-1, keepdims=True)
    acc_ref[...] += ce * _dot(_bf(hid), wd_ref[0])

    @pl.when(e == N_EXPERTS - 1)
    def _():
        o_ref[0] = _layernorm(DEEPNORM_ALPHA * x_ref[0] + acc_ref[...], g_ref[...], b_ref[...])


def _moe(x, rwt, rbias, wg, wu, wd, g, bb, tt):
    b, s, d = x.shape
    full = lambda shape: pl.BlockSpec(shape, lambda bi, i, e: (0,) * len(shape))
    return pl.pallas_call(
        _moe_body,
        grid=(b, s // tt, N_EXPERTS),
        in_specs=[pl.BlockSpec((1, tt, d), lambda bi, i, e: (bi, i, 0)),
                  full((N_EXPERTS, d)), full((N_EXPERTS, 1)),
                  pl.BlockSpec((1, d, D_EXPERT), lambda bi, i, e: (e, 0, 0)),
                  pl.BlockSpec((1, d, D_EXPERT), lambda bi, i, e: (e, 0, 0)),
                  pl.BlockSpec((1, D_EXPERT, d), lambda bi, i, e: (e, 0, 0)),
                  full((1, d)), full((1, d))],
        out_specs=pl.BlockSpec((1, tt, d), lambda bi, i, e: (bi, i, 0)),
        out_shape=jax.ShapeDtypeStruct((b, s, d), F32),
        scratch_shapes=[pltpu.VMEM((tt, d), BF16), pltpu.VMEM((tt, N_EXPERTS), F32), pltpu.VMEM((tt, d), F32)],
        compiler_params=_cparams(("parallel", "parallel", "arbitrary")),
        name="moe_ln2",
    )(x, rwt, rbias, wg, wu, wd, g, bb)


def _cols(w, name):
    o, n = _OFF[name]
    return w[:, o:o + n]


def _pack_inproj(w_in_l, w_vres_l):
    d = w_in_l.shape[0]
    w16 = jnp.concatenate([_cols(w_in_l, "gates"), _cols(w_in_l, "gla_q"), _cols(w_in_l, "gla_k"),
                           _cols(w_in_l, "gla_v"), _cols(w_in_l, "gla_og"), _cols(w_in_l, "gdn_qkv"),
                           _cols(w_in_l, "gdn_z"), _cols(w_in_l, "rwkv_rkv")], axis=1).astype(BF16)
    vres = jnp.zeros((d, RWKV_V_RANK), F32) if w_vres_l is None else w_vres_l
    misc = jnp.concatenate([_cols(w_in_l, "k_rope"), _cols(w_in_l, "gla_gl"), _cols(w_in_l, "gdn_beta"),
                            _cols(w_in_l, "gdn_a"), vres], axis=1)
    misc = jnp.pad(misc, ((0, 0), (0, LANES - misc.shape[1])))
    w32 = jnp.concatenate([_cols(w_in_l, "q_lat"), _cols(w_in_l, "rwkv_lat"), _cols(w_in_l, "c_kv"), misc],
                          axis=1).astype(BF16)
    return w16, w32


def _row_pad(w, start, total=LANES):
    return jnp.pad(w, ((start, total - start - w.shape[0]), (0, 0)))


def _lane_row(vec, start, total=LANES):
    return jnp.pad(vec, (start, total - start - vec.shape[0])).reshape(1, total)


def _pack_mla(w_uq, w_ukv):
    half = MLA_ROPE // 2
    wq = w_uq.reshape(MLA_Q_RANK, MLA_HEADS, MLA_NOPE + MLA_ROPE)
    nope, r1, r2 = wq[..., :MLA_NOPE], wq[..., MLA_NOPE:MLA_NOPE + half], wq[..., MLA_NOPE + half:]
    pad = jnp.zeros((MLA_Q_RANK, MLA_HEADS, LANES - MLA_NOPE - MLA_ROPE), F32)
    wqa = jnp.concatenate([nope, r1, r2, pad], axis=-1).reshape(MLA_Q_RANK, MLA_HEADS * LANES)
    wqb = jnp.concatenate([jnp.zeros_like(nope), -r2, r1, pad], axis=-1).reshape(MLA_Q_RANK, MLA_HEADS * LANES)
    wkv = w_ukv.reshape(MLA_KV_RANK, MLA_HEADS, MLA_NOPE + MLA_V)
    wuk = jnp.pad(wkv[..., :MLA_NOPE], ((0, 0), (0, 0), (0, LANES - MLA_NOPE))).reshape(MLA_KV_RANK, MLA_HEADS * LANES)
    wuv = wkv[..., MLA_NOPE:].reshape(MLA_KV_RANK, MLA_HEADS * MLA_V).T
    ra = np.zeros((LANES, LANES), np.float32)
    rb = np.zeros((LANES, LANES), np.float32)
    for j in range(half):
        ra[MISC_KROPE + j, MLA_NOPE + j] = 1.0
        ra[MISC_KROPE + half + j, MLA_NOPE + half + j] = 1.0
        rb[MISC_KROPE + half + j, MLA_NOPE + j] = -1.0
        rb[MISC_KROPE + j, MLA_NOPE + half + j] = 1.0
    inv_freq = ROPE_THETA ** (-jnp.arange(half, dtype=F32) / half)
    freq = jnp.concatenate([jnp.zeros((MLA_NOPE,), F32), inv_freq, inv_freq,
                            jnp.zeros((LANES - MLA_NOPE - MLA_ROPE,), F32)]).reshape(1, LANES)
    return (wqa.astype(BF16), wqb.astype(BF16), wuk.astype(BF16), wuv.astype(BF16),
            jnp.asarray(ra, BF16), jnp.asarray(rb, BF16), freq)


def _head_sum_matrix():
    idx = np.arange(RWKV_WIDTH) // RWKV_N
    return jnp.asarray((idx[:, None] == idx[None, :]).astype(np.float32), BF16)


def _tile(s, pref):
    t = min(pref, s)
    assert s % t == 0 and t % CHUNK == 0
    return t


def kernel(x, positions, router_w, w_in, w_in_vres, mla_q_norm, mla_w_uq, mla_kv_norm, mla_w_ukv,
           gla_w_gate_up, gla_b_gate, gla_norm, gdn_conv, gdn_a_log, gdn_dt_bias, gdn_norm,
           rwkv_mu, rwkv_w0, rwkv_w_up, rwkv_a0, rwkv_a_up, rwkv_g_up, rwkv_k_k, rwkv_k_a, rwkv_r_k,
           rwkv_ln_g, rwkv_ln_b, rwkv_vres_mu, rwkv_v0, rwkv_v_up, w_branch, w_out,
           ln1_g, ln1_b, ln2_g, ln2_b, router_bias, moe_w_gate, moe_w_up, moe_w_down):
    b, s, d = x.shape
    assert d == D_MODEL and w_in.shape[2] == IN_COLS
    pos3 = positions.reshape(b, s, 1)
    rwt = router_w.T
    hsum = _head_sum_matrix()
    row = lambda v: v.reshape(1, -1)
    t_proj = _tile(s, 1024)
    t_scan = _tile(s, 256)
    t_attn = _tile(s, 512)
    t_merge = _tile(s, 256)
    t_moe = _tile(s, 1024)

    v_first = None
    for l in range(DEPTH):
        w16, w32 = _pack_inproj(w_in[l], None if l == 0 else w_in_vres[l - 1])
        p16 = _inproj(x, w16, BF16, t_proj, 1024)
        p32 = _inproj(x, w32, F32, t_proj, P32_WIDTH)

        wqa, wqb, wuk, wuv, wkra, wkrb, freq = _pack_mla(mla_w_uq[l], mla_w_ukv[l])
        q, k, v = _mla_prep(p32, pos3, row(mla_q_norm[l]), wqa, wqb, row(mla_kv_norm[l]), wuk, wuv, wkra, wkrb,
                            freq, t_attn)
        y_a = _mla_attn(q, k, v, t_attn)

        wgu = _row_pad(gla_w_gate_up[l], MISC_GLA_GL)
        y_b = _gla(p16, p32, wgu, row(gla_b_gate[l]), row(gla_norm[l]), t_scan)

        y_c = _gdn(p16, p32, gdn_conv[l], _lane_row(gdn_a_log[l], MISC_GDN_A), _lane_row(gdn_dt_bias[l], MISC_GDN_A),
                   row(gdn_norm[l]), t_scan)

        mu = rwkv_mu[l]
        weights = [row(mu[:1536]), row(mu[1536:]), row(rwkv_w0[l]),
                   _row_pad(rwkv_w_up[l], 0).astype(BF16), row(rwkv_a0[l]),
                   _row_pad(rwkv_a_up[l], RWKV_W_RANK).astype(BF16), rwkv_g_up[l].astype(BF16),
                   row(rwkv_k_k[l]), row(rwkv_k_a[l]), row(rwkv_r_k[l]), row(rwkv_ln_g[l]), row(rwkv_ln_b[l])]
        if l == 0:
            y_d, v_first = _rwkv(p16, p32, None, weights + [hsum], t_scan, False)
        else:
            weights += [_lane_row(rwkv_vres_mu[l - 1], MISC_VRES), row(rwkv_v0[l - 1]),
                        _row_pad(rwkv_v_up[l - 1], MISC_VRES).astype(BF16)]
            (y_d,) = _rwkv(p16, p32, v_first, weights + [hsum], t_scan, True)

        x = _merge((y_a, y_b, y_c, y_d), p16, x, w_branch[l].astype(BF16), w_out[l].astype(BF16),
                   row(ln1_g[l]), row(ln1_b[l]), t_merge)
        x = _moe(x, rwt, router_bias[l].reshape(N_EXPERTS, 1), moe_w_gate[l].astype(BF16),
                 moe_w_up[l].astype(BF16), moe_w_down[l].astype(BF16), row(ln2_g[l]), row(ln2_b[l]), t_moe)
    return x
```

```python
import functools
import math

import jax
import jax.numpy as jnp
import numpy as np
from jax import lax
from jax.experimental import pallas as pl
from jax.experimental.pallas import tpu as pltpu

F32 = jnp.float32
BF16 = jnp.bfloat16

LANES = 128
VMEM_LIMIT = 56 * 1024 * 1024

D_MODEL = 1024
DEPTH = 2
MLA_HEADS, MLA_NOPE, MLA_ROPE, MLA_V = 8, 64, 32, 64
MLA_Q_RANK, MLA_KV_RANK = 256, 128
MLA_VT_ROWS = MLA_V + 16
ROPE_THETA = 10000.0
GLA_HEADS, GLA_DK, GLA_DV, GLA_GATE_RANK, GLA_TAU = 4, 64, 128, 16, 16.0
GDN_HEADS, GDN_DK, GDN_DV, GDN_CONV = 4, 128, 128, 4
RWKV_HEADS, RWKV_N = 8, 64
RWKV_W_RANK, RWKV_A_RANK, RWKV_V_RANK, RWKV_G_RANK = 64, 64, 32, 128
RWKV_LN_EPS = 64e-5
CHUNK = 64
N_BRANCH, BRANCH_WIDTH = 4, 512
N_EXPERTS, N_GROUPS, TOP_K, D_EXPERT = 16, 4, 2, 512
GROUP_SIZE = N_EXPERTS // N_GROUPS
MOE_ROWS = 256
DEEPNORM_ALPHA = (2.0 * DEPTH) ** 0.25
LN_EPS = 1e-5
NORM_EPS = 1e-6
RWKV_WIDTH = RWKV_HEADS * RWKV_N

_OFF = {}
_o = 0
for _name, _w in (("q_lat", MLA_Q_RANK), ("c_kv", MLA_KV_RANK), ("k_rope", MLA_ROPE),
                  ("gla_q", 256), ("gla_k", 256), ("gla_v", 512), ("gla_gl", GLA_GATE_RANK), ("gla_og", 512),
                  ("gdn_qkv", 1536), ("gdn_beta", GDN_HEADS), ("gdn_a", GDN_HEADS), ("gdn_z", 512),
                  ("rwkv_rkv", 1536), ("rwkv_lat", 256), ("gates", N_BRANCH * D_MODEL)):
    _OFF[_name] = (_o, _w)
    _o += _w
IN_COLS = _o

MISC_KROPE = 0
MISC_GLA_GL = 32
MISC_GDN_BETA = 48
MISC_GDN_A = 52
MISC_VRES = 56

P16_WIDTH = 9216
P16_GATES = 0
P16_GLA_QK = 8
P16_GLA_V = 9
P16_GLA_OG = 10
P16_GDN_Q = 11
P16_GDN_Z = 14
P16_RWKV_R = 15
P32_WIDTH = 768
P32_QLAT = 0
P32_RWKV_LAT = 1
P32_CKV = 4
P32_MISC = 5


def _cparams(sem):
    return pltpu.CompilerParams(dimension_semantics=sem, vmem_limit_bytes=VMEM_LIMIT)


def _dot(a, b):
    return jnp.dot(a, b, preferred_element_type=F32)


def _dot_nt(a, b):
    return lax.dot_general(a, b, (((1,), (1,)), ((), ())), preferred_element_type=F32)


def _dot_tn(a, b):
    return lax.dot_general(a, b, (((0,), (0,)), ((), ())), preferred_element_type=F32)


def _bmm(a, b):
    return jnp.einsum('bij,bjk->bik', a, b, preferred_element_type=F32)


def _bf(x):
    return x.astype(BF16)


def _split2(x):
    hi = x.astype(BF16)
    lo = (x - hi.astype(F32)).astype(BF16)
    return hi, lo


def _dot_sel(m_bf, x):
    hi, lo = _split2(x)
    return _dot(m_bf, hi) + _dot(m_bf, lo)


def _dot_sel_r(x, m_bf):
    hi, lo = _split2(x)
    return _dot(hi, m_bf) + _dot(lo, m_bf)


def _dot3(a, b):
    ah, al = _split2(a)
    bh, bl = _split2(b)
    return _dot(ah, bh) + _dot(al, bh) + _dot(ah, bl)


def _dot3_nt(a, b):
    ah, al = _split2(a)
    bh, bl = _split2(b)
    return _dot_nt(ah, bh) + _dot_nt(al, bh) + _dot_nt(ah, bl)


def _sigmoid(x):
    return 1.0 / (1.0 + jnp.exp(-x))


def _silu(x):
    return x * _sigmoid(x)


def _softplus(x):
    return jnp.maximum(x, 0.0) + jnp.log(1.0 + jnp.exp(-jnp.abs(x)))


def _iota2(shape, dim):
    return lax.broadcasted_iota(jnp.int32, shape, dim)


def _tri_masks(c):
    r = _iota2((c, c), 0)
    q = _iota2((c, c), 1)
    return r >= q, r > q


def _unit_lower_inverse(l_mat):
    nb, c, _ = l_mat.shape
    r = _iota2((c, c), 0)
    q = _iota2((c, c), 1)
    pair = jnp.where((r >> 1) == (q >> 1), 1.0, 0.0).astype(F32)
    t = jnp.where(r == q, 1.0, 0.0).astype(F32)[None] - l_mat * pair[None]
    s = 2
    while s < c:
        sh = s.bit_length()
        same = (r >> sh) == (q >> sh)
        m = jnp.where(same, jnp.where((r & s) != 0, jnp.where((q & s) == 0, 1.0, 0.0), 0.0), 0.0)
        ls = l_mat * m[None]
        tb = _bf(t)
        x = _bmm(_bf(_bmm(tb, _bf(ls))), tb)
        t = t - x
        s *= 2
    return t


def _inproj_body(x_ref, w_ref, o_ref, xb_ref):
    @pl.when(pl.program_id(2) == 0)
    def _():
        xb_ref[...] = x_ref[0].astype(BF16)

    o_ref[0] = _dot(xb_ref[...], w_ref[...]).astype(o_ref.dtype)


def _inproj(x, w, out_dtype, tm, tn):
    b, s, d = x.shape
    n = w.shape[1]
    return pl.pallas_call(
        _inproj_body,
        grid=(b, s // tm, n // tn),
        in_specs=[pl.BlockSpec((1, tm, d), lambda bi, i, j: (bi, i, 0)),
                  pl.BlockSpec((d, tn), lambda bi, i, j: (0, j))],
        out_specs=pl.BlockSpec((1, tm, tn), lambda bi, i, j: (bi, i, j)),
        out_shape=jax.ShapeDtypeStruct((b, s, n), out_dtype),
        scratch_shapes=[pltpu.VMEM((tm, d), BF16)],
        compiler_params=_cparams(("parallel", "parallel", "arbitrary")),
        name="inproj",
    )(x, w)


def _mla_prep_body(qlat_ref, ckv_ref, misc_ref, pos_ref, qnorm_ref, wqa_ref, wqb_ref, kvnorm_ref,
                   wuk_ref, wuv_ref, wkra_ref, wkrb_ref, freq_ref, q_ref, k_ref, v_ref):
    scale = (MLA_NOPE + MLA_ROPE) ** -0.5 * math.log2(math.e)
    ang = pos_ref[0].astype(F32) * freq_ref[...]
    cos_t = jnp.cos(ang)
    sin_t = jnp.sin(ang)

    ql = qlat_ref[0]
    qn = ql * lax.rsqrt(jnp.mean(ql * ql, axis=-1, keepdims=True) + NORM_EPS) * qnorm_ref[...]
    qnb = _bf(qn)
    qa = _dot(qnb, wqa_ref[...])
    qb = _dot(qnb, wqb_ref[...])

    ck = ckv_ref[0]
    kvn = ck * lax.rsqrt(jnp.mean(ck * ck, axis=-1, keepdims=True) + NORM_EPS) * kvnorm_ref[...]
    kvb = _bf(kvn)
    kn = _dot(kvb, wuk_ref[...])
    vt = _dot_nt(wuv_ref[...], kvb)
    misc = misc_ref[0]
    kr = _dot_sel_r(misc, wkra_ref[...]) * cos_t + _dot_sel_r(misc, wkrb_ref[...]) * sin_t

    for h in range(MLA_HEADS):
        sl = slice(h * LANES, (h + 1) * LANES)
        q_ref[0, h] = ((qa[:, sl] * cos_t + qb[:, sl] * sin_t) * scale).astype(BF16)
        k_ref[0, h] = (kn[:, sl] + kr).astype(BF16)
    t = vt.shape[1]
    ones = jnp.ones((MLA_VT_ROWS - MLA_V, t), F32)
    for h in range(MLA_HEADS):
        v_ref[0, h, 0] = jnp.concatenate([vt[h * MLA_V:(h + 1) * MLA_V], ones], axis=0).astype(BF16)


def _mla_prep(p32, pos3, qnorm, wqa, wqb, kvnorm, wuk, wuv, wkra, wkrb, freq, tt):
    b, s, _ = p32.shape
    full = lambda shape: pl.BlockSpec(shape, lambda bi, i: (0,) * len(shape))
    return pl.pallas_call(
        _mla_prep_body,
        grid=(b, s // tt),
        in_specs=[pl.BlockSpec((1, tt, 256), lambda bi, i: (bi, i, P32_QLAT)),
                  pl.BlockSpec((1, tt, 128), lambda bi, i: (bi, i, P32_CKV)),
                  pl.BlockSpec((1, tt, 128), lambda bi, i: (bi, i, P32_MISC)),
                  pl.BlockSpec((1, tt, 1), lambda bi, i: (bi, i, 0)),
                  full((1, 256)), full((256, 1024)), full((256, 1024)), full((1, 128)),
                  full((128, 1024)), full((512, 128)), full((128, 128)), full((128, 128)), full((1, 128))],
        out_specs=[pl.BlockSpec((1, MLA_HEADS, tt, 128), lambda bi, i: (bi, 0, i, 0)),
                   pl.BlockSpec((1, MLA_HEADS, tt, 128), lambda bi, i: (bi, 0, i, 0)),
                   pl.BlockSpec((1, MLA_HEADS, 1, MLA_VT_ROWS, tt), lambda bi, i: (bi, 0, i, 0, 0))],
        out_shape=[jax.ShapeDtypeStruct((b, MLA_HEADS, s, 128), BF16),
                   jax.ShapeDtypeStruct((b, MLA_HEADS, s, 128), BF16),
                   jax.ShapeDtypeStruct((b, MLA_HEADS, s // tt, MLA_VT_ROWS, tt), BF16)],
        compiler_params=_cparams(("parallel", "parallel")),
        name="mla_prep",
    )(p32, p32, p32, pos3, qnorm, wqa, wqb, kvnorm, wuk, wuv, wkra, wkrb, freq)


def _mla_attn_body(q_ref, k_ref, v_ref, o_ref, m_ref, acc_ref, s_ref, *, tq):
    i = pl.program_id(2)
    for hh in range(2):
        m_ref[hh] = jnp.full((1, tq), -jnp.inf, F32)
        acc_ref[hh] = jnp.zeros((MLA_VT_ROWS, tq), F32)

    def scores(hh, j):
        start = pl.multiple_of(j * tq, tq)
        return _dot_nt(k_ref[0, hh, pl.ds(start, tq), :], q_ref[0, hh])

    def consume(hh, j, masked):
        st = s_ref[hh]
        if masked:
            st = jnp.where(_iota2((tq, tq), 0) <= _iota2((tq, tq), 1), st, -jnp.inf)
        m_old = m_ref[hh]
        m_new = jnp.maximum(m_old, jnp.max(st, axis=0, keepdims=True))
        p = jnp.exp2(st - m_new)
        acc_ref[hh] = acc_ref[hh] * jnp.exp2(m_old - m_new) + _dot(v_ref[0, hh, j], _bf(p))
        m_ref[hh] = m_new

    s_ref[0] = scores(0, 0)

    def step(j):
        s_ref[1] = scores(1, j)
        consume(0, j, False)
        s_ref[0] = scores(0, j + 1)
        consume(1, j, False)

    def loop_body(t, carry):
        step(2 * t)
        step(2 * t + 1)
        return carry

    lax.fori_loop(0, i // 2, loop_body, 0)

    @pl.when(i % 2 == 1)
    def _():
        step(i - 1)

    s_ref[1] = scores(1, i)
    consume(0, i, True)
    consume(1, i, True)
    outs = []
    for hh in range(2):
        a = acc_ref[hh]
        outs.append(a[0:MLA_V] / a[MLA_V:MLA_V + 1])
    o_ref[0] = jnp.concatenate(outs, axis=0).T.astype(o_ref.dtype)


def _mla_attn(q, k, v, tq):
    b, h, s, _ = q.shape
    return pl.pallas_call(
        functools.partial(_mla_attn_body, tq=tq),
        grid=(b, h // 2, s // tq),
        in_specs=[pl.BlockSpec((1, 2, tq, 128), lambda bi, p, i: (bi, p, i, 0)),
                  pl.BlockSpec((1, 2, s, 128), lambda bi, p, i: (bi, p, 0, 0)),
                  pl.BlockSpec((1, 2, s // tq, MLA_VT_ROWS, tq), lambda bi, p, i: (bi, p, 0, 0, 0))],
        out_specs=pl.BlockSpec((1, tq, 128), lambda bi, p, i: (bi, i, p)),
        out_shape=jax.ShapeDtypeStruct((b, s, h * MLA_V), BF16),
        scratch_shapes=[pltpu.VMEM((2, 1, tq), F32), pltpu.VMEM((2, MLA_VT_ROWS, tq), F32),
                        pltpu.VMEM((2, tq, tq), F32)],
        compiler_params=_cparams(("parallel", "parallel", "arbitrary")),
        name="mla_attn",
    )(q, k, v)


def _gla_body(qk_ref, v_ref, og_ref, misc_ref, wgu_ref, bgate_ref, norm_ref, o_ref, st_ref, *, tt):
    c = CHUNK

    @pl.when(pl.program_id(1) == 0)
    def _():
        st_ref[...] = jnp.zeros_like(st_ref)

    causal, _ = _tri_masks(c)
    tri = jnp.where(causal, 1.0, 0.0).astype(BF16)
    lane = _iota2((1, LANES), 1)
    qk = qk_ref[0].astype(F32)
    q_all = qk[:, :256] * (GLA_DK ** -0.5)
    k_all = qk[:, 256:]
    ga = _dot3(misc_ref[0], wgu_ref[...]) + bgate_ref[...]
    log_a = (jnp.minimum(ga, 0.0) - jnp.log(1.0 + jnp.exp(-jnp.abs(ga)))) * (1.0 / GLA_TAU)

    for ci in range(tt // c):
        rows = slice(ci * c, (ci + 1) * c)
        bcum = _dot_sel(tri, log_a[rows])
        b_last = bcum[c - 1:c]
        q_dec = q_all[rows] * jnp.exp(bcum)
        k_inv = k_all[rows] * jnp.exp(-bcum)
        k_end = k_all[rows] * jnp.exp(b_last - bcum)
        e_last = jnp.exp(b_last)
        for p in range(2):
            ls = slice(p * LANES, (p + 1) * LANES)
            qd_p, ki_p, ke_p, el_p = q_dec[:, ls], k_inv[:, ls], k_end[:, ls], e_last[:, ls]
            for hh in range(2):
                h = 2 * p + hh
                own = (lane >= hh * GLA_DK) & (lane < (hh + 1) * GLA_DK)
                qd_h = _bf(jnp.where(own, qd_p, 0.0))
                v_h = v_ref[0, rows, h * GLA_DV:(h + 1) * GLA_DV]
                att = jnp.where(causal, _dot_nt(qd_h, _bf(ki_p)), 0.0)
                st = st_ref[h]
                o = _dot(_bf(att), v_h) + _dot_nt(qd_h, _bf(st))
                st_ref[h] = st * el_p + jnp.where(own, _dot_tn(v_h, _bf(ke_p)), 0.0)
                o = o * lax.rsqrt(jnp.mean(o * o, axis=-1, keepdims=True) + NORM_EPS) * norm_ref[...]
                og = og_ref[0, rows, h * GLA_DV:(h + 1) * GLA_DV].astype(F32)
                o_ref[0, rows, h * GLA_DV:(h + 1) * GLA_DV] = (o * _silu(og)).astype(o_ref.dtype)


def _gla(p16, p32, wgu, bgate, norm, tt):
    b, s, _ = p16.shape
    full = lambda shape: pl.BlockSpec(shape, lambda bi, i: (0,) * len(shape))
    return pl.pallas_call(
        functools.partial(_gla_body, tt=tt),
        grid=(b, s // tt),
        in_specs=[pl.BlockSpec((1, tt, 512), lambda bi, i: (bi, i, P16_GLA_QK)),
                  pl.BlockSpec((1, tt, 512), lambda bi, i: (bi, i, P16_GLA_V)),
                  pl.BlockSpec((1, tt, 512), lambda bi, i: (bi, i, P16_GLA_OG)),
                  pl.BlockSpec((1, tt, 128), lambda bi, i: (bi, i, P32_MISC)),
                  full((128, 256)), full((1, 256)), full((1, 128))],
        out_specs=pl.BlockSpec((1, tt, 512), lambda bi, i: (bi, i, 0)),
        out_shape=jax.ShapeDtypeStruct((b, s, 512), BF16),
        scratch_shapes=[pltpu.VMEM((GLA_HEADS, GLA_DV, LANES), F32)],
        compiler_params=_cparams(("parallel", "arbitrary")),
        name="gla",
    )(p16, p16, p16, p32, wgu, bgate, norm)


def _gdn_body(q_ref, k_ref, v_ref, z_ref, misc_ref, conv_ref, alog_ref, dtb_ref, norm_ref, o_ref,
              ext_ref, st_ref, *, tt):
    c = CHUNK
    nc = tt // c

    @pl.when(pl.program_id(1) == 0)
    def _():
        st_ref[...] = jnp.zeros_like(st_ref)
        ext_ref[0:8, :] = jnp.zeros((8, 1536), F32)

    causal, strict = _tri_masks(c)
    tri = jnp.where(causal, 1.0, 0.0).astype(BF16)
    ones_cc = jnp.ones((c, c), BF16)
    eye = jnp.where(_iota2((c, c), 0) == _iota2((c, c), 1), 1.0, 0.0).astype(F32)

    ext_ref[8:8 + tt, 0:512] = q_ref[0].astype(F32)
    ext_ref[8:8 + tt, 512:1024] = k_ref[0].astype(F32)
    ext_ref[8:8 + tt, 1024:1536] = v_ref[0].astype(F32)
    conv = None
    for j in range(GDN_CONV):
        term = ext_ref[8 - (GDN_CONV - 1) + j:8 - (GDN_CONV - 1) + j + tt, :] * conv_ref[j:j + 1, :]
        conv = term if conv is None else conv + term
    ext_ref[0:8, :] = ext_ref[tt:tt + 8, :]
    qkv = _silu(conv)

    misc = misc_ref[0]
    beta_all = _sigmoid(misc)
    g_all = -jnp.exp(alog_ref[...]) * _softplus(misc + dtb_ref[...])

    l_list, pre = [], []
    for ci in range(nc):
        rows = slice(ci * c, (ci + 1) * c)
        gc_all = _dot_sel(tri, g_all[rows])
        for h in range(GDN_HEADS):
            ls = slice(h * LANES, (h + 1) * LANES)
            q = qkv[rows, ls]
            k = qkv[rows, 512 + h * LANES:512 + (h + 1) * LANES]
            v = qkv[rows, 1024 + h * LANES:1024 + (h + 1) * LANES]
            q = q * lax.rsqrt(jnp.sum(q * q, axis=-1, keepdims=True) + NORM_EPS) * (GDN_DK ** -0.5)
            k = k * lax.rsqrt(jnp.sum(k * k, axis=-1, keepdims=True) + NORM_EPS)
            beta = beta_all[rows, MISC_GDN_BETA + h:MISC_GDN_BETA + h + 1]
            gc = gc_all[:, MISC_GDN_A + h:MISC_GDN_A + h + 1]
            gc_row = _dot_sel(ones_cc, eye * gc)
            decay = jnp.where(causal, jnp.exp(jnp.minimum(gc - gc_row, 0.0)), 0.0)
            kb = k * beta
            kbf = _bf(k)
            l_list.append(jnp.where(strict, _dot_nt(_bf(kb), kbf) * decay, 0.0))
            att = _dot_nt(_bf(q), kbf) * decay
            eg = jnp.exp(gc)
            g_last = gc[c - 1:c]
            pre.append(dict(rhs=_bf(jnp.concatenate([v * beta, kb * eg], axis=1)), att=_bf(att), q_dec=q * eg,
                            k_end=_bf(k * jnp.exp(g_last - gc)), e_last=jnp.exp(g_last)))
    t_all = _unit_lower_inverse(jnp.stack(l_list, axis=0))

    stack = lambda key: jnp.stack([d[key] for d in pre], axis=0)
    uw = _bf(_bmm(_bf(t_all), stack('rhs')))
    ab = _bmm(stack('att'), uw)
    seq = []
    for n, d in enumerate(pre):
        kw = _dot_tn(d['k_end'], uw[n])
        seq.append(dict(h=_bf(d['q_dec'] - ab[n, :, LANES:]), z=ab[n, :, :LANES], m=_bf(kw[:, LANES:]),
                        n=kw[:, :LANES], e_last=d['e_last']))

    for ci in range(nc):
        rows = slice(ci * c, (ci + 1) * c)
        for h in range(GDN_HEADS):
            d = seq[ci * GDN_HEADS + h]
            st = st_ref[h]
            stb = _bf(st)
            o = _dot(d['h'], stb) + d['z']
            st_ref[h] = st * d['e_last'] - _dot(d['m'], stb) + d['n']
            o = o * lax.rsqrt(jnp.mean(o * o, axis=-1, keepdims=True) + NORM_EPS) * norm_ref[...]
            z = z_ref[0, rows, h * LANES:(h + 1) * LANES].astype(F32)
            o_ref[0, rows, h * LANES:(h + 1) * LANES] = (o * _silu(z)).astype(o_ref.dtype)


def _gdn(p16, p32, conv_w, alog, dtb, norm, tt):
    b, s, _ = p16.shape
    full = lambda shape: pl.BlockSpec(shape, lambda bi, i: (0,) * len(shape))
    return pl.pallas_call(
        functools.partial(_gdn_body, tt=tt),
        grid=(b, s // tt),
        in_specs=[pl.BlockSpec((1, tt, 512), lambda bi, i: (bi, i, P16_GDN_Q)),
                  pl.BlockSpec((1, tt, 512), lambda bi, i: (bi, i, P16_GDN_Q + 1)),
                  pl.BlockSpec((1, tt, 512), lambda bi, i: (bi, i, P16_GDN_Q + 2)),
                  pl.BlockSpec((1, tt, 512), lambda bi, i: (bi, i, P16_GDN_Z)),
                  pl.BlockSpec((1, tt, 128), lambda bi, i: (bi, i, P32_MISC)),
                  full((GDN_CONV, 1536)), full((1, 128)), full((1, 128)), full((1, 128))],
        out_specs=pl.BlockSpec((1, tt, 512), lambda bi, i: (bi, i, 0)),
        out_shape=jax.ShapeDtypeStruct((b, s, 512), BF16),
        scratch_shapes=[pltpu.VMEM((tt + 8, 1536), F32), pltpu.VMEM((GDN_HEADS, GDN_DK, GDN_DV), F32)],
        compiler_params=_cparams(("parallel", "arbitrary")),
        name="gdn",
    )(p16, p16, p16, p16, p32, conv_w, alog, dtb, norm)


def _rwkv_body(*refs, tt, has_vres):
    if has_vres:
        (r_ref, k_ref, v_ref, lat_ref, misc_ref, vfirst_ref, mu_ref, mulat_ref, w0_ref, wup_ref, a0_ref, aup_ref,
         gup_ref, kk_ref, ka_ref, rk_ref, lng_ref, lnb_ref, vmu_ref, v0_ref, vup_ref, hsum_ref,
         o_ref, ext_ref, extl_ref, extm_ref, st_ref) = refs
        vout_ref = None
    else:
        (r_ref, k_ref, v_ref, lat_ref, mu_ref, mulat_ref, w0_ref, wup_ref, a0_ref, aup_ref,
         gup_ref, kk_ref, ka_ref, rk_ref, lng_ref, lnb_ref, hsum_ref,
         o_ref, vout_ref, ext_ref, extl_ref, st_ref) = refs
    c = CHUNK
    nc = tt // c
    n_pairs = RWKV_HEADS // 2

    @pl.when(pl.program_id(1) == 0)
    def _():
        st_ref[...] = jnp.zeros_like(st_ref)
        ext_ref[0:8, :] = jnp.zeros((8, 1536), F32)
        extl_ref[0:8, :] = jnp.zeros((8, 256), F32)
        if has_vres:
            extm_ref[0:8, :] = jnp.zeros((8, 128), F32)

    ext_ref[8:8 + tt, 0:512] = r_ref[0].astype(F32)
    ext_ref[8:8 + tt, 512:1024] = k_ref[0].astype(F32)
    ext_ref[8:8 + tt, 1024:1536] = v_ref[0].astype(F32)
    cur = ext_ref[8:8 + tt, :]
    rkv = cur + (ext_ref[7:7 + tt, :] - cur) * mu_ref[...]
    ext_ref[0:8, :] = ext_ref[tt:tt + 8, :]
    extl_ref[8:8 + tt, :] = lat_ref[0]
    curl = extl_ref[8:8 + tt, :]
    lat = curl + (extl_ref[7:7 + tt, :] - curl) * mulat_ref[...]
    extl_ref[0:8, :] = extl_ref[tt:tt + 8, :]

    r = rkv[:, 0:512]
    k = rkv[:, 512:1024]
    v = rkv[:, 1024:1536]
    lat_wa = lat[:, 0:128]
    lat_g = lat[:, 128:256]
    w_pre = w0_ref[...] + _dot(_bf(jnp.tanh(lat_wa)), wup_ref[...])
    w = -_softplus(-w_pre) - 0.5
    logw = -jnp.exp(w)
    a = _sigmoid(a0_ref[...] + _dot(_bf(lat_wa), aup_ref[...]))
    g = _dot(_bf(_sigmoid(lat_g)), gup_ref[...])
    if has_vres:
        extm_ref[8:8 + tt, :] = misc_ref[0]
        curm = extm_ref[8:8 + tt, :]
        vlat = curm + (extm_ref[7:7 + tt, :] - curm) * vmu_ref[...]
        extm_ref[0:8, :] = extm_ref[tt:tt + 8, :]
        v_gate = _sigmoid(v0_ref[...] + _dot(_bf(vlat), vup_ref[...]))
        v = v + (vfirst_ref[0] - v) * v_gate
    else:
        vout_ref[0] = v

    hsum = hsum_ref[...]
    kkv = k * kk_ref[...]
    kk = kkv * lax.rsqrt(_dot_sel_r(kkv * kkv, hsum) + NORM_EPS)
    k = k * (1.0 + (a - 1.0) * ka_ref[...])
    bonus = _dot_sel_r(r * k * rk_ref[...], hsum) * v
    bvec = kk * a

    lane = _iota2((1, LANES), 1)
    first = lane < RWKV_N
    rowh = _iota2((LANES, LANES), 0) < RWKV_N
    colh = _iota2((LANES, LANES), 1) < RWKV_N
    blockdiag = rowh == colh
    causal, strict = _tri_masks(c)
    tri = jnp.where(causal, 1.0, 0.0).astype(BF16)

    l_list, pre = [], []
    for ci in range(nc):
        rows = slice(ci * c, (ci + 1) * c)
        lw = logw[rows]
        cum = _dot_sel(tri, lw)
        gam = jnp.exp(cum)
        inv = jnp.exp(-cum)
        gam_ex = jnp.exp(cum - lw)
        gam_c = gam[c - 1:c]
        a_t = -kk[rows] * gam_ex
        r_t = r[rows] * gam
        b_t = bvec[rows] * inv
        k_t = k[rows] * inv
        for p in range(n_pairs):
            ls = slice(p * LANES, (p + 1) * LANES)
            a_p, r_p, b_p, k_p = a_t[:, ls], r_t[:, ls], b_t[:, ls], k_t[:, ls]
            a_h = [jnp.where(first, a_p, 0.0), jnp.where(first, 0.0, a_p)]
            r_h = [jnp.where(first, r_p, 0.0), jnp.where(first, 0.0, r_p)]
            lhs = _bf(jnp.concatenate(a_h + r_h, axis=0))
            m_b = _dot_nt(lhs, _bf(b_p))
            m_k = _dot_nt(lhs, _bf(k_p))
            heads = []
            for hh in range(2):
                l_list.append(jnp.where(strict, -m_b[hh * c:(hh + 1) * c], 0.0))
                heads.append(dict(a_h=a_h[hh], r_h=r_h[hh],
                                  a_ak=_bf(jnp.where(strict, m_k[hh * c:(hh + 1) * c], 0.0)),
                                  a_rb=_bf(jnp.where(causal, m_b[(2 + hh) * c:(3 + hh) * c], 0.0)),
                                  a_rk=_bf(jnp.where(causal, m_k[(2 + hh) * c:(3 + hh) * c], 0.0))))
            pre.append(dict(heads=heads, gam_c=gam_c[:, ls],
                            b_end=_bf(b_p * gam_c[:, ls]), k_end=_bf(k_p * gam_c[:, ls]),
                            v_p=_bf(v[rows, ls])))
    t_all = _unit_lower_inverse(jnp.stack(l_list, axis=0))

    heads_all = [hd for d in pre for hd in d['heads']]
    stack = lambda key: jnp.stack([hd[key] for hd in heads_all], axis=0)
    v_all = jnp.stack([d['v_p'] for d in pre for _ in range(2)], axis=0)
    akv = _bmm(stack('a_ak'), v_all)
    gq = _bf(_bmm(_bf(t_all), _bf(jnp.concatenate([stack('a_h'), akv], axis=2))))
    rb = _bmm(stack('a_rb'), gq)
    h_all = stack('r_h') + rb[:, :, :LANES]
    z_all = rb[:, :, LANES:] + _bmm(stack('a_rk'), v_all)
    seq = []
    for n, d in enumerate(pre):
        g_p = gq[2 * n, :, :LANES] + gq[2 * n + 1, :, :LANES]
        q1_p = jnp.where(first, gq[2 * n, :, LANES:], gq[2 * n + 1, :, LANES:])
        m_p = jnp.where(blockdiag, _dot_tn(g_p, d['b_end']), 0.0)
        n_p = jnp.where(blockdiag, _dot_tn(q1_p, d['b_end']) + _dot_tn(d['v_p'], d['k_end']), 0.0)
        seq.append(dict(h=_bf(h_all[2 * n] + h_all[2 * n + 1]), z=jnp.where(first, z_all[2 * n], z_all[2 * n + 1]),
                        m=_bf(m_p), n=n_p, gam_c=d['gam_c']))

    for ci in range(nc):
        for p in range(n_pairs):
            d = seq[ci * n_pairs + p]
            st = st_ref[p]
            stb = _bf(st)
            y = _dot_nt(d['h'], stb) + d['z']
            st_ref[p] = st * d['gam_c'] + _dot(stb, d['m']) + d['n']
            ext_ref[8 + ci * c:8 + (ci + 1) * c, p * LANES:(p + 1) * LANES] = y

    y = ext_ref[8:8 + tt, 0:512]
    mean = _dot_sel_r(y, hsum) * (1.0 / RWKV_N)
    yc = y - mean
    var = _dot_sel_r(yc * yc, hsum) * (1.0 / RWKV_N)
    yn = yc * lax.rsqrt(var + RWKV_LN_EPS) * lng_ref[...] + lnb_ref[...]
    o_ref[0] = ((yn + bonus) * g).astype(o_ref.dtype)


def _rwkv(p16, p32, v_first, weights, tt, has_vres):
    b, s, _ = p16.shape
    full = lambda shape: pl.BlockSpec(shape, lambda bi, i: (0,) * len(shape))
    tok = lambda w, blk: pl.BlockSpec((1, tt, w), lambda bi, i: (bi, i, blk))
    in_specs = [tok(512, P16_RWKV_R), tok(512, P16_RWKV_R + 1), tok(512, P16_RWKV_R + 2), tok(256, P32_RWKV_LAT)]
    args = [p16, p16, p16, p32]
    if has_vres:
        in_specs += [tok(128, P32_MISC), tok(512, 0)]
        args += [p32, v_first]
    in_specs += [full(w.shape) for w in weights]
    args += list(weights)
    out_specs = [pl.BlockSpec((1, tt, 512), lambda bi, i: (bi, i, 0))]
    out_shape = [jax.ShapeDtypeStruct((b, s, 512), BF16)]
    scratch = [pltpu.VMEM((tt + 8, 1536), F32), pltpu.VMEM((tt + 8, 256), F32)]
    if has_vres:
        scratch.append(pltpu.VMEM((tt + 8, 128), F32))
    else:
        out_specs.append(pl.BlockSpec((1, tt, 512), lambda bi, i: (bi, i, 0)))
        out_shape.append(jax.ShapeDtypeStruct((b, s, 512), F32))
    scratch.append(pltpu.VMEM((RWKV_HEADS // 2, LANES, LANES), F32))
    return pl.pallas_call(
        functools.partial(_rwkv_body, tt=tt, has_vres=has_vres),
        grid=(b, s // tt),
        in_specs=in_specs,
        out_specs=out_specs,
        out_shape=out_shape,
        scratch_shapes=scratch,
        compiler_params=_cparams(("parallel", "arbitrary")),
        name="rwkv7",
    )(*args)


def _layernorm(h, g, b):
    mu = jnp.mean(h, axis=-1, keepdims=True)
    hc = h - mu
    var = jnp.mean(hc * hc, axis=-1, keepdims=True)
    return hc * lax.rsqrt(var + LN_EPS) * g + b


def _merge_body(ya_ref, yb_ref, yc_ref, yd_ref, gate_ref, x_ref, wb_ref, wo_ref, g_ref, b_ref, o_ref):
    merged = None
    for n, y_ref in enumerate((ya_ref, yb_ref, yc_ref, yd_ref)):
        gate = _sigmoid(gate_ref[0, :, n * D_MODEL:(n + 1) * D_MODEL].astype(F32))
        term = gate * _dot(y_ref[0], wb_ref[n])
        merged = term if merged is None else merged + term
    mix = _dot(_bf(merged), wo_ref[...])
    o_ref[0] = _layernorm(DEEPNORM_ALPHA * x_ref[0] + mix, g_ref[...], b_ref[...])


def _merge(ys, p16, x, wb, wo, g, bb, tt):
    b, s, d = x.shape
    full = lambda shape: pl.BlockSpec(shape, lambda bi, i: (0,) * len(shape))
    ytok = pl.BlockSpec((1, tt, 512), lambda bi, i: (bi, i, 0))
    return pl.pallas_call(
        _merge_body,
        grid=(b, s // tt),
        in_specs=[ytok, ytok, ytok, ytok,
                  pl.BlockSpec((1, tt, N_BRANCH * D_MODEL), lambda bi, i: (bi, i, P16_GATES)),
                  pl.BlockSpec((1, tt, d), lambda bi, i: (bi, i, 0)),
                  full((N_BRANCH, BRANCH_WIDTH, D_MODEL)), full((D_MODEL, D_MODEL)), full((1, d)), full((1, d))],
        out_specs=pl.BlockSpec((1, tt, d), lambda bi, i: (bi, i, 0)),
        out_shape=jax.ShapeDtypeStruct((b, s, d), F32),
        compiler_params=_cparams(("parallel", "parallel")),
        name="merge_ln1",
    )(*ys, p16, x, wb, wo, g, bb)


def _moe_body(x_ref, rwt_ref, rbias_ref, tri_ref, wg_ref, wu_ref, wd_ref, g_ref, b_ref, o_ref,
              xb_ref, comb_ref, grp_ref, acc_ref):
    gidx = pl.program_id(2)
    tt = x_ref.shape[1]

    @pl.when(gidx == 0)
    def _():
        x = x_ref[0]
        xb_ref[...] = x.astype(BF16)
        acc_ref[...] = jnp.zeros_like(acc_ref)
        scores = _sigmoid(_dot3_nt(rwt_ref[...], x))
        biased = scores + rbias_ref[...]
        row = _iota2((N_EXPERTS, tt), 0).astype(F32)
        row_group = (_iota2((N_EXPERTS, tt), 0) >> 2).astype(F32)
        neg = -jnp.inf
        gs = []
        for gi in range(N_GROUPS):
            rows = [biased[gi * GROUP_SIZE + j:gi * GROUP_SIZE + j + 1] for j in range(GROUP_SIZE)]
            best = None
            for i0 in range(GROUP_SIZE):
                for i1 in range(i0 + 1, GROUP_SIZE):
                    pair = rows[i0] + rows[i1]
                    best = pair if best is None else jnp.maximum(best, pair)
            gs.append(best)
        gmax = functools.reduce(jnp.maximum, gs)
        best_group = jnp.full((1, tt), float(N_GROUPS), F32)
        for gi in reversed(range(N_GROUPS)):
            best_group = jnp.where(gs[gi] == gmax, float(gi), best_group)
        masked = jnp.where(row_group == best_group, biased, neg)
        m1 = jnp.max(masked, axis=0, keepdims=True)
        i1 = jnp.min(jnp.where(masked == m1, row, float(N_EXPERTS)), axis=0, keepdims=True)
        masked2 = jnp.where(row == i1, neg, masked)
        m2 = jnp.max(masked2, axis=0, keepdims=True)
        i2 = jnp.min(jnp.where(masked2 == m2, row, float(N_EXPERTS)), axis=0, keepdims=True)
        s1 = jnp.sum(jnp.where(row == i1, scores, 0.0), axis=0, keepdims=True)
        s2 = jnp.sum(jnp.where(row == i2, scores, 0.0), axis=0, keepdims=True)
        tot = s1 + s2
        comb_t = jnp.where(row == i1, s1 / tot, 0.0) + jnp.where(row == i2, s2 / tot, 0.0)
        for gi in range(N_GROUPS):
            comb_ref[gi, 0:GROUP_SIZE, :] = comb_t[gi * GROUP_SIZE:(gi + 1) * GROUP_SIZE]
            comb_ref[gi, GROUP_SIZE:8, :] = jnp.zeros((8 - GROUP_SIZE, tt), F32)
        grp_ref[...] = jnp.broadcast_to(best_group, (8, tt))

    member = grp_ref[...] == gidx.astype(F32)
    cnt = _dot(jnp.where(member, 1.0, 0.0).astype(BF16), tri_ref[...])
    pos = jnp.where(member, cnt - 1.0, -1.0)
    n_tok = cnt[0:1, tt - 1:tt][0, 0].astype(jnp.int32)
    eye8 = jnp.where(_iota2((8, 8), 0) == _iota2((8, 8), 1), 1.0, 0.0).astype(BF16)
    p_hi, p_lo = _split2(pos)
    pos_col = (_dot_tn(p_hi, eye8) + _dot_tn(p_lo, eye8))[:, 0:1]
    pos_row = pos[0:1]
    c_hi, c_lo = _split2(comb_ref[gidx])
    xb = xb_ref[...]
    r = MOE_ROWS

    def sub_block(k, carry):
        base = (k * r).astype(F32)
        sel = jnp.where(pos_row == base + _iota2((r, 1), 0).astype(F32), 1.0, 0.0).astype(BF16)
        sel_t = jnp.where(pos_col == base + _iota2((1, r), 1).astype(F32), 1.0, 0.0).astype(BF16)
        xc = _bf(_dot(sel, xb))
        cw = _dot_nt(sel, c_hi) + _dot_nt(sel, c_lo)
        y = None
        for j in range(GROUP_SIZE):
            hid = _silu(_dot(xc, wg_ref[0, j])) * _dot(xc, wu_ref[0, j])
            term = cw[:, j:j + 1] * _dot(_bf(hid), wd_ref[0, j])
            y = term if y is None else y + term
        acc_ref[...] += _dot(sel_t, _bf(y))
        return carry

    lax.fori_loop(0, (n_tok + (r - 1)) // r, sub_block, 0)

    @pl.when(gidx == N_GROUPS - 1)
    def _():
        o_ref[0] = _layernorm(DEEPNORM_ALPHA * x_ref[0] + acc_ref[...], g_ref[...], b_ref[...])


def _moe(x, rwt, rbias, wg, wu, wd, g, bb, tt):
    b, s, d = x.shape
    full = lambda shape: pl.BlockSpec(shape, lambda bi, i, e: (0,) * len(shape))
    once = lambda shape: pl.BlockSpec(shape, lambda bi, i, e: (0,) * len(shape), pipeline_mode=pl.Buffered(1))
    grouped = lambda w: w.reshape((N_GROUPS, GROUP_SIZE) + w.shape[1:])
    idx = np.arange(tt)
    tri = jnp.asarray((idx[:, None] <= idx[None, :]).astype(np.float32), BF16)
    return pl.pallas_call(
        _moe_body,
        grid=(b, s // tt, N_GROUPS),
        in_specs=[pl.BlockSpec((1, tt, d), lambda bi, i, e: (bi, i, 0), pipeline_mode=pl.Buffered(1)),
                  full((N_EXPERTS, d)), full((N_EXPERTS, 1)), once((tt, tt)),
                  pl.BlockSpec((1, GROUP_SIZE, d, D_EXPERT), lambda bi, i, e: (e, 0, 0, 0)),
                  pl.BlockSpec((1, GROUP_SIZE, d, D_EXPERT), lambda bi, i, e: (e, 0, 0, 0)),
                  pl.BlockSpec((1, GROUP_SIZE, D_EXPERT, d), lambda bi, i, e: (e, 0, 0, 0)),
                  full((1, d)), full((1, d))],
        out_specs=pl.BlockSpec((1, tt, d), lambda bi, i, e: (bi, i, 0)),
        out_shape=jax.ShapeDtypeStruct((b, s, d), F32),
        scratch_shapes=[pltpu.VMEM((tt, d), BF16), pltpu.VMEM((N_GROUPS, 8, tt), F32), pltpu.VMEM((8, tt), F32),
                        pltpu.VMEM((tt, d), F32)],
        compiler_params=_cparams(("parallel", "parallel", "arbitrary")),
        name="moe_ln2",
    )(x, rwt, rbias, tri, grouped(wg), grouped(wu), grouped(wd), g, bb)


def _cols(w, name):
    o, n = _OFF[name]
    return w[:, o:o + n]


def _pack_inproj(w_in_l, w_vres_l):
    d = w_in_l.shape[0]
    w16 = jnp.concatenate([_cols(w_in_l, "gates"), _cols(w_in_l, "gla_q"), _cols(w_in_l, "gla_k"),
                           _cols(w_in_l, "gla_v"), _cols(w_in_l, "gla_og"), _cols(w_in_l, "gdn_qkv"),
                           _cols(w_in_l, "gdn_z"), _cols(w_in_l, "rwkv_rkv")], axis=1).astype(BF16)
    vres = jnp.zeros((d, RWKV_V_RANK), F32) if w_vres_l is None else w_vres_l
    misc = jnp.concatenate([_cols(w_in_l, "k_rope"), _cols(w_in_l, "gla_gl"), _cols(w_in_l, "gdn_beta"),
                            _cols(w_in_l, "gdn_a"), vres], axis=1)
    misc = jnp.pad(misc, ((0, 0), (0, LANES - misc.shape[1])))
    w32 = jnp.concatenate([_cols(w_in_l, "q_lat"), _cols(w_in_l, "rwkv_lat"), _cols(w_in_l, "c_kv"), misc],
                          axis=1).astype(BF16)
    return w16, w32


def _row_pad(w, start, total=LANES):
    return jnp.pad(w, ((start, total - start - w.shape[0]), (0, 0)))


def _lane_row(vec, start, total=LANES):
    return jnp.pad(vec, (start, total - start - vec.shape[0])).reshape(1, total)


def _pack_mla(w_uq, w_ukv):
    half = MLA_ROPE // 2
    wq = w_uq.reshape(MLA_Q_RANK, MLA_HEADS, MLA_NOPE + MLA_ROPE)
    nope, r1, r2 = wq[..., :MLA_NOPE], wq[..., MLA_NOPE:MLA_NOPE + half], wq[..., MLA_NOPE + half:]
    pad = jnp.zeros((MLA_Q_RANK, MLA_HEADS, LANES - MLA_NOPE - MLA_ROPE), F32)
    wqa = jnp.concatenate([nope, r1, r2, pad], axis=-1).reshape(MLA_Q_RANK, MLA_HEADS * LANES)
    wqb = jnp.concatenate([jnp.zeros_like(nope), -r2, r1, pad], axis=-1).reshape(MLA_Q_RANK, MLA_HEADS * LANES)
    wkv = w_ukv.reshape(MLA_KV_RANK, MLA_HEADS, MLA_NOPE + MLA_V)
    wuk = jnp.pad(wkv[..., :MLA_NOPE], ((0, 0), (0, 0), (0, LANES - MLA_NOPE))).reshape(MLA_KV_RANK, MLA_HEADS * LANES)
    wuv = wkv[..., MLA_NOPE:].reshape(MLA_KV_RANK, MLA_HEADS * MLA_V).T
    ra = np.zeros((LANES, LANES), np.float32)
    rb = np.zeros((LANES, LANES), np.float32)
    for j in range(half):
        ra[MISC_KROPE + j, MLA_NOPE + j] = 1.0
        ra[MISC_KROPE + half + j, MLA_NOPE + half + j] = 1.0
        rb[MISC_KROPE + half + j, MLA_NOPE + j] = -1.0
        rb[MISC_KROPE + j, MLA_NOPE + half + j] = 1.0
    inv_freq = ROPE_THETA ** (-jnp.arange(half, dtype=F32) / half)
    freq = jnp.concatenate([jnp.zeros((MLA_NOPE,), F32), inv_freq, inv_freq,
                            jnp.zeros((LANES - MLA_NOPE - MLA_ROPE,), F32)]).reshape(1, LANES)
    return (wqa.astype(BF16), wqb.astype(BF16), wuk.astype(BF16), wuv.astype(BF16),
            jnp.asarray(ra, BF16), jnp.asarray(rb, BF16), freq)


def _head_sum_matrix():
    idx = np.arange(RWKV_WIDTH) // RWKV_N
    return jnp.asarray((idx[:, None] == idx[None, :]).astype(np.float32), BF16)


def _tile(s, pref):
    t = min(pref, s)
    assert s % t == 0 and t % CHUNK == 0
    return t


def kernel(x, positions, router_w, w_in, w_in_vres, mla_q_norm, mla_w_uq, mla_kv_norm, mla_w_ukv,
           gla_w_gate_up, gla_b_gate, gla_norm, gdn_conv, gdn_a_log, gdn_dt_bias, gdn_norm,
           rwkv_mu, rwkv_w0, rwkv_w_up, rwkv_a0, rwkv_a_up, rwkv_g_up, rwkv_k_k, rwkv_k_a, rwkv_r_k,
           rwkv_ln_g, rwkv_ln_b, rwkv_vres_mu, rwkv_v0, rwkv_v_up, w_branch, w_out,
           ln1_g, ln1_b, ln2_g, ln2_b, router_bias, moe_w_gate, moe_w_up, moe_w_down):
    b, s, d = x.shape
    assert d == D_MODEL and w_in.shape[2] == IN_COLS
    pos3 = positions.reshape(b, s, 1)
    rwt = router_w.T
    hsum = _head_sum_matrix()
    row = lambda v: v.reshape(1, -1)
    t_proj = _tile(s, 1024)
    t_scan = _tile(s, 256)
    t_attn = _tile(s, 512)
    t_merge = _tile(s, 256)
    t_moe = _tile(s, 1024)

    v_first = None
    for l in range(DEPTH):
        w16, w32 = _pack_inproj(w_in[l], None if l == 0 else w_in_vres[l - 1])
        p16 = _inproj(x, w16, BF16, t_proj, 1024)
        p32 = _inproj(x, w32, F32, t_proj, P32_WIDTH)

        wqa, wqb, wuk, wuv, wkra, wkrb, freq = _pack_mla(mla_w_uq[l], mla_w_ukv[l])
        q, k, v = _mla_prep(p32, pos3, row(mla_q_norm[l]), wqa, wqb, row(mla_kv_norm[l]), wuk, wuv, wkra, wkrb,
                            freq, t_attn)
        y_a = _mla_attn(q, k, v, t_attn)

        wgu = _row_pad(gla_w_gate_up[l], MISC_GLA_GL)
        y_b = _gla(p16, p32, wgu, row(gla_b_gate[l]), row(gla_norm[l]), t_scan)

        y_c = _gdn(p16, p32, gdn_conv[l], _lane_row(gdn_a_log[l], MISC_GDN_A), _lane_row(gdn_dt_bias[l], MISC_GDN_A),
                   row(gdn_norm[l]), t_scan)

        mu = rwkv_mu[l]
        weights = [row(mu[:1536]), row(mu[1536:]), row(rwkv_w0[l]),
                   _row_pad(rwkv_w_up[l], 0).astype(BF16), row(rwkv_a0[l]),
                   _row_pad(rwkv_a_up[l], RWKV_W_RANK).astype(BF16), rwkv_g_up[l].astype(BF16),
                   row(rwkv_k_k[l]), row(rwkv_k_a[l]), row(rwkv_r_k[l]), row(rwkv_ln_g[l]), row(rwkv_ln_b[l])]
        if l == 0:
            y_d, v_first = _rwkv(p16, p32, None, weights + [hsum], t_scan, False)
        else:
            weights += [_lane_row(rwkv_vres_mu[l - 1], MISC_VRES), row(rwkv_v0[l - 1]),
                        _row_pad(rwkv_v_up[l - 1], MISC_VRES).astype(BF16)]
            (y_d,) = _rwkv(p16, p32, v_first, weights + [hsum], t_scan, True)

        x = _merge((y_a, y_b, y_c, y_d), p16, x, w_branch[l].astype(BF16), w_out[l].astype(BF16),
                   row(ln1_g[l]), row(ln1_b[l]), t_merge)
        x = _moe(x, rwt, router_bias[l].reshape(N_EXPERTS, 1), moe_w_gate[l].astype(BF16),
                 moe_w_up[l].astype(BF16), moe_w_down[l].astype(BF16), row(ln2_g[l]), row(ln2_b[l]), t_moe)
    return x
```

```python
import functools
import math

import jax
import jax.numpy as jnp
import numpy as np
from jax import lax
from jax.experimental import pallas as pl
from jax.experimental.pallas import tpu as pltpu

F32 = jnp.float32
BF16 = jnp.bfloat16

LANES = 128
VMEM_LIMIT = 56 * 1024 * 1024

D_MODEL = 1024
DEPTH = 2
MLA_HEADS, MLA_NOPE, MLA_ROPE, MLA_V = 8, 64, 32, 64
MLA_Q_RANK, MLA_KV_RANK = 256, 128
MLA_GROUP = 4
MLA_VT_ROWS = MLA_V + 16
ROPE_THETA = 10000.0
GLA_HEADS, GLA_DK, GLA_DV, GLA_GATE_RANK, GLA_TAU = 4, 64, 128, 16, 16.0
GDN_HEADS, GDN_DK, GDN_DV, GDN_CONV = 4, 128, 128, 4
RWKV_HEADS, RWKV_N = 8, 64
RWKV_W_RANK, RWKV_A_RANK, RWKV_V_RANK, RWKV_G_RANK = 64, 64, 32, 128
RWKV_LN_EPS = 64e-5
CHUNK = 64
N_BRANCH, BRANCH_WIDTH = 4, 512
N_EXPERTS, N_GROUPS, TOP_K, D_EXPERT = 16, 4, 2, 512
GROUP_SIZE = N_EXPERTS // N_GROUPS
MOE_ROWS = 320
DEEPNORM_ALPHA = (2.0 * DEPTH) ** 0.25
LN_EPS = 1e-5
NORM_EPS = 1e-6
RWKV_WIDTH = RWKV_HEADS * RWKV_N
RWKV_SPLIT = 1

_OFF = {}
_o = 0
for _name, _w in (("q_lat", MLA_Q_RANK), ("c_kv", MLA_KV_RANK), ("k_rope", MLA_ROPE),
                  ("gla_q", 256), ("gla_k", 256), ("gla_v", 512), ("gla_gl", GLA_GATE_RANK), ("gla_og", 512),
                  ("gdn_qkv", 1536), ("gdn_beta", GDN_HEADS), ("gdn_a", GDN_HEADS), ("gdn_z", 512),
                  ("rwkv_rkv", 1536), ("rwkv_lat", 256), ("gates", N_BRANCH * D_MODEL)):
    _OFF[_name] = (_o, _w)
    _o += _w
IN_COLS = _o

MISC_KROPE = 0
MISC_GLA_GL = 32
MISC_GDN_BETA = 48
MISC_GDN_A = 52
MISC_VRES = 56

P16_WIDTH = 9216
P16_GATES = 0
P16_GLA_QK = 8
P16_GLA_V = 9
P16_GLA_OG = 10
P16_GDN_Q = 11
P16_GDN_Z = 14
P16_RWKV_R = 15
P32_WIDTH = 768
P32_QLAT = 0
P32_RWKV_LAT = 1
P32_CKV = 4
P32_MISC = 5


def _cparams(sem):
    return pltpu.CompilerParams(dimension_semantics=sem, vmem_limit_bytes=VMEM_LIMIT)


def _dot(a, b):
    return jnp.dot(a, b, preferred_element_type=F32)


def _dot_nt(a, b):
    return lax.dot_general(a, b, (((1,), (1,)), ((), ())), preferred_element_type=F32)


def _dot_tn(a, b):
    return lax.dot_general(a, b, (((0,), (0,)), ((), ())), preferred_element_type=F32)


def _bmm(a, b):
    return jnp.einsum('bij,bjk->bik', a, b, preferred_element_type=F32)


def _bf(x):
    return x.astype(BF16)


def _split2(x):
    hi = x.astype(BF16)
    lo = (x - hi.astype(F32)).astype(BF16)
    return hi, lo


def _dot_sel(m_bf, x):
    hi, lo = _split2(x)
    return _dot(m_bf, hi) + _dot(m_bf, lo)


def _dot_sel_r(x, m_bf):
    hi, lo = _split2(x)
    return _dot(hi, m_bf) + _dot(lo, m_bf)


def _dot3(a, b):
    ah, al = _split2(a)
    bh, bl = _split2(b)
    return _dot(ah, bh) + _dot(al, bh) + _dot(ah, bl)


def _dot3_nt(a, b):
    ah, al = _split2(a)
    bh, bl = _split2(b)
    return _dot_nt(ah, bh) + _dot_nt(al, bh) + _dot_nt(ah, bl)


def _sigmoid(x):
    return 1.0 / (1.0 + jnp.exp(-x))


def _silu(x):
    return x * _sigmoid(x)


def _softplus(x):
    return jnp.maximum(x, 0.0) + jnp.log(1.0 + jnp.exp(-jnp.abs(x)))


def _iota2(shape, dim):
    return lax.broadcasted_iota(jnp.int32, shape, dim)


def _tri_masks(c):
    r = _iota2((c, c), 0)
    q = _iota2((c, c), 1)
    return r >= q, r > q


def _unit_lower_inverse(l_mat):
    nb, c, _ = l_mat.shape
    r = _iota2((c, c), 0)
    q = _iota2((c, c), 1)
    pair = jnp.where((r >> 1) == (q >> 1), 1.0, 0.0).astype(F32)
    t = jnp.where(r == q, 1.0, 0.0).astype(F32)[None] - l_mat * pair[None]
    s = 2
    while s < c:
        sh = s.bit_length()
        same = (r >> sh) == (q >> sh)
        m = jnp.where(same, jnp.where((r & s) != 0, jnp.where((q & s) == 0, 1.0, 0.0), 0.0), 0.0)
        ls = l_mat * m[None]
        tb = _bf(t)
        x = _bmm(_bf(_bmm(tb, _bf(ls))), tb)
        t = t - x
        s *= 2
    return t


def _inproj_body(x_ref, w_ref, o_ref, xb_ref):
    @pl.when(pl.program_id(2) == 0)
    def _():
        xb_ref[...] = x_ref[0].astype(BF16)

    o_ref[0] = _dot(xb_ref[...], w_ref[...]).astype(o_ref.dtype)


def _inproj(x, w, out_dtype, tm, tn):
    b, s, d = x.shape
    n = w.shape[1]
    return pl.pallas_call(
        _inproj_body,
        grid=(b, s // tm, n // tn),
        in_specs=[pl.BlockSpec((1, tm, d), lambda bi, i, j: (bi, i, 0)),
                  pl.BlockSpec((d, tn), lambda bi, i, j: (0, j))],
        out_specs=pl.BlockSpec((1, tm, tn), lambda bi, i, j: (bi, i, j)),
        out_shape=jax.ShapeDtypeStruct((b, s, n), out_dtype),
        scratch_shapes=[pltpu.VMEM((tm, d), BF16)],
        compiler_params=_cparams(("parallel", "parallel", "arbitrary")),
        name="inproj",
    )(x, w)


def _mla_prep_body(qlat_ref, ckv_ref, misc_ref, pos_ref, qnorm_ref, wqa_ref, wqb_ref, kvnorm_ref,
                   wuk_ref, wuv_ref, wkra_ref, wkrb_ref, freq_ref, q_ref, k_ref, v_ref):
    scale = (MLA_NOPE + MLA_ROPE) ** -0.5 * math.log2(math.e)
    ang = pos_ref[0].astype(F32) * freq_ref[...]
    cos_t = jnp.cos(ang)
    sin_t = jnp.sin(ang)

    ql = qlat_ref[0]
    qn = ql * lax.rsqrt(jnp.mean(ql * ql, axis=-1, keepdims=True) + NORM_EPS) * qnorm_ref[...]
    qnb = _bf(qn)
    qa = _dot(qnb, wqa_ref[...])
    qb = _dot(qnb, wqb_ref[...])

    ck = ckv_ref[0]
    kvn = ck * lax.rsqrt(jnp.mean(ck * ck, axis=-1, keepdims=True) + NORM_EPS) * kvnorm_ref[...]
    kvb = _bf(kvn)
    kn = _dot(kvb, wuk_ref[...])
    vt = _dot_nt(wuv_ref[...], kvb)
    misc = misc_ref[0]
    kr = _dot_sel_r(misc, wkra_ref[...]) * cos_t + _dot_sel_r(misc, wkrb_ref[...]) * sin_t

    for h in range(MLA_HEADS):
        sl = slice(h * LANES, (h + 1) * LANES)
        q_ref[0, h] = ((qa[:, sl] * cos_t + qb[:, sl] * sin_t) * scale).astype(BF16)
        k_ref[0, h] = (kn[:, sl] + kr).astype(BF16)
    t = vt.shape[1]
    ones = jnp.ones((MLA_VT_ROWS - MLA_V, t), F32)
    for h in range(MLA_HEADS):
        v_ref[0, h, 0] = jnp.concatenate([vt[h * MLA_V:(h + 1) * MLA_V], ones], axis=0).astype(BF16)


def _mla_prep(p32, pos3, qnorm, wqa, wqb, kvnorm, wuk, wuv, wkra, wkrb, freq, tt):
    b, s, _ = p32.shape
    full = lambda shape: pl.BlockSpec(shape, lambda bi, i: (0,) * len(shape))
    return pl.pallas_call(
        _mla_prep_body,
        grid=(b, s // tt),
        in_specs=[pl.BlockSpec((1, tt, 256), lambda bi, i: (bi, i, P32_QLAT)),
                  pl.BlockSpec((1, tt, 128), lambda bi, i: (bi, i, P32_CKV)),
                  pl.BlockSpec((1, tt, 128), lambda bi, i: (bi, i, P32_MISC)),
                  pl.BlockSpec((1, tt, 1), lambda bi, i: (bi, i, 0)),
                  full((1, 256)), full((256, 1024)), full((256, 1024)), full((1, 128)),
                  full((128, 1024)), full((512, 128)), full((128, 128)), full((128, 128)), full((1, 128))],
        out_specs=[pl.BlockSpec((1, MLA_HEADS, tt, 128), lambda bi, i: (bi, 0, i, 0)),
                   pl.BlockSpec((1, MLA_HEADS, tt, 128), lambda bi, i: (bi, 0, i, 0)),
                   pl.BlockSpec((1, MLA_HEADS, 1, MLA_VT_ROWS, tt), lambda bi, i: (bi, 0, i, 0, 0))],
        out_shape=[jax.ShapeDtypeStruct((b, MLA_HEADS, s, 128), BF16),
                   jax.ShapeDtypeStruct((b, MLA_HEADS, s, 128), BF16),
                   jax.ShapeDtypeStruct((b, MLA_HEADS, s // tt, MLA_VT_ROWS, tt), BF16)],
        compiler_params=_cparams(("parallel", "parallel")),
        name="mla_prep",
    )(p32, p32, p32, pos3, qnorm, wqa, wqb, kvnorm, wuk, wuv, wkra, wkrb, freq)


def _mla_attn_body(q_ref, k_ref, v_ref, o_ref, m_ref, acc_ref, s_ref, *, tq):
    nh = MLA_GROUP
    i = pl.program_id(2)
    for hh in range(nh):
        m_ref[hh] = jnp.full((1, tq), -jnp.inf, F32)
        acc_ref[hh] = jnp.zeros((MLA_VT_ROWS, tq), F32)

    def scores(hh, j):
        start = pl.multiple_of(j * tq, tq)
        return _dot_nt(k_ref[0, hh, pl.ds(start, tq), :], q_ref[0, hh])

    def consume(hh, j, masked):
        st = s_ref[hh]
        if masked:
            st = jnp.where(_iota2((tq, tq), 0) <= _iota2((tq, tq), 1), st, -jnp.inf)
        m_old = m_ref[hh]
        m_new = jnp.maximum(m_old, jnp.max(st, axis=0, keepdims=True))
        p = jnp.exp2(st - m_new)
        acc_ref[hh] = acc_ref[hh] * jnp.exp2(m_old - m_new) + _dot(v_ref[0, hh, j], _bf(p))
        m_ref[hh] = m_new

    s_ref[0] = scores(0, 0)

    def loop_body(j, carry):
        for hh in range(nh):
            if hh + 1 < nh:
                s_ref[hh + 1] = scores(hh + 1, j)
            else:
                s_ref[0] = scores(0, j + 1)
            consume(hh, j, False)
        return carry

    lax.fori_loop(0, i, loop_body, 0)

    for hh in range(nh):
        if hh + 1 < nh:
            s_ref[hh + 1] = scores(hh + 1, i)
        consume(hh, i, True)
    outs = []
    for hh in range(nh):
        a = acc_ref[hh]
        outs.append(a[0:MLA_V] / a[MLA_V:MLA_V + 1])
    o_ref[0] = jnp.concatenate(outs, axis=0).T.astype(o_ref.dtype)


def _mla_attn(q, k, v, tq):
    b, h, s, _ = q.shape
    nh = MLA_GROUP
    once = pl.Buffered(1)
    return pl.pallas_call(
        functools.partial(_mla_attn_body, tq=tq),
        grid=(b, h // nh, s // tq),
        in_specs=[pl.BlockSpec((1, nh, tq, 128), lambda bi, p, i: (bi, p, i, 0)),
                  pl.BlockSpec((1, nh, s, 128), lambda bi, p, i: (bi, p, 0, 0), pipeline_mode=once),
                  pl.BlockSpec((1, nh, s // tq, MLA_VT_ROWS, tq), lambda bi, p, i: (bi, p, 0, 0, 0),
                               pipeline_mode=once)],
        out_specs=pl.BlockSpec((1, tq, nh * MLA_V), lambda bi, p, i: (bi, i, p)),
        out_shape=jax.ShapeDtypeStruct((b, s, h * MLA_V), BF16),
        scratch_shapes=[pltpu.VMEM((nh, 1, tq), F32), pltpu.VMEM((nh, MLA_VT_ROWS, tq), F32),
                        pltpu.VMEM((nh, tq, tq), F32)],
        compiler_params=_cparams(("parallel", "parallel", "arbitrary")),
        name="mla_attn",
    )(q, k, v)


def _gla_body(qk_ref, v_ref, og_ref, misc_ref, wgu_ref, bgate_ref, norm_ref, o_ref, st_ref, *, tt):
    c = CHUNK

    @pl.when(pl.program_id(1) == 0)
    def _():
        st_ref[...] = jnp.zeros_like(st_ref)

    causal, _ = _tri_masks(c)
    tri = jnp.where(causal, 1.0, 0.0).astype(BF16)
    lane = _iota2((1, LANES), 1)
    qk = qk_ref[0].astype(F32)
    q_all = qk[:, :256] * (GLA_DK ** -0.5)
    k_all = qk[:, 256:]
    ga = _dot3(misc_ref[0], wgu_ref[...]) + bgate_ref[...]
    log_a = (jnp.minimum(ga, 0.0) - jnp.log(1.0 + jnp.exp(-jnp.abs(ga)))) * (1.0 / GLA_TAU)

    units = []
    for ci in range(tt // c):
        rows = slice(ci * c, (ci + 1) * c)
        bcum = _dot_sel(tri, log_a[rows])
        b_last = bcum[c - 1:c]
        q_dec = q_all[rows] * jnp.exp(bcum)
        k_inv = k_all[rows] * jnp.exp(-bcum)
        k_end = k_all[rows] * jnp.exp(b_last - bcum)
        e_last = jnp.exp(b_last)
        for p in range(2):
            ls = slice(p * LANES, (p + 1) * LANES)
            qd_p, ki_p, ke_p, el_p = q_dec[:, ls], _bf(k_inv[:, ls]), _bf(k_end[:, ls]), e_last[:, ls]
            for hh in range(2):
                h = 2 * p + hh
                own = (lane >= hh * GLA_DK) & (lane < (hh + 1) * GLA_DK)
                qd_h = _bf(jnp.where(own, qd_p, 0.0))
                v_h = v_ref[0, rows, h * GLA_DV:(h + 1) * GLA_DV]
                units.append(dict(h=h, rows=rows, qd=qd_h, v=v_h, el=el_p,
                                  att=_bf(jnp.where(causal, _dot_nt(qd_h, ki_p), 0.0)),
                                  kv=jnp.where(own, _dot_tn(v_h, ke_p), 0.0)))
    o_intra = _bmm(jnp.stack([u['att'] for u in units], axis=0), jnp.stack([u['v'] for u in units], axis=0))

    for n, u in enumerate(units):
        h, rows = u['h'], u['rows']
        st = st_ref[h]
        o = o_intra[n] + _dot_nt(u['qd'], _bf(st))
        st_ref[h] = st * u['el'] + u['kv']
        o = o * lax.rsqrt(jnp.mean(o * o, axis=-1, keepdims=True) + NORM_EPS) * norm_ref[...]
        og = og_ref[0, rows, h * GLA_DV:(h + 1) * GLA_DV].astype(F32)
        o_ref[0, rows, h * GLA_DV:(h + 1) * GLA_DV] = (o * _silu(og)).astype(o_ref.dtype)


def _gla(p16, p32, wgu, bgate, norm, tt):
    b, s, _ = p16.shape
    full = lambda shape: pl.BlockSpec(shape, lambda bi, i: (0,) * len(shape))
    return pl.pallas_call(
        functools.partial(_gla_body, tt=tt),
        grid=(b, s // tt),
        in_specs=[pl.BlockSpec((1, tt, 512), lambda bi, i: (bi, i, P16_GLA_QK)),
                  pl.BlockSpec((1, tt, 512), lambda bi, i: (bi, i, P16_GLA_V)),
                  pl.BlockSpec((1, tt, 512), lambda bi, i: (bi, i, P16_GLA_OG)),
                  pl.BlockSpec((1, tt, 128), lambda bi, i: (bi, i, P32_MISC)),
                  full((128, 256)), full((1, 256)), full((1, 128))],
        out_specs=pl.BlockSpec((1, tt, 512), lambda bi, i: (bi, i, 0)),
        out_shape=jax.ShapeDtypeStruct((b, s, 512), BF16),
        scratch_shapes=[pltpu.VMEM((GLA_HEADS, GLA_DV, LANES), F32)],
        compiler_params=_cparams(("parallel", "arbitrary")),
        name="gla",
    )(p16, p16, p16, p32, wgu, bgate, norm)


def _gdn_body(q_ref, k_ref, v_ref, z_ref, misc_ref, conv_ref, alog_ref, dtb_ref, norm_ref, o_ref,
              ext_ref, st_ref, *, tt):
    c = CHUNK
    nc = tt // c

    @pl.when(pl.program_id(1) == 0)
    def _():
        st_ref[...] = jnp.zeros_like(st_ref)
        ext_ref[0:8, :] = jnp.zeros((8, 1536), F32)

    causal, strict = _tri_masks(c)
    tri = jnp.where(causal, 1.0, 0.0).astype(BF16)
    ones_cc = jnp.ones((c, c), BF16)
    eye = jnp.where(_iota2((c, c), 0) == _iota2((c, c), 1), 1.0, 0.0).astype(F32)

    ext_ref[8:8 + tt, 0:512] = q_ref[0].astype(F32)
    ext_ref[8:8 + tt, 512:1024] = k_ref[0].astype(F32)
    ext_ref[8:8 + tt, 1024:1536] = v_ref[0].astype(F32)
    conv = None
    for j in range(GDN_CONV):
        term = ext_ref[8 - (GDN_CONV - 1) + j:8 - (GDN_CONV - 1) + j + tt, :] * conv_ref[j:j + 1, :]
        conv = term if conv is None else conv + term
    ext_ref[0:8, :] = ext_ref[tt:tt + 8, :]
    qkv = _silu(conv)

    misc = misc_ref[0]
    beta_all = _sigmoid(misc)
    g_all = -jnp.exp(alog_ref[...]) * _softplus(misc + dtb_ref[...])

    l_list, pre = [], []
    for ci in range(nc):
        rows = slice(ci * c, (ci + 1) * c)
        gc_all = _dot_sel(tri, g_all[rows])
        for h in range(GDN_HEADS):
            ls = slice(h * LANES, (h + 1) * LANES)
            q = qkv[rows, ls]
            k = qkv[rows, 512 + h * LANES:512 + (h + 1) * LANES]
            v = qkv[rows, 1024 + h * LANES:1024 + (h + 1) * LANES]
            q = q * lax.rsqrt(jnp.sum(q * q, axis=-1, keepdims=True) + NORM_EPS) * (GDN_DK ** -0.5)
            k = k * lax.rsqrt(jnp.sum(k * k, axis=-1, keepdims=True) + NORM_EPS)
            beta = beta_all[rows, MISC_GDN_BETA + h:MISC_GDN_BETA + h + 1]
            gc = gc_all[:, MISC_GDN_A + h:MISC_GDN_A + h + 1]
            gc_row = _dot_sel(ones_cc, eye * gc)
            decay = jnp.where(causal, jnp.exp(jnp.minimum(gc - gc_row, 0.0)), 0.0)
            kb = k * beta
            kbf = _bf(k)
            l_list.append(jnp.where(strict, _dot_nt(_bf(kb), kbf) * decay, 0.0))
            att = _dot_nt(_bf(q), kbf) * decay
            eg = jnp.exp(gc)
            g_last = gc[c - 1:c]
            pre.append(dict(rhs=_bf(jnp.concatenate([v * beta, kb * eg], axis=1)), att=_bf(att), q_dec=q * eg,
                            k_end=_bf(k * jnp.exp(g_last - gc)), e_last=jnp.exp(g_last)))
    t_all = _unit_lower_inverse(jnp.stack(l_list, axis=0))

    stack = lambda key: jnp.stack([d[key] for d in pre], axis=0)
    uw = _bf(_bmm(_bf(t_all), stack('rhs')))
    ab = _bmm(stack('att'), uw)
    seq = []
    for n, d in enumerate(pre):
        kw = _dot_tn(d['k_end'], uw[n])
        seq.append(dict(h=_bf(d['q_dec'] - ab[n, :, LANES:]), z=ab[n, :, :LANES], m=_bf(kw[:, LANES:]),
                        n=kw[:, :LANES], e_last=d['e_last']))

    for ci in range(nc):
        rows = slice(ci * c, (ci + 1) * c)
        for h in range(GDN_HEADS):
            d = seq[ci * GDN_HEADS + h]
            st = st_ref[h]
            stb = _bf(st)
            o = _dot(d['h'], stb) + d['z']
            st_ref[h] = st * d['e_last'] - _dot(d['m'], stb) + d['n']
            o = o * lax.rsqrt(jnp.mean(o * o, axis=-1, keepdims=True) + NORM_EPS) * norm_ref[...]
            z = z_ref[0, rows, h * LANES:(h + 1) * LANES].astype(F32)
            o_ref[0, rows, h * LANES:(h + 1) * LANES] = (o * _silu(z)).astype(o_ref.dtype)


def _gdn(p16, p32, conv_w, alog, dtb, norm, tt):
    b, s, _ = p16.shape
    full = lambda shape: pl.BlockSpec(shape, lambda bi, i: (0,) * len(shape))
    return pl.pallas_call(
        functools.partial(_gdn_body, tt=tt),
        grid=(b, s // tt),
        in_specs=[pl.BlockSpec((1, tt, 512), lambda bi, i: (bi, i, P16_GDN_Q)),
                  pl.BlockSpec((1, tt, 512), lambda bi, i: (bi, i, P16_GDN_Q + 1)),
                  pl.BlockSpec((1, tt, 512), lambda bi, i: (bi, i, P16_GDN_Q + 2)),
                  pl.BlockSpec((1, tt, 512), lambda bi, i: (bi, i, P16_GDN_Z)),
                  pl.BlockSpec((1, tt, 128), lambda bi, i: (bi, i, P32_MISC)),
                  full((GDN_CONV, 1536)), full((1, 128)), full((1, 128)), full((1, 128))],
        out_specs=pl.BlockSpec((1, tt, 512), lambda bi, i: (bi, i, 0)),
        out_shape=jax.ShapeDtypeStruct((b, s, 512), BF16),
        scratch_shapes=[pltpu.VMEM((tt + 8, 1536), F32), pltpu.VMEM((GDN_HEADS, GDN_DK, GDN_DV), F32)],
        compiler_params=_cparams(("parallel", "arbitrary")),
        name="gdn",
    )(p16, p16, p16, p16, p32, conv_w, alog, dtb, norm)


def _rwkv_unpack(refs, has_vres):
    if has_vres:
        (r_ref, k_ref, v_ref, lat_ref, misc_ref, vfirst_ref, mu_ref, mulat_ref, w0_ref, wup_ref, a0_ref, aup_ref,
         gup_ref, kk_ref, ka_ref, rk_ref, lng_ref, lnb_ref, vmu_ref, v0_ref, vup_ref, hsum_ref,
         o_ref, ext_ref, extl_ref, extm_ref, y_ref, st_ref) = refs
        vout_ref = None
    else:
        (r_ref, k_ref, v_ref, lat_ref, mu_ref, mulat_ref, w0_ref, wup_ref, a0_ref, aup_ref,
         gup_ref, kk_ref, ka_ref, rk_ref, lng_ref, lnb_ref, hsum_ref,
         o_ref, vout_ref, ext_ref, extl_ref, y_ref, st_ref) = refs
        misc_ref = vfirst_ref = vmu_ref = v0_ref = vup_ref = extm_ref = None
    return (r_ref, k_ref, v_ref, lat_ref, misc_ref, vfirst_ref, mu_ref, mulat_ref, w0_ref, wup_ref, a0_ref, aup_ref,
            gup_ref, kk_ref, ka_ref, rk_ref, lng_ref, lnb_ref, vmu_ref, v0_ref, vup_ref, hsum_ref,
            o_ref, vout_ref, ext_ref, extl_ref, extm_ref, y_ref, st_ref)


def _rwkv_body(*refs, tt, has_vres):
    (r_ref, k_ref, v_ref, lat_ref, misc_ref, _, _, _, _, _, _, _, _, _, _, _, _, _, _, _, _, _,
     _, _, ext_ref, extl_ref, extm_ref, _, st_ref) = _rwkv_unpack(refs, has_vres)

    @pl.when(pl.program_id(1) == 0)
    def _():
        st_ref[...] = jnp.zeros_like(st_ref)
        ext_ref[0:8, :] = jnp.zeros((8, 1536), F32)
        extl_ref[0:8, :] = jnp.zeros((8, 256), F32)
        if has_vres:
            extm_ref[0:8, :] = jnp.zeros((8, 128), F32)

    ext_ref[8:8 + tt, 0:512] = r_ref[0].astype(F32)
    ext_ref[8:8 + tt, 512:1024] = k_ref[0].astype(F32)
    ext_ref[8:8 + tt, 1024:1536] = v_ref[0].astype(F32)
    extl_ref[8:8 + tt, :] = lat_ref[0]
    if has_vres:
        extm_ref[8:8 + tt, :] = misc_ref[0]
    th = tt // RWKV_SPLIT
    for part in range(RWKV_SPLIT):
        _rwkv_rows(refs, part * th, (part + 1) * th, has_vres)
    ext_ref[0:8, :] = ext_ref[tt:tt + 8, :]
    extl_ref[0:8, :] = extl_ref[tt:tt + 8, :]
    if has_vres:
        extm_ref[0:8, :] = extm_ref[tt:tt + 8, :]


def _rwkv_rows(refs, lo, hi, has_vres):
    (r_ref, k_ref, v_ref, lat_ref, misc_ref, vfirst_ref, mu_ref, mulat_ref, w0_ref, wup_ref, a0_ref, aup_ref,
     gup_ref, kk_ref, ka_ref, rk_ref, lng_ref, lnb_ref, vmu_ref, v0_ref, vup_ref, hsum_ref,
     o_ref, vout_ref, ext_ref, extl_ref, extm_ref, y_ref, st_ref) = _rwkv_unpack(refs, has_vres)
    c = CHUNK
    nc = (hi - lo) // c
    n_pairs = RWKV_HEADS // 2

    cur = ext_ref[8 + lo:8 + hi, :]
    rkv = cur + (ext_ref[7 + lo:7 + hi, :] - cur) * mu_ref[...]
    curl = extl_ref[8 + lo:8 + hi, :]
    lat = curl + (extl_ref[7 + lo:7 + hi, :] - curl) * mulat_ref[...]

    r = rkv[:, 0:512]
    k = rkv[:, 512:1024]
    v = rkv[:, 1024:1536]
    lat_wa = lat[:, 0:128]
    lat_g = lat[:, 128:256]
    w_pre = w0_ref[...] + _dot(_bf(jnp.tanh(lat_wa)), wup_ref[...])
    w = -_softplus(-w_pre) - 0.5
    logw = -jnp.exp(w)
    a = _sigmoid(a0_ref[...] + _dot(_bf(lat_wa), aup_ref[...]))
    g = _dot(_bf(_sigmoid(lat_g)), gup_ref[...])
    if has_vres:
        curm = extm_ref[8 + lo:8 + hi, :]
        vlat = curm + (extm_ref[7 + lo:7 + hi, :] - curm) * vmu_ref[...]
        v_gate = _sigmoid(v0_ref[...] + _dot(_bf(vlat), vup_ref[...]))
        v = v + (vfirst_ref[0, lo:hi, :] - v) * v_gate
    else:
        vout_ref[0, lo:hi, :] = v

    hsum = hsum_ref[...]
    kkv = k * kk_ref[...]
    kk = kkv * lax.rsqrt(_dot_sel_r(kkv * kkv, hsum) + NORM_EPS)
    k = k * (1.0 + (a - 1.0) * ka_ref[...])
    bonus = _dot_sel_r(r * k * rk_ref[...], hsum) * v
    bvec = kk * a

    lane = _iota2((1, LANES), 1)
    first = lane < RWKV_N
    rowh = _iota2((LANES, LANES), 0) < RWKV_N
    colh = _iota2((LANES, LANES), 1) < RWKV_N
    blockdiag = rowh == colh
    causal, strict = _tri_masks(c)
    tri = jnp.where(causal, 1.0, 0.0).astype(BF16)

    l_list, pre = [], []
    for ci in range(nc):
        rows = slice(ci * c, (ci + 1) * c)
        lw = logw[rows]
        cum = _dot_sel(tri, lw)
        gam = jnp.exp(cum)
        inv = jnp.exp(-cum)
        gam_ex = jnp.exp(cum - lw)
        gam_c = gam[c - 1:c]
        a_t = -kk[rows] * gam_ex
        r_t = r[rows] * gam
        b_t = bvec[rows] * inv
        k_t = k[rows] * inv
        for p in range(n_pairs):
            ls = slice(p * LANES, (p + 1) * LANES)
            a_p, r_p, b_p, k_p = a_t[:, ls], r_t[:, ls], b_t[:, ls], k_t[:, ls]
            a_h = [jnp.where(first, a_p, 0.0), jnp.where(first, 0.0, a_p)]
            r_h = [jnp.where(first, r_p, 0.0), jnp.where(first, 0.0, r_p)]
            lhs = _bf(jnp.concatenate(a_h + r_h, axis=0))
            m_b = _dot_nt(lhs, _bf(b_p))
            m_k = _dot_nt(lhs, _bf(k_p))
            heads = []
            for hh in range(2):
                l_list.append(jnp.where(strict, -m_b[hh * c:(hh + 1) * c], 0.0))
                heads.append(dict(a_h=a_h[hh], r_h=r_h[hh],
                                  a_ak=_bf(jnp.where(strict, m_k[hh * c:(hh + 1) * c], 0.0)),
                                  a_rb=_bf(jnp.where(causal, m_b[(2 + hh) * c:(3 + hh) * c], 0.0)),
                                  a_rk=_bf(jnp.where(causal, m_k[(2 + hh) * c:(3 + hh) * c], 0.0))))
            pre.append(dict(heads=heads, gam_c=gam_c[:, ls],
                            b_end=_bf(b_p * gam_c[:, ls]), k_end=_bf(k_p * gam_c[:, ls]),
                            v_p=_bf(v[rows, ls])))
    t_all = _unit_lower_inverse(jnp.stack(l_list, axis=0))

    heads_all = [hd for d in pre for hd in d['heads']]
    stack = lambda key: jnp.stack([hd[key] for hd in heads_all], axis=0)
    v_all = jnp.stack([d['v_p'] for d in pre for _ in range(2)], axis=0)
    akv = _bmm(stack('a_ak'), v_all)
    gq = _bf(_bmm(_bf(t_all), _bf(jnp.concatenate([stack('a_h'), akv], axis=2))))
    rb = _bmm(stack('a_rb'), gq)
    h_all = stack('r_h') + rb[:, :, :LANES]
    z_all = rb[:, :, LANES:] + _bmm(stack('a_rk'), v_all)
    seq = []
    for n, d in enumerate(pre):
        g_p = gq[2 * n, :, :LANES] + gq[2 * n + 1, :, :LANES]
        q1_p = jnp.where(first, gq[2 * n, :, LANES:], gq[2 * n + 1, :, LANES:])
        m_p = jnp.where(blockdiag, _dot_tn(g_p, d['b_end']), 0.0)
        n_p = jnp.where(blockdiag, _dot_tn(q1_p, d['b_end']) + _dot_tn(d['v_p'], d['k_end']), 0.0)
        seq.append(dict(h=_bf(h_all[2 * n] + h_all[2 * n + 1]), z=jnp.where(first, z_all[2 * n], z_all[2 * n + 1]),
                        m=_bf(m_p), n=n_p, gam_c=d['gam_c']))

    for ci in range(nc):
        for p in range(n_pairs):
            d = seq[ci * n_pairs + p]
            st = st_ref[p]
            stb = _bf(st)
            y = _dot_nt(d['h'], stb) + d['z']
            st_ref[p] = st * d['gam_c'] + _dot(stb, d['m']) + d['n']
            y_ref[lo + ci * c:lo + (ci + 1) * c, p * LANES:(p + 1) * LANES] = y

    y = y_ref[lo:hi, :]
    mean = _dot_sel_r(y, hsum) * (1.0 / RWKV_N)
    yc = y - mean
    var = _dot_sel_r(yc * yc, hsum) * (1.0 / RWKV_N)
    yn = yc * lax.rsqrt(var + RWKV_LN_EPS) * lng_ref[...] + lnb_ref[...]
    o_ref[0, lo:hi, :] = ((yn + bonus) * g).astype(o_ref.dtype)


def _rwkv(p16, p32, v_first, weights, tt, has_vres):
    b, s, _ = p16.shape
    full = lambda shape: pl.BlockSpec(shape, lambda bi, i: (0,) * len(shape))
    tok = lambda w, blk: pl.BlockSpec((1, tt, w), lambda bi, i: (bi, i, blk))
    in_specs = [tok(512, P16_RWKV_R), tok(512, P16_RWKV_R + 1), tok(512, P16_RWKV_R + 2), tok(256, P32_RWKV_LAT)]
    args = [p16, p16, p16, p32]
    if has_vres:
        in_specs += [tok(128, P32_MISC), tok(512, 0)]
        args += [p32, v_first]
    in_specs += [full(w.shape) for w in weights]
    args += list(weights)
    out_specs = [pl.BlockSpec((1, tt, 512), lambda bi, i: (bi, i, 0))]
    out_shape = [jax.ShapeDtypeStruct((b, s, 512), BF16)]
    scratch = [pltpu.VMEM((tt + 8, 1536), F32), pltpu.VMEM((tt + 8, 256), F32)]
    if has_vres:
        scratch.append(pltpu.VMEM((tt + 8, 128), F32))
    else:
        out_specs.append(pl.BlockSpec((1, tt, 512), lambda bi, i: (bi, i, 0)))
        out_shape.append(jax.ShapeDtypeStruct((b, s, 512), F32))
    scratch.append(pltpu.VMEM((tt, RWKV_WIDTH), F32))
    scratch.append(pltpu.VMEM((RWKV_HEADS // 2, LANES, LANES), F32))
    return pl.pallas_call(
        functools.partial(_rwkv_body, tt=tt, has_vres=has_vres),
        grid=(b, s // tt),
        in_specs=in_specs,
        out_specs=out_specs,
        out_shape=out_shape,
        scratch_shapes=scratch,
        compiler_params=_cparams(("parallel", "arbitrary")),
        name="rwkv7",
    )(*args)


def _layernorm(h, g, b):
    mu = jnp.mean(h, axis=-1, keepdims=True)
    hc = h - mu
    var = jnp.mean(hc * hc, axis=-1, keepdims=True)
    return hc * lax.rsqrt(var + LN_EPS) * g + b


def _merge_body(ya_ref, yb_ref, yc_ref, yd_ref, gate_ref, x_ref, wb_ref, wo_ref, g_ref, b_ref, o_ref):
    merged = None
    for n, y_ref in enumerate((ya_ref, yb_ref, yc_ref, yd_ref)):
        gate = _sigmoid(gate_ref[0, :, n * D_MODEL:(n + 1) * D_MODEL].astype(F32))
        term = gate * _dot(y_ref[0], wb_ref[n])
        merged = term if merged is None else merged + term
    mix = _dot(_bf(merged), wo_ref[...])
    o_ref[0] = _layernorm(DEEPNORM_ALPHA * x_ref[0] + mix, g_ref[...], b_ref[...])


def _merge(ys, p16, x, wb, wo, g, bb, tt):
    b, s, d = x.shape
    full = lambda shape: pl.BlockSpec(shape, lambda bi, i: (0,) * len(shape))
    ytok = pl.BlockSpec((1, tt, 512), lambda bi, i: (bi, i, 0))
    return pl.pallas_call(
        _merge_body,
        grid=(b, s // tt),
        in_specs=[ytok, ytok, ytok, ytok,
                  pl.BlockSpec((1, tt, N_BRANCH * D_MODEL), lambda bi, i: (bi, i, P16_GATES)),
                  pl.BlockSpec((1, tt, d), lambda bi, i: (bi, i, 0)),
                  full((N_BRANCH, BRANCH_WIDTH, D_MODEL)), full((D_MODEL, D_MODEL)), full((1, d)), full((1, d))],
        out_specs=pl.BlockSpec((1, tt, d), lambda bi, i: (bi, i, 0)),
        out_shape=jax.ShapeDtypeStruct((b, s, d), F32),
        compiler_params=_cparams(("parallel", "parallel")),
        name="merge_ln1",
    )(*ys, p16, x, wb, wo, g, bb)


def _moe_body(x_ref, rwt_ref, rbias_ref, tri_ref, wg_ref, wu_ref, wd_ref, g_ref, b_ref, o_ref,
              xb_ref, comb_ref, grp_ref, acc_ref):
    gidx = pl.program_id(2)
    tt = x_ref.shape[1]

    @pl.when(gidx == 0)
    def _():
        x = x_ref[0]
        xb_ref[...] = x.astype(BF16)
        acc_ref[...] = jnp.zeros_like(acc_ref)
        scores = _sigmoid(_dot3_nt(rwt_ref[...], x))
        biased = scores + rbias_ref[...]
        row = _iota2((N_EXPERTS, tt), 0).astype(F32)
        row_group = (_iota2((N_EXPERTS, tt), 0) >> 2).astype(F32)
        neg = -jnp.inf
        gs = []
        for gi in range(N_GROUPS):
            rows = [biased[gi * GROUP_SIZE + j:gi * GROUP_SIZE + j + 1] for j in range(GROUP_SIZE)]
            best = None
            for i0 in range(GROUP_SIZE):
                for i1 in range(i0 + 1, GROUP_SIZE):
                    pair = rows[i0] + rows[i1]
                    best = pair if best is None else jnp.maximum(best, pair)
            gs.append(best)
        gmax = functools.reduce(jnp.maximum, gs)
        best_group = jnp.full((1, tt), float(N_GROUPS), F32)
        for gi in reversed(range(N_GROUPS)):
            best_group = jnp.where(gs[gi] == gmax, float(gi), best_group)
        masked = jnp.where(row_group == best_group, biased, neg)
        m1 = jnp.max(masked, axis=0, keepdims=True)
        i1 = jnp.min(jnp.where(masked == m1, row, float(N_EXPERTS)), axis=0, keepdims=True)
        masked2 = jnp.where(row == i1, neg, masked)
        m2 = jnp.max(masked2, axis=0, keepdims=True)
        i2 = jnp.min(jnp.where(masked2 == m2, row, float(N_EXPERTS)), axis=0, keepdims=True)
        s1 = jnp.sum(jnp.where(row == i1, scores, 0.0), axis=0, keepdims=True)
        s2 = jnp.sum(jnp.where(row == i2, scores, 0.0), axis=0, keepdims=True)
        tot = s1 + s2
        comb_t = jnp.where(row == i1, s1 / tot, 0.0) + jnp.where(row == i2, s2 / tot, 0.0)
        for gi in range(N_GROUPS):
            comb_ref[gi, 0:GROUP_SIZE, :] = comb_t[gi * GROUP_SIZE:(gi + 1) * GROUP_SIZE]
            comb_ref[gi, GROUP_SIZE:8, :] = jnp.zeros((8 - GROUP_SIZE, tt), F32)
        grp_ref[...] = jnp.broadcast_to(best_group, (8, tt))

    member = grp_ref[...] == gidx.astype(F32)
    cnt = _dot(jnp.where(member, 1.0, 0.0).astype(BF16), tri_ref[...])
    pos = jnp.where(member, cnt - 1.0, -1.0)
    n_tok = cnt[0:1, tt - 1:tt][0, 0].astype(jnp.int32)
    eye8 = jnp.where(_iota2((8, 8), 0) == _iota2((8, 8), 1), 1.0, 0.0).astype(BF16)
    p_hi, p_lo = _split2(pos)
    pos_col = (_dot_tn(p_hi, eye8) + _dot_tn(p_lo, eye8))[:, 0:1]
    pos_row = pos[0:1]
    c_hi, c_lo = _split2(comb_ref[gidx])
    xb = xb_ref[...]
    r = MOE_ROWS

    def sub_block(k, carry):
        base = (k * r).astype(F32)
        sel = jnp.where(pos_row == base + _iota2((r, 1), 0).astype(F32), 1.0, 0.0).astype(BF16)
        sel_t = jnp.where(pos_col == base + _iota2((1, r), 1).astype(F32), 1.0, 0.0).astype(BF16)
        xc = _bf(_dot(sel, xb))
        cw = _dot_nt(sel, c_hi) + _dot_nt(sel, c_lo)
        y = None
        for j in range(GROUP_SIZE):
            hid = _silu(_dot(xc, wg_ref[0, j])) * _dot(xc, wu_ref[0, j])
            term = cw[:, j:j + 1] * _dot(_bf(hid), wd_ref[0, j])
            y = term if y is None else y + term
        acc_ref[...] += _dot(sel_t, _bf(y))
        return carry

    lax.fori_loop(0, (n_tok + (r - 1)) // r, sub_block, 0)

    @pl.when(gidx == N_GROUPS - 1)
    def _():
        o_ref[0] = _layernorm(DEEPNORM_ALPHA * x_ref[0] + acc_ref[...], g_ref[...], b_ref[...])


def _moe(x, rwt, rbias, wg, wu, wd, g, bb, tt):
    b, s, d = x.shape
    full = lambda shape: pl.BlockSpec(shape, lambda bi, i, e: (0,) * len(shape))
    once = lambda shape: pl.BlockSpec(shape, lambda bi, i, e: (0,) * len(shape), pipeline_mode=pl.Buffered(1))
    grouped = lambda w: w.reshape((N_GROUPS, GROUP_SIZE) + w.shape[1:])
    idx = np.arange(tt)
    tri = jnp.asarray((idx[:, None] <= idx[None, :]).astype(np.float32), BF16)
    return pl.pallas_call(
        _moe_body,
        grid=(b, s // tt, N_GROUPS),
        in_specs=[pl.BlockSpec((1, tt, d), lambda bi, i, e: (bi, i, 0), pipeline_mode=pl.Buffered(1)),
                  full((N_EXPERTS, d)), full((N_EXPERTS, 1)), once((tt, tt)),
                  pl.BlockSpec((1, GROUP_SIZE, d, D_EXPERT), lambda bi, i, e: (e, 0, 0, 0)),
                  pl.BlockSpec((1, GROUP_SIZE, d, D_EXPERT), lambda bi, i, e: (e, 0, 0, 0)),
                  pl.BlockSpec((1, GROUP_SIZE, D_EXPERT, d), lambda bi, i, e: (e, 0, 0, 0)),
                  full((1, d)), full((1, d))],
        out_specs=pl.BlockSpec((1, tt, d), lambda bi, i, e: (bi, i, 0)),
        out_shape=jax.ShapeDtypeStruct((b, s, d), F32),
        scratch_shapes=[pltpu.VMEM((tt, d), BF16), pltpu.VMEM((N_GROUPS, 8, tt), F32), pltpu.VMEM((8, tt), F32),
                        pltpu.VMEM((tt, d), F32)],
        compiler_params=_cparams(("parallel", "parallel", "arbitrary")),
        name="moe_ln2",
    )(x, rwt, rbias, tri, grouped(wg), grouped(wu), grouped(wd), g, bb)


def _cols(w, name):
    o, n = _OFF[name]
    return w[:, o:o + n]


def _pack_inproj(w_in_l, w_vres_l):
    d = w_in_l.shape[0]
    w16 = jnp.concatenate([_cols(w_in_l, "gates"), _cols(w_in_l, "gla_q"), _cols(w_in_l, "gla_k"),
                           _cols(w_in_l, "gla_v"), _cols(w_in_l, "gla_og"), _cols(w_in_l, "gdn_qkv"),
                           _cols(w_in_l, "gdn_z"), _cols(w_in_l, "rwkv_rkv")], axis=1).astype(BF16)
    vres = jnp.zeros((d, RWKV_V_RANK), F32) if w_vres_l is None else w_vres_l
    misc = jnp.concatenate([_cols(w_in_l, "k_rope"), _cols(w_in_l, "gla_gl"), _cols(w_in_l, "gdn_beta"),
                            _cols(w_in_l, "gdn_a"), vres], axis=1)
    misc = jnp.pad(misc, ((0, 0), (0, LANES - misc.shape[1])))
    w32 = jnp.concatenate([_cols(w_in_l, "q_lat"), _cols(w_in_l, "rwkv_lat"), _cols(w_in_l, "c_kv"), misc],
                          axis=1).astype(BF16)
    return w16, w32


def _row_pad(w, start, total=LANES):
    return jnp.pad(w, ((start, total - start - w.shape[0]), (0, 0)))


def _lane_row(vec, start, total=LANES):
    return jnp.pad(vec, (start, total - start - vec.shape[0])).reshape(1, total)


def _pack_mla(w_uq, w_ukv):
    half = MLA_ROPE // 2
    wq = w_uq.reshape(MLA_Q_RANK, MLA_HEADS, MLA_NOPE + MLA_ROPE)
    nope, r1, r2 = wq[..., :MLA_NOPE], wq[..., MLA_NOPE:MLA_NOPE + half], wq[..., MLA_NOPE + half:]
    pad = jnp.zeros((MLA_Q_RANK, MLA_HEADS, LANES - MLA_NOPE - MLA_ROPE), F32)
    wqa = jnp.concatenate([nope, r1, r2, pad], axis=-1).reshape(MLA_Q_RANK, MLA_HEADS * LANES)
    wqb = jnp.concatenate([jnp.zeros_like(nope), -r2, r1, pad], axis=-1).reshape(MLA_Q_RANK, MLA_HEADS * LANES)
    wkv = w_ukv.reshape(MLA_KV_RANK, MLA_HEADS, MLA_NOPE + MLA_V)
    wuk = jnp.pad(wkv[..., :MLA_NOPE], ((0, 0), (0, 0), (0, LANES - MLA_NOPE))).reshape(MLA_KV_RANK, MLA_HEADS * LANES)
    wuv = wkv[..., MLA_NOPE:].reshape(MLA_KV_RANK, MLA_HEADS * MLA_V).T
    ra = np.zeros((LANES, LANES), np.float32)
    rb = np.zeros((LANES, LANES), np.float32)
    for j in range(half):
        ra[MISC_KROPE + j, MLA_NOPE + j] = 1.0
        ra[MISC_KROPE + half + j, MLA_NOPE + half + j] = 1.0
        rb[MISC_KROPE + half + j, MLA_NOPE + j] = -1.0
        rb[MISC_KROPE + j, MLA_NOPE + half + j] = 1.0
    inv_freq = ROPE_THETA ** (-jnp.arange(half, dtype=F32) / half)
    freq = jnp.concatenate([jnp.zeros((MLA_NOPE,), F32), inv_freq, inv_freq,
                            jnp.zeros((LANES - MLA_NOPE - MLA_ROPE,), F32)]).reshape(1, LANES)
    return (wqa.astype(BF16), wqb.astype(BF16), wuk.astype(BF16), wuv.astype(BF16),
            jnp.asarray(ra, BF16), jnp.asarray(rb, BF16), freq)


def _head_sum_matrix():
    idx = np.arange(RWKV_WIDTH) // RWKV_N
    return jnp.asarray((idx[:, None] == idx[None, :]).astype(np.float32), BF16)


def _tile(s, pref):
    t = min(pref, s)
    assert s % t == 0 and t % CHUNK == 0
    return t


def kernel(x, positions, router_w, w_in, w_in_vres, mla_q_norm, mla_w_uq, mla_kv_norm, mla_w_ukv,
           gla_w_gate_up, gla_b_gate, gla_norm, gdn_conv, gdn_a_log, gdn_dt_bias, gdn_norm,
           rwkv_mu, rwkv_w0, rwkv_w_up, rwkv_a0, rwkv_a_up, rwkv_g_up, rwkv_k_k, rwkv_k_a, rwkv_r_k,
           rwkv_ln_g, rwkv_ln_b, rwkv_vres_mu, rwkv_v0, rwkv_v_up, w_branch, w_out,
           ln1_g, ln1_b, ln2_g, ln2_b, router_bias, moe_w_gate, moe_w_up, moe_w_down):
    b, s, d = x.shape
    assert d == D_MODEL and w_in.shape[2] == IN_COLS
    pos3 = positions.reshape(b, s, 1)
    rwt = router_w.T
    hsum = _head_sum_matrix()
    row = lambda v: v.reshape(1, -1)
    t_proj = _tile(s, 1024)
    t_scan = _tile(s, 256)
    t_rwkv = _tile(s, 256)
    t_attn = _tile(s, 512)
    t_merge = _tile(s, 256)
    t_moe = _tile(s, 1024)

    v_first = None
    for l in range(DEPTH):
        w16, w32 = _pack_inproj(w_in[l], None if l == 0 else w_in_vres[l - 1])
        p16 = _inproj(x, w16, BF16, t_proj, 1024)
        p32 = _inproj(x, w32, F32, t_proj, P32_WIDTH)

        wqa, wqb, wuk, wuv, wkra, wkrb, freq = _pack_mla(mla_w_uq[l], mla_w_ukv[l])
        q, k, v = _mla_prep(p32, pos3, row(mla_q_norm[l]), wqa, wqb, row(mla_kv_norm[l]), wuk, wuv, wkra, wkrb,
                            freq, t_attn)
        y_a = _mla_attn(q, k, v, t_attn)

        wgu = _row_pad(gla_w_gate_up[l], MISC_GLA_GL)
        y_b = _gla(p16, p32, wgu, row(gla_b_gate[l]), row(gla_norm[l]), t_scan)

        y_c = _gdn(p16, p32, gdn_conv[l], _lane_row(gdn_a_log[l], MISC_GDN_A), _lane_row(gdn_dt_bias[l], MISC_GDN_A),
                   row(gdn_norm[l]), t_scan)

        mu = rwkv_mu[l]
        weights = [row(mu[:1536]), row(mu[1536:]), row(rwkv_w0[l]),
                   _row_pad(rwkv_w_up[l], 0).astype(BF16), row(rwkv_a0[l]),
                   _row_pad(rwkv_a_up[l], RWKV_W_RANK).astype(BF16), rwkv_g_up[l].astype(BF16),
                   row(rwkv_k_k[l]), row(rwkv_k_a[l]), row(rwkv_r_k[l]), row(rwkv_ln_g[l]), row(rwkv_ln_b[l])]
        if l == 0:
            y_d, v_first = _rwkv(p16, p32, None, weights + [hsum], t_rwkv, False)
        else:
            weights += [_lane_row(rwkv_vres_mu[l - 1], MISC_VRES), row(rwkv_v0[l - 1]),
                        _row_pad(rwkv_v_up[l - 1], MISC_VRES).astype(BF16)]
            (y_d,) = _rwkv(p16, p32, v_first, weights + [hsum], t_rwkv, True)

        x = _merge((y_a, y_b, y_c, y_d), p16, x, w_branch[l].astype(BF16), w_out[l].astype(BF16),
                   row(ln1_g[l]), row(ln1_b[l]), t_merge)
        x = _moe(x, rwt, router_bias[l].reshape(N_EXPERTS, 1), moe_w_gate[l].astype(BF16),
                 moe_w_up[l].astype(BF16), moe_w_down[l].astype(BF16), row(ln2_g[l]), row(ln2_b[l]), t_moe)
    return x
```

```python
import functools
import math

import jax
import jax.numpy as jnp
import numpy as np
from jax import lax
from jax.experimental import pallas as pl
from jax.experimental.pallas import tpu as pltpu

F32 = jnp.float32
BF16 = jnp.bfloat16

LANES = 128
VMEM_LIMIT = 56 * 1024 * 1024

D_MODEL = 1024
DEPTH = 2
MLA_HEADS, MLA_NOPE, MLA_ROPE, MLA_V = 8, 64, 32, 64
MLA_Q_RANK, MLA_KV_RANK = 256, 128
MLA_GROUP = 4
MLA_VT_ROWS = MLA_V + 16
ROPE_THETA = 10000.0
GLA_HEADS, GLA_DK, GLA_DV, GLA_GATE_RANK, GLA_TAU = 4, 64, 128, 16, 16.0
GDN_HEADS, GDN_DK, GDN_DV, GDN_CONV = 4, 128, 128, 4
RWKV_HEADS, RWKV_N = 8, 64
RWKV_W_RANK, RWKV_A_RANK, RWKV_V_RANK, RWKV_G_RANK = 64, 64, 32, 128
RWKV_LN_EPS = 64e-5
CHUNK = 64
N_BRANCH, BRANCH_WIDTH = 4, 512
N_EXPERTS, N_GROUPS, TOP_K, D_EXPERT = 16, 4, 2, 512
GROUP_SIZE = N_EXPERTS // N_GROUPS
MOE_ROWS = 320
DEEPNORM_ALPHA = (2.0 * DEPTH) ** 0.25
LN_EPS = 1e-5
NORM_EPS = 1e-6
RWKV_WIDTH = RWKV_HEADS * RWKV_N
RWKV_SPLIT = 1
RWKV_PREP_PER_STAGE = 2

_OFF = {}
_o = 0
for _name, _w in (("q_lat", MLA_Q_RANK), ("c_kv", MLA_KV_RANK), ("k_rope", MLA_ROPE),
                  ("gla_q", 256), ("gla_k", 256), ("gla_v", 512), ("gla_gl", GLA_GATE_RANK), ("gla_og", 512),
                  ("gdn_qkv", 1536), ("gdn_beta", GDN_HEADS), ("gdn_a", GDN_HEADS), ("gdn_z", 512),
                  ("rwkv_rkv", 1536), ("rwkv_lat", 256), ("gates", N_BRANCH * D_MODEL)):
    _OFF[_name] = (_o, _w)
    _o += _w
IN_COLS = _o

MISC_KROPE = 0
MISC_GLA_GL = 32
MISC_GDN_BETA = 48
MISC_GDN_A = 52
MISC_VRES = 56

P16_WIDTH = 9216
P16_GATES = 0
P16_GLA_QK = 8
P16_GLA_V = 9
P16_GLA_OG = 10
P16_GDN_Q = 11
P16_GDN_Z = 14
P16_RWKV_R = 15
P32_WIDTH = 768
P32_QLAT = 0
P32_RWKV_LAT = 1
P32_CKV = 4
P32_MISC = 5


def _cparams(sem):
    return pltpu.CompilerParams(dimension_semantics=sem, vmem_limit_bytes=VMEM_LIMIT)


def _dot(a, b):
    return jnp.dot(a, b, preferred_element_type=F32)


def _dot_nt(a, b):
    return lax.dot_general(a, b, (((1,), (1,)), ((), ())), preferred_element_type=F32)


def _dot_tn(a, b):
    return lax.dot_general(a, b, (((0,), (0,)), ((), ())), preferred_element_type=F32)


def _bmm(a, b):
    return jnp.einsum('bij,bjk->bik', a, b, preferred_element_type=F32)


def _bf(x):
    return x.astype(BF16)


def _split2(x):
    hi = x.astype(BF16)
    lo = (x - hi.astype(F32)).astype(BF16)
    return hi, lo


def _dot_sel(m_bf, x):
    hi, lo = _split2(x)
    return _dot(m_bf, hi) + _dot(m_bf, lo)


def _dot_sel_r(x, m_bf):
    hi, lo = _split2(x)
    return _dot(hi, m_bf) + _dot(lo, m_bf)


def _dot3(a, b):
    ah, al = _split2(a)
    bh, bl = _split2(b)
    return _dot(ah, bh) + _dot(al, bh) + _dot(ah, bl)


def _dot3_nt(a, b):
    ah, al = _split2(a)
    bh, bl = _split2(b)
    return _dot_nt(ah, bh) + _dot_nt(al, bh) + _dot_nt(ah, bl)


def _sigmoid(x):
    return 1.0 / (1.0 + jnp.exp(-x))


def _silu(x):
    return x * _sigmoid(x)


def _softplus(x):
    return jnp.maximum(x, 0.0) + jnp.log(1.0 + jnp.exp(-jnp.abs(x)))


def _iota2(shape, dim):
    return lax.broadcasted_iota(jnp.int32, shape, dim)


def _tri_masks(c):
    r = _iota2((c, c), 0)
    q = _iota2((c, c), 1)
    return r >= q, r > q


def _unit_lower_inverse(l_mat):
    t = None
    for t in _unit_lower_inverse_levels(l_mat):
        pass
    return t


def _unit_lower_inverse_levels(l_mat):
    c = l_mat.shape[-1]
    r = _iota2((c, c), 0)
    q = _iota2((c, c), 1)
    pair = jnp.where((r >> 1) == (q >> 1), 1.0, 0.0).astype(F32)
    t = jnp.where(r == q, 1.0, 0.0).astype(F32)[None] - l_mat * pair[None]
    s = 2
    while s < c:
        sh = s.bit_length()
        same = (r >> sh) == (q >> sh)
        m = jnp.where(same, jnp.where((r & s) != 0, jnp.where((q & s) == 0, 1.0, 0.0), 0.0), 0.0)
        ls = l_mat * m[None]
        tb = _bf(t)
        x = _bmm(_bf(_bmm(tb, _bf(ls))), tb)
        t = t - x
        s *= 2
        yield t


def _advance(gen, stop_tags):
    for tag in gen:
        if tag in stop_tags:
            return tag
    return None


def _interleave_ranges(parts, prep_per_stage):
    _advance(parts[0], ('boundary',))
    for k, cur in enumerate(parts):
        nxt = parts[k + 1] if k + 1 < len(parts) else None
        cur_tag = 'm'
        nxt_tag = 'v' if nxt is not None else 'boundary'
        while cur_tag != 'seq' or nxt_tag != 'boundary':
            if cur_tag != 'seq':
                cur_tag = _advance(cur, ('m', 'seq'))
            for _ in range(prep_per_stage):
                if nxt_tag != 'boundary':
                    nxt_tag = _advance(nxt, ('v', 'boundary'))
        _advance(cur, ())


def _inproj_body(x_ref, w_ref, o_ref, xb_ref):
    @pl.when(pl.program_id(2) == 0)
    def _():
        xb_ref[...] = x_ref[0].astype(BF16)

    o_ref[0] = _dot(xb_ref[...], w_ref[...]).astype(o_ref.dtype)


def _inproj(x, w, out_dtype, tm, tn):
    b, s, d = x.shape
    n = w.shape[1]
    return pl.pallas_call(
        _inproj_body,
        grid=(b, s // tm, n // tn),
        in_specs=[pl.BlockSpec((1, tm, d), lambda bi, i, j: (bi, i, 0)),
                  pl.BlockSpec((d, tn), lambda bi, i, j: (0, j))],
        out_specs=pl.BlockSpec((1, tm, tn), lambda bi, i, j: (bi, i, j)),
        out_shape=jax.ShapeDtypeStruct((b, s, n), out_dtype),
        scratch_shapes=[pltpu.VMEM((tm, d), BF16)],
        compiler_params=_cparams(("parallel", "parallel", "arbitrary")),
        name="inproj",
    )(x, w)


def _rope_table_body(pos_ref, freq_ref, cos_ref, sin_ref):
    ang = pos_ref[0].astype(F32) * freq_ref[...]
    cos_ref[0] = jnp.cos(ang)
    sin_ref[0] = jnp.sin(ang)


def _rope_tables(pos3, freq, tt):
    b, s, _ = pos3.shape
    out = pl.BlockSpec((1, tt, LANES), lambda bi, i: (bi, i, 0))
    return pl.pallas_call(
        _rope_table_body,
        grid=(b, s // tt),
        in_specs=[pl.BlockSpec((1, tt, 1), lambda bi, i: (bi, i, 0)), pl.BlockSpec((1, LANES), lambda bi, i: (0, 0))],
        out_specs=[out, out],
        out_shape=[jax.ShapeDtypeStruct((b, s, LANES), F32)] * 2,
        compiler_params=_cparams(("parallel", "parallel")),
        name="rope_tables",
    )(pos3, freq)


def _mla_prep_body(qlat_ref, ckv_ref, misc_ref, cos_ref, sin_ref, qnorm_ref, wqa_ref, wqb_ref, kvnorm_ref,
                   wuk_ref, wuv_ref, wkra_ref, wkrb_ref, q_ref, k_ref, v_ref):
    scale = (MLA_NOPE + MLA_ROPE) ** -0.5 * math.log2(math.e)
    cos_t = cos_ref[0]
    sin_t = sin_ref[0]

    ql = qlat_ref[0]
    qn = ql * lax.rsqrt(jnp.mean(ql * ql, axis=-1, keepdims=True) + NORM_EPS) * qnorm_ref[...]
    qnb = _bf(qn)
    qa = _dot(qnb, wqa_ref[...])
    qb = _dot(qnb, wqb_ref[...])

    ck = ckv_ref[0]
    kvn = ck * lax.rsqrt(jnp.mean(ck * ck, axis=-1, keepdims=True) + NORM_EPS) * kvnorm_ref[...]
    kvb = _bf(kvn)
    kn = _dot(kvb, wuk_ref[...])
    vt = _dot_nt(wuv_ref[...], kvb)
    misc = misc_ref[0]
    kr = _dot_sel_r(misc, wkra_ref[...]) * cos_t + _dot_sel_r(misc, wkrb_ref[...]) * sin_t

    for h in range(MLA_HEADS):
        sl = slice(h * LANES, (h + 1) * LANES)
        q_ref[0, h] = ((qa[:, sl] * cos_t + qb[:, sl] * sin_t) * scale).astype(BF16)
        k_ref[0, h] = (kn[:, sl] + kr).astype(BF16)
    t = vt.shape[1]
    ones = jnp.ones((MLA_VT_ROWS - MLA_V, t), F32)
    for h in range(MLA_HEADS):
        v_ref[0, h, 0] = jnp.concatenate([vt[h * MLA_V:(h + 1) * MLA_V], ones], axis=0).astype(BF16)


def _mla_prep(p32, cos_tab, sin_tab, qnorm, wqa, wqb, kvnorm, wuk, wuv, wkra, wkrb, tt):
    b, s, _ = p32.shape
    full = lambda shape: pl.BlockSpec(shape, lambda bi, i: (0,) * len(shape))
    return pl.pallas_call(
        _mla_prep_body,
        grid=(b, s // tt),
        in_specs=[pl.BlockSpec((1, tt, 256), lambda bi, i: (bi, i, P32_QLAT)),
                  pl.BlockSpec((1, tt, 128), lambda bi, i: (bi, i, P32_CKV)),
                  pl.BlockSpec((1, tt, 128), lambda bi, i: (bi, i, P32_MISC)),
                  pl.BlockSpec((1, tt, LANES), lambda bi, i: (bi, i, 0)),
                  pl.BlockSpec((1, tt, LANES), lambda bi, i: (bi, i, 0)),
                  full((1, 256)), full((256, 1024)), full((256, 1024)), full((1, 128)),
                  full((128, 1024)), full((512, 128)), full((128, 128)), full((128, 128))],
        out_specs=[pl.BlockSpec((1, MLA_HEADS, tt, 128), lambda bi, i: (bi, 0, i, 0)),
                   pl.BlockSpec((1, MLA_HEADS, tt, 128), lambda bi, i: (bi, 0, i, 0)),
                   pl.BlockSpec((1, MLA_HEADS, 1, MLA_VT_ROWS, tt), lambda bi, i: (bi, 0, i, 0, 0))],
        out_shape=[jax.ShapeDtypeStruct((b, MLA_HEADS, s, 128), BF16),
                   jax.ShapeDtypeStruct((b, MLA_HEADS, s, 128), BF16),
                   jax.ShapeDtypeStruct((b, MLA_HEADS, s // tt, MLA_VT_ROWS, tt), BF16)],
        compiler_params=_cparams(("parallel", "parallel")),
        name="mla_prep",
    )(p32, p32, p32, cos_tab, sin_tab, qnorm, wqa, wqb, kvnorm, wuk, wuv, wkra, wkrb)


def _mla_attn_body(q_ref, k_ref, v_ref, o_ref, m_ref, acc_ref, s_ref, *, tq):
    nh = MLA_GROUP
    i = pl.program_id(2)
    for hh in range(nh):
        m_ref[hh] = jnp.full((1, tq), -jnp.inf, F32)
        acc_ref[hh] = jnp.zeros((MLA_VT_ROWS, tq), F32)

    def scores(hh, j):
        start = pl.multiple_of(j * tq, tq)
        return _dot_nt(k_ref[0, hh, pl.ds(start, tq), :], q_ref[0, hh])

    def consume(hh, j, masked):
        st = s_ref[hh]
        if masked:
            st = jnp.where(_iota2((tq, tq), 0) <= _iota2((tq, tq), 1), st, -jnp.inf)
        m_old = m_ref[hh]
        m_new = jnp.maximum(m_old, jnp.max(st, axis=0, keepdims=True))
        p = jnp.exp2(st - m_new)
        acc_ref[hh] = acc_ref[hh] * jnp.exp2(m_old - m_new) + _dot(v_ref[0, hh, j], _bf(p))
        m_ref[hh] = m_new

    s_ref[0] = scores(0, 0)

    def loop_body(j, carry):
        for hh in range(nh):
            if hh + 1 < nh:
                s_ref[hh + 1] = scores(hh + 1, j)
            else:
                s_ref[0] = scores(0, j + 1)
            consume(hh, j, False)
        return carry

    lax.fori_loop(0, i, loop_body, 0)

    for hh in range(nh):
        if hh + 1 < nh:
            s_ref[hh + 1] = scores(hh + 1, i)
        consume(hh, i, True)
    outs = []
    for hh in range(nh):
        a = acc_ref[hh]
        outs.append(a[0:MLA_V] / a[MLA_V:MLA_V + 1])
    o_ref[0] = jnp.concatenate(outs, axis=0).T.astype(o_ref.dtype)


def _mla_attn(q, k, v, tq):
    b, h, s, _ = q.shape
    nh = MLA_GROUP
    once = pl.Buffered(1)
    return pl.pallas_call(
        functools.partial(_mla_attn_body, tq=tq),
        grid=(b, h // nh, s // tq),
        in_specs=[pl.BlockSpec((1, nh, tq, 128), lambda bi, p, i: (bi, p, i, 0)),
                  pl.BlockSpec((1, nh, s, 128), lambda bi, p, i: (bi, p, 0, 0), pipeline_mode=once),
                  pl.BlockSpec((1, nh, s // tq, MLA_VT_ROWS, tq), lambda bi, p, i: (bi, p, 0, 0, 0),
                               pipeline_mode=once)],
        out_specs=pl.BlockSpec((1, tq, nh * MLA_V), lambda bi, p, i: (bi, i, p)),
        out_shape=jax.ShapeDtypeStruct((b, s, h * MLA_V), BF16),
        scratch_shapes=[pltpu.VMEM((nh, 1, tq), F32), pltpu.VMEM((nh, MLA_VT_ROWS, tq), F32),
                        pltpu.VMEM((nh, tq, tq), F32)],
        compiler_params=_cparams(("parallel", "parallel", "arbitrary")),
        name="mla_attn",
    )(q, k, v)


def _gla_body(qk_ref, v_ref, og_ref, misc_ref, wgu_ref, bgate_ref, norm_ref, o_ref, st_ref, *, tt):
    c = CHUNK

    @pl.when(pl.program_id(1) == 0)
    def _():
        st_ref[...] = jnp.zeros_like(st_ref)

    causal, _ = _tri_masks(c)
    tri = jnp.where(causal, 1.0, 0.0).astype(BF16)
    lane = _iota2((1, LANES), 1)
    qk = qk_ref[0].astype(F32)
    q_all = qk[:, :256] * (GLA_DK ** -0.5)
    k_all = qk[:, 256:]
    ga = _dot3(misc_ref[0], wgu_ref[...]) + bgate_ref[...]
    log_a = (jnp.minimum(ga, 0.0) - jnp.log(1.0 + jnp.exp(-jnp.abs(ga)))) * (1.0 / GLA_TAU)

    units = []
    for ci in range(tt // c):
        rows = slice(ci * c, (ci + 1) * c)
        bcum = _dot_sel(tri, log_a[rows])
        b_last = bcum[c - 1:c]
        q_dec = q_all[rows] * jnp.exp(bcum)
        k_inv = k_all[rows] * jnp.exp(-bcum)
        k_end = k_all[rows] * jnp.exp(b_last - bcum)
        e_last = jnp.exp(b_last)
        for p in range(2):
            ls = slice(p * LANES, (p + 1) * LANES)
            qd_p, ki_p, ke_p, el_p = q_dec[:, ls], _bf(k_inv[:, ls]), _bf(k_end[:, ls]), e_last[:, ls]
            for hh in range(2):
                h = 2 * p + hh
                own = (lane >= hh * GLA_DK) & (lane < (hh + 1) * GLA_DK)
                qd_h = _bf(jnp.where(own, qd_p, 0.0))
                v_h = v_ref[0, rows, h * GLA_DV:(h + 1) * GLA_DV]
                units.append(dict(h=h, rows=rows, qd=qd_h, v=v_h, el=el_p,
                                  att=_bf(jnp.where(causal, _dot_nt(qd_h, ki_p), 0.0)),
                                  kv=jnp.where(own, _dot_tn(v_h, ke_p), 0.0)))
    o_intra = _bmm(jnp.stack([u['att'] for u in units], axis=0), jnp.stack([u['v'] for u in units], axis=0))

    for n, u in enumerate(units):
        h, rows = u['h'], u['rows']
        st = st_ref[h]
        o = o_intra[n] + _dot_nt(u['qd'], _bf(st))
        st_ref[h] = st * u['el'] + u['kv']
        o = o * lax.rsqrt(jnp.mean(o * o, axis=-1, keepdims=True) + NORM_EPS) * norm_ref[...]
        og = og_ref[0, rows, h * GLA_DV:(h + 1) * GLA_DV].astype(F32)
        o_ref[0, rows, h * GLA_DV:(h + 1) * GLA_DV] = (o * _silu(og)).astype(o_ref.dtype)


def _gla(p16, p32, wgu, bgate, norm, tt):
    b, s, _ = p16.shape
    full = lambda shape: pl.BlockSpec(shape, lambda bi, i: (0,) * len(shape))
    return pl.pallas_call(
        functools.partial(_gla_body, tt=tt),
        grid=(b, s // tt),
        in_specs=[pl.BlockSpec((1, tt, 512), lambda bi, i: (bi, i, P16_GLA_QK)),
                  pl.BlockSpec((1, tt, 512), lambda bi, i: (bi, i, P16_GLA_V)),
                  pl.BlockSpec((1, tt, 512), lambda bi, i: (bi, i, P16_GLA_OG)),
                  pl.BlockSpec((1, tt, 128), lambda bi, i: (bi, i, P32_MISC)),
                  full((128, 256)), full((1, 256)), full((1, 128))],
        out_specs=pl.BlockSpec((1, tt, 512), lambda bi, i: (bi, i, 0)),
        out_shape=jax.ShapeDtypeStruct((b, s, 512), BF16),
        scratch_shapes=[pltpu.VMEM((GLA_HEADS, GLA_DV, LANES), F32)],
        compiler_params=_cparams(("parallel", "arbitrary")),
        name="gla",
    )(p16, p16, p16, p32, wgu, bgate, norm)


def _gdn_body(q_ref, k_ref, v_ref, z_ref, misc_ref, conv_ref, alog_ref, dtb_ref, norm_ref, o_ref,
              ext_ref, st_ref, *, tt):
    c = CHUNK
    nc = tt // c

    @pl.when(pl.program_id(1) == 0)
    def _():
        st_ref[...] = jnp.zeros_like(st_ref)
        ext_ref[0:8, :] = jnp.zeros((8, 1536), F32)

    causal, strict = _tri_masks(c)
    tri = jnp.where(causal, 1.0, 0.0).astype(BF16)
    ones_cc = jnp.ones((c, c), BF16)
    eye = jnp.where(_iota2((c, c), 0) == _iota2((c, c), 1), 1.0, 0.0).astype(F32)

    ext_ref[8:8 + tt, 0:512] = q_ref[0].astype(F32)
    ext_ref[8:8 + tt, 512:1024] = k_ref[0].astype(F32)
    ext_ref[8:8 + tt, 1024:1536] = v_ref[0].astype(F32)
    conv = None
    for j in range(GDN_CONV):
        term = ext_ref[8 - (GDN_CONV - 1) + j:8 - (GDN_CONV - 1) + j + tt, :] * conv_ref[j:j + 1, :]
        conv = term if conv is None else conv + term
    ext_ref[0:8, :] = ext_ref[tt:tt + 8, :]
    qkv = _silu(conv)

    misc = misc_ref[0]
    beta_all = _sigmoid(misc)
    g_all = -jnp.exp(alog_ref[...]) * _softplus(misc + dtb_ref[...])

    qn, kn = [], []
    for h in range(GDN_HEADS):
        q = qkv[:, h * LANES:(h + 1) * LANES]
        k = qkv[:, 512 + h * LANES:512 + (h + 1) * LANES]
        qn.append(q * lax.rsqrt(jnp.sum(q * q, axis=-1, keepdims=True) + NORM_EPS) * (GDN_DK ** -0.5))
        kn.append(k * lax.rsqrt(jnp.sum(k * k, axis=-1, keepdims=True) + NORM_EPS))
    gc_chunks = [_dot_sel(tri, g_all[ci * c:(ci + 1) * c]) for ci in range(nc)]
    units = []
    for ci in range(nc):
        rows = slice(ci * c, (ci + 1) * c)
        for h in range(GDN_HEADS):
            q, k = qn[h][rows], kn[h][rows]
            v = qkv[rows, 1024 + h * LANES:1024 + (h + 1) * LANES]
            beta = beta_all[rows, MISC_GDN_BETA + h:MISC_GDN_BETA + h + 1]
            gc = gc_chunks[ci][:, MISC_GDN_A + h:MISC_GDN_A + h + 1]
            kb = k * beta
            eg = jnp.exp(gc)
            g_last = gc[c - 1:c]
            units.append(dict(gc=gc, q=_bf(q), k=_bf(k), kb=_bf(kb),
                              rhs=_bf(jnp.concatenate([v * beta, kb * eg], axis=1)), q_dec=q * eg,
                              k_end=_bf(k * jnp.exp(g_last - gc)), e_last=jnp.exp(g_last)))
    for u in units:
        u['gc_row'] = _dot_sel(ones_cc, eye * u['gc'])
    for u in units:
        u['kk'] = _dot_nt(u['kb'], u['k'])
    for u in units:
        u['qk'] = _dot_nt(u['q'], u['k'])
    l_list, pre = [], []
    for u in units:
        decay = jnp.where(causal, jnp.exp(jnp.minimum(u['gc'] - u['gc_row'], 0.0)), 0.0)
        l_list.append(jnp.where(strict, u['kk'] * decay, 0.0))
        pre.append(dict(rhs=u['rhs'], att=_bf(u['qk'] * decay), q_dec=u['q_dec'], k_end=u['k_end'],
                        e_last=u['e_last']))
    t_all = _unit_lower_inverse(jnp.stack(l_list, axis=0))

    stack = lambda key: jnp.stack([d[key] for d in pre], axis=0)
    uw = _bf(_bmm(_bf(t_all), stack('rhs')))
    ab = _bmm(stack('att'), uw)
    seq = []
    for n, d in enumerate(pre):
        kw = _dot_tn(d['k_end'], uw[n])
        seq.append(dict(h=_bf(d['q_dec'] - ab[n, :, LANES:]), z=ab[n, :, :LANES], m=_bf(kw[:, LANES:]),
                        n=kw[:, :LANES], e_last=d['e_last']))

    for ci in range(nc):
        rows = slice(ci * c, (ci + 1) * c)
        for h in range(GDN_HEADS):
            d = seq[ci * GDN_HEADS + h]
            st = st_ref[h]
            stb = _bf(st)
            o = _dot(d['h'], stb) + d['z']
            st_ref[h] = st * d['e_last'] - _dot(d['m'], stb) + d['n']
            o = o * lax.rsqrt(jnp.mean(o * o, axis=-1, keepdims=True) + NORM_EPS) * norm_ref[...]
            z = z_ref[0, rows, h * LANES:(h + 1) * LANES].astype(F32)
            o_ref[0, rows, h * LANES:(h + 1) * LANES] = (o * _silu(z)).astype(o_ref.dtype)


def _gdn(p16, p32, conv_w, alog, dtb, norm, tt):
    b, s, _ = p16.shape
    full = lambda shape: pl.BlockSpec(shape, lambda bi, i: (0,) * len(shape))
    return pl.pallas_call(
        functools.partial(_gdn_body, tt=tt),
        grid=(b, s // tt),
        in_specs=[pl.BlockSpec((1, tt, 512), lambda bi, i: (bi, i, P16_GDN_Q)),
                  pl.BlockSpec((1, tt, 512), lambda bi, i: (bi, i, P16_GDN_Q + 1)),
                  pl.BlockSpec((1, tt, 512), lambda bi, i: (bi, i, P16_GDN_Q + 2)),
                  pl.BlockSpec((1, tt, 512), lambda bi, i: (bi, i, P16_GDN_Z)),
                  pl.BlockSpec((1, tt, 128), lambda bi, i: (bi, i, P32_MISC)),
                  full((GDN_CONV, 1536)), full((1, 128)), full((1, 128)), full((1, 128))],
        out_specs=pl.BlockSpec((1, tt, 512), lambda bi, i: (bi, i, 0)),
        out_shape=jax.ShapeDtypeStruct((b, s, 512), BF16),
        scratch_shapes=[pltpu.VMEM((tt + 8, 1536), F32), pltpu.VMEM((GDN_HEADS, GDN_DK, GDN_DV), F32)],
        compiler_params=_cparams(("parallel", "arbitrary")),
        name="gdn",
    )(p16, p16, p16, p16, p32, conv_w, alog, dtb, norm)


def _rwkv_unpack(refs, has_vres):
    if has_vres:
        (r_ref, k_ref, v_ref, lat_ref, misc_ref, vfirst_ref, mu_ref, mulat_ref, w0_ref, wup_ref, a0_ref, aup_ref,
         gup_ref, kk_ref, ka_ref, rk_ref, lng_ref, lnb_ref, vmu_ref, v0_ref, vup_ref, hsum_ref,
         o_ref, ext_ref, extl_ref, extm_ref, y_ref, st_ref) = refs
        vout_ref = None
    else:
        (r_ref, k_ref, v_ref, lat_ref, mu_ref, mulat_ref, w0_ref, wup_ref, a0_ref, aup_ref,
         gup_ref, kk_ref, ka_ref, rk_ref, lng_ref, lnb_ref, hsum_ref,
         o_ref, vout_ref, ext_ref, extl_ref, y_ref, st_ref) = refs
        misc_ref = vfirst_ref = vmu_ref = v0_ref = vup_ref = extm_ref = None
    return (r_ref, k_ref, v_ref, lat_ref, misc_ref, vfirst_ref, mu_ref, mulat_ref, w0_ref, wup_ref, a0_ref, aup_ref,
            gup_ref, kk_ref, ka_ref, rk_ref, lng_ref, lnb_ref, vmu_ref, v0_ref, vup_ref, hsum_ref,
            o_ref, vout_ref, ext_ref, extl_ref, extm_ref, y_ref, st_ref)


def _rwkv_body(*refs, tt, has_vres):
    (r_ref, k_ref, v_ref, lat_ref, misc_ref, _, _, _, _, _, _, _, _, _, _, _, _, _, _, _, _, _,
     _, _, ext_ref, extl_ref, extm_ref, _, st_ref) = _rwkv_unpack(refs, has_vres)

    @pl.when(pl.program_id(1) == 0)
    def _():
        st_ref[...] = jnp.zeros_like(st_ref)
        ext_ref[0:8, :] = jnp.zeros((8, 1536), F32)
        extl_ref[0:8, :] = jnp.zeros((8, 256), F32)
        if has_vres:
            extm_ref[0:8, :] = jnp.zeros((8, 128), F32)

    ext_ref[8:8 + tt, 0:512] = r_ref[0].astype(F32)
    ext_ref[8:8 + tt, 512:1024] = k_ref[0].astype(F32)
    ext_ref[8:8 + tt, 1024:1536] = v_ref[0].astype(F32)
    extl_ref[8:8 + tt, :] = lat_ref[0]
    if has_vres:
        extm_ref[8:8 + tt, :] = misc_ref[0]
    th = tt // RWKV_SPLIT
    _interleave_ranges([_rwkv_rows(refs, part * th, (part + 1) * th, has_vres) for part in range(RWKV_SPLIT)],
                       RWKV_PREP_PER_STAGE)
    ext_ref[0:8, :] = ext_ref[tt:tt + 8, :]
    extl_ref[0:8, :] = extl_ref[tt:tt + 8, :]
    if has_vres:
        extm_ref[0:8, :] = extm_ref[tt:tt + 8, :]


def _rwkv_rows(refs, lo, hi, has_vres):
    (r_ref, k_ref, v_ref, lat_ref, misc_ref, vfirst_ref, mu_ref, mulat_ref, w0_ref, wup_ref, a0_ref, aup_ref,
     gup_ref, kk_ref, ka_ref, rk_ref, lng_ref, lnb_ref, vmu_ref, v0_ref, vup_ref, hsum_ref,
     o_ref, vout_ref, ext_ref, extl_ref, extm_ref, y_ref, st_ref) = _rwkv_unpack(refs, has_vres)
    c = CHUNK
    nc = (hi - lo) // c
    n_pairs = RWKV_HEADS // 2

    cur = ext_ref[8 + lo:8 + hi, :]
    rkv = cur + (ext_ref[7 + lo:7 + hi, :] - cur) * mu_ref[...]
    curl = extl_ref[8 + lo:8 + hi, :]
    lat = curl + (extl_ref[7 + lo:7 + hi, :] - curl) * mulat_ref[...]
    yield 'v'

    r = rkv[:, 0:512]
    k = rkv[:, 512:1024]
    v = rkv[:, 1024:1536]
    lat_wa = lat[:, 0:128]
    lat_g = lat[:, 128:256]
    w_pre = w0_ref[...] + _dot(_bf(jnp.tanh(lat_wa)), wup_ref[...])
    w = -_softplus(-w_pre) - 0.5
    logw = -jnp.exp(w)
    yield 'v'
    a = _sigmoid(a0_ref[...] + _dot(_bf(lat_wa), aup_ref[...]))
    g = _dot(_bf(_sigmoid(lat_g)), gup_ref[...])
    yield 'v'
    if has_vres:
        curm = extm_ref[8 + lo:8 + hi, :]
        vlat = curm + (extm_ref[7 + lo:7 + hi, :] - curm) * vmu_ref[...]
        v_gate = _sigmoid(v0_ref[...] + _dot(_bf(vlat), vup_ref[...]))
        v = v + (vfirst_ref[0, lo:hi, :] - v) * v_gate
    else:
        vout_ref[0, lo:hi, :] = v

    yield 'v'
    hsum = hsum_ref[...]
    kkv = k * kk_ref[...]
    kk = kkv * lax.rsqrt(_dot_sel_r(kkv * kkv, hsum) + NORM_EPS)
    yield 'v'
    k = k * (1.0 + (a - 1.0) * ka_ref[...])
    bonus = _dot(_bf(r * k * rk_ref[...]), hsum) * v
    bvec = kk * a
    yield 'v'

    lane = _iota2((1, LANES), 1)
    first = lane < RWKV_N
    rowh = _iota2((LANES, LANES), 0) < RWKV_N
    colh = _iota2((LANES, LANES), 1) < RWKV_N
    blockdiag = rowh == colh
    causal, strict = _tri_masks(c)
    tri = jnp.where(causal, 1.0, 0.0).astype(BF16)

    l_list, pre = [], []
    for ci in range(nc):
        rows = slice(ci * c, (ci + 1) * c)
        lw = logw[rows]
        cum = _dot_sel(tri, lw)
        gam = jnp.exp(cum)
        inv = jnp.exp(-cum)
        gam_ex = jnp.exp(cum - lw)
        gam_c = gam[c - 1:c]
        a_t = -kk[rows] * gam_ex
        r_t = r[rows] * gam
        b_t = bvec[rows] * inv
        k_t = k[rows] * inv
        yield 'v'
        for p in range(n_pairs):
            ls = slice(p * LANES, (p + 1) * LANES)
            a_p, r_p, b_p, k_p = a_t[:, ls], r_t[:, ls], b_t[:, ls], k_t[:, ls]
            a_h = [jnp.where(first, a_p, 0.0), jnp.where(first, 0.0, a_p)]
            r_h = [jnp.where(first, r_p, 0.0), jnp.where(first, 0.0, r_p)]
            lhs = _bf(jnp.concatenate(a_h + r_h, axis=0))
            m_b = _dot_nt(lhs, _bf(b_p))
            m_k = _dot_nt(lhs, _bf(k_p))
            heads = []
            for hh in range(2):
                l_list.append(jnp.where(strict, -m_b[hh * c:(hh + 1) * c], 0.0))
                heads.append(dict(a_h=a_h[hh], r_h=r_h[hh],
                                  a_ak=_bf(jnp.where(strict, m_k[hh * c:(hh + 1) * c], 0.0)),
                                  a_rb=_bf(jnp.where(causal, m_b[(2 + hh) * c:(3 + hh) * c], 0.0)),
                                  a_rk=_bf(jnp.where(causal, m_k[(2 + hh) * c:(3 + hh) * c], 0.0))))
            pre.append(dict(heads=heads, gam_c=gam_c[:, ls],
                            b_end=_bf(b_p * gam_c[:, ls]), k_end=_bf(k_p * gam_c[:, ls]),
                            v_p=_bf(v[rows, ls])))
            yield 'v'
    yield 'boundary'
    t_all = None
    for t_all in _unit_lower_inverse_levels(jnp.stack(l_list, axis=0)):
        yield 'm'

    heads_all = [hd for d in pre for hd in d['heads']]
    stack = lambda key: jnp.stack([hd[key] for hd in heads_all], axis=0)
    v_all = jnp.stack([d['v_p'] for d in pre for _ in range(2)], axis=0)
    akv = _bmm(stack('a_ak'), v_all)
    yield 'm'
    gq = _bf(_bmm(_bf(t_all), _bf(jnp.concatenate([stack('a_h'), akv], axis=2))))
    yield 'm'
    rb = _bmm(stack('a_rb'), gq)
    h_all = stack('r_h') + rb[:, :, :LANES]
    yield 'm'
    z_all = rb[:, :, LANES:] + _bmm(stack('a_rk'), v_all)
    yield 'm'
    seq = []
    for n, d in enumerate(pre):
        if n % 4 == 3:
            yield 'm'
        g_p = gq[2 * n, :, :LANES] + gq[2 * n + 1, :, :LANES]
        q1_p = jnp.where(first, gq[2 * n, :, LANES:], gq[2 * n + 1, :, LANES:])
        m_p = jnp.where(blockdiag, _dot_tn(g_p, d['b_end']), 0.0)
        n_p = jnp.where(blockdiag, _dot_tn(q1_p, d['b_end']) + _dot_tn(d['v_p'], d['k_end']), 0.0)
        seq.append(dict(h=_bf(h_all[2 * n] + h_all[2 * n + 1]), z=jnp.where(first, z_all[2 * n], z_all[2 * n + 1]),
                        m=_bf(m_p), n=n_p, gam_c=d['gam_c']))

    yield 'seq'
    for ci in range(nc):
        for p in range(n_pairs):
            d = seq[ci * n_pairs + p]
            st = st_ref[p]
            stb = _bf(st)
            y = _dot_nt(d['h'], stb) + d['z']
            st_ref[p] = st * d['gam_c'] + _dot(stb, d['m']) + d['n']
            y_ref[lo + ci * c:lo + (ci + 1) * c, p * LANES:(p + 1) * LANES] = y

    y = y_ref[lo:hi, :]
    mean = _dot_sel_r(y, hsum) * (1.0 / RWKV_N)
    yc = y - mean
    var = _dot(_bf(yc * yc), hsum) * (1.0 / RWKV_N)
    yn = yc * lax.rsqrt(var + RWKV_LN_EPS) * lng_ref[...] + lnb_ref[...]
    o_ref[0, lo:hi, :] = ((yn + bonus) * g).astype(o_ref.dtype)


def _rwkv(p16, p32, v_first, weights, tt, has_vres):
    b, s, _ = p16.shape
    full = lambda shape: pl.BlockSpec(shape, lambda bi, i: (0,) * len(shape))
    tok = lambda w, blk: pl.BlockSpec((1, tt, w), lambda bi, i: (bi, i, blk))
    in_specs = [tok(512, P16_RWKV_R), tok(512, P16_RWKV_R + 1), tok(512, P16_RWKV_R + 2), tok(256, P32_RWKV_LAT)]
    args = [p16, p16, p16, p32]
    if has_vres:
        in_specs += [tok(128, P32_MISC), tok(512, 0)]
        args += [p32, v_first]
    in_specs += [full(w.shape) for w in weights]
    args += list(weights)
    out_specs = [pl.BlockSpec((1, tt, 512), lambda bi, i: (bi, i, 0))]
    out_shape = [jax.ShapeDtypeStruct((b, s, 512), BF16)]
    scratch = [pltpu.VMEM((tt + 8, 1536), F32), pltpu.VMEM((tt + 8, 256), F32)]
    if has_vres:
        scratch.append(pltpu.VMEM((tt + 8, 128), F32))
    else:
        out_specs.append(pl.BlockSpec((1, tt, 512), lambda bi, i: (bi, i, 0)))
        out_shape.append(jax.ShapeDtypeStruct((b, s, 512), F32))
    scratch.append(pltpu.VMEM((tt, RWKV_WIDTH), F32))
    scratch.append(pltpu.VMEM((RWKV_HEADS // 2, LANES, LANES), F32))
    return pl.pallas_call(
        functools.partial(_rwkv_body, tt=tt, has_vres=has_vres),
        grid=(b, s // tt),
        in_specs=in_specs,
        out_specs=out_specs,
        out_shape=out_shape,
        scratch_shapes=scratch,
        compiler_params=_cparams(("parallel", "arbitrary")),
        name="rwkv7",
    )(*args)


def _layernorm(h, g, b):
    mu = jnp.mean(h, axis=-1, keepdims=True)
    hc = h - mu
    var = jnp.mean(hc * hc, axis=-1, keepdims=True)
    return hc * lax.rsqrt(var + LN_EPS) * g + b


def _merge_body(ya_ref, yb_ref, yc_ref, yd_ref, gate_ref, x_ref, wb_ref, wo_ref, g_ref, b_ref, o_ref):
    merged = None
    for n, y_ref in enumerate((ya_ref, yb_ref, yc_ref, yd_ref)):
        gate = _sigmoid(gate_ref[0, :, n * D_MODEL:(n + 1) * D_MODEL].astype(F32))
        term = gate * _dot(y_ref[0], wb_ref[n])
        merged = term if merged is None else merged + term
    mix = _dot(_bf(merged), wo_ref[...])
    o_ref[0] = _layernorm(DEEPNORM_ALPHA * x_ref[0] + mix, g_ref[...], b_ref[...])


def _merge(ys, p16, x, wb, wo, g, bb, tt):
    b, s, d = x.shape
    full = lambda shape: pl.BlockSpec(shape, lambda bi, i: (0,) * len(shape))
    ytok = pl.BlockSpec((1, tt, 512), lambda bi, i: (bi, i, 0))
    return pl.pallas_call(
        _merge_body,
        grid=(b, s // tt),
        in_specs=[ytok, ytok, ytok, ytok,
                  pl.BlockSpec((1, tt, N_BRANCH * D_MODEL), lambda bi, i: (bi, i, P16_GATES)),
                  pl.BlockSpec((1, tt, d), lambda bi, i: (bi, i, 0)),
                  full((N_BRANCH, BRANCH_WIDTH, D_MODEL)), full((D_MODEL, D_MODEL)), full((1, d)), full((1, d))],
        out_specs=pl.BlockSpec((1, tt, d), lambda bi, i: (bi, i, 0)),
        out_shape=jax.ShapeDtypeStruct((b, s, d), F32),
        compiler_params=_cparams(("parallel", "parallel")),
        name="merge_ln1",
    )(*ys, p16, x, wb, wo, g, bb)


def _moe_body(x_ref, rwt_ref, rbias_ref, tri_ref, wg_ref, wu_ref, wd_ref, g_ref, b_ref, o_ref,
              xb_ref, comb_ref, grp_ref, acc_ref):
    gidx = pl.program_id(2)
    tt = x_ref.shape[1]

    @pl.when(gidx == 0)
    def _():
        x = x_ref[0]
        xb_ref[...] = x.astype(BF16)
        acc_ref[...] = jnp.zeros_like(acc_ref)
        scores = _sigmoid(_dot3_nt(rwt_ref[...], x))
        biased = scores + rbias_ref[...]
        row = _iota2((N_EXPERTS, tt), 0).astype(F32)
        row_group = (_iota2((N_EXPERTS, tt), 0) >> 2).astype(F32)
        neg = -jnp.inf
        gs = []
        for gi in range(N_GROUPS):
            rows = [biased[gi * GROUP_SIZE + j:gi * GROUP_SIZE + j + 1] for j in range(GROUP_SIZE)]
            best = None
            for i0 in range(GROUP_SIZE):
                for i1 in range(i0 + 1, GROUP_SIZE):
                    pair = rows[i0] + rows[i1]
                    best = pair if best is None else jnp.maximum(best, pair)
            gs.append(best)
        gmax = functools.reduce(jnp.maximum, gs)
        best_group = jnp.full((1, tt), float(N_GROUPS), F32)
        for gi in reversed(range(N_GROUPS)):
            best_group = jnp.where(gs[gi] == gmax, float(gi), best_group)
        masked = jnp.where(row_group == best_group, biased, neg)
        m1 = jnp.max(masked, axis=0, keepdims=True)
        i1 = jnp.min(jnp.where(masked == m1, row, float(N_EXPERTS)), axis=0, keepdims=True)
        masked2 = jnp.where(row == i1, neg, masked)
        m2 = jnp.max(masked2, axis=0, keepdims=True)
        i2 = jnp.min(jnp.where(masked2 == m2, row, float(N_EXPERTS)), axis=0, keepdims=True)
        s1 = jnp.sum(jnp.where(row == i1, scores, 0.0), axis=0, keepdims=True)
        s2 = jnp.sum(jnp.where(row == i2, scores, 0.0), axis=0, keepdims=True)
        tot = s1 + s2
        comb_t = jnp.where(row == i1, s1 / tot, 0.0) + jnp.where(row == i2, s2 / tot, 0.0)
        for gi in range(N_GROUPS):
            comb_ref[gi, 0:GROUP_SIZE, :] = comb_t[gi * GROUP_SIZE:(gi + 1) * GROUP_SIZE]
            comb_ref[gi, GROUP_SIZE:8, :] = jnp.zeros((8 - GROUP_SIZE, tt), F32)
        grp_ref[...] = jnp.broadcast_to(best_group, (8, tt))

    member = grp_ref[...] == gidx.astype(F32)
    cnt = _dot(jnp.where(member, 1.0, 0.0).astype(BF16), tri_ref[...])
    pos = jnp.where(member, cnt - 1.0, -1.0)
    n_tok = cnt[0:1, tt - 1:tt][0, 0].astype(jnp.int32)
    eye8 = jnp.where(_iota2((8, 8), 0) == _iota2((8, 8), 1), 1.0, 0.0).astype(BF16)
    p_hi, p_lo = _split2(pos)
    pos_col = (_dot_tn(p_hi, eye8) + _dot_tn(p_lo, eye8))[:, 0:1]
    pos_row = pos[0:1]
    c_hi, c_lo = _split2(comb_ref[gidx])
    xb = xb_ref[...]
    r = MOE_ROWS

    def sub_block(k, carry):
        base = (k * r).astype(F32)
        sel = jnp.where(pos_row == base + _iota2((r, 1), 0).astype(F32), 1.0, 0.0).astype(BF16)
        sel_t = jnp.where(pos_col == base + _iota2((1, r), 1).astype(F32), 1.0, 0.0).astype(BF16)
        xc = _bf(_dot(sel, xb))
        cw = _dot_nt(sel, c_hi) + _dot_nt(sel, c_lo)
        y = None
        for j in range(GROUP_SIZE):
            hid = _silu(_dot(xc, wg_ref[0, j])) * _dot(xc, wu_ref[0, j])
            term = cw[:, j:j + 1] * _dot(_bf(hid), wd_ref[0, j])
            y = term if y is None else y + term
        acc_ref[...] += _dot(sel_t, _bf(y))
        return carry

    lax.fori_loop(0, (n_tok + (r - 1)) // r, sub_block, 0)

    @pl.when(gidx == N_GROUPS - 1)
    def _():
        o_ref[0] = _layernorm(DEEPNORM_ALPHA * x_ref[0] + acc_ref[...], g_ref[...], b_ref[...])


def _moe(x, rwt, rbias, wg, wu, wd, g, bb, tt):
    b, s, d = x.shape
    full = lambda shape: pl.BlockSpec(shape, lambda bi, i, e: (0,) * len(shape))
    once = lambda shape: pl.BlockSpec(shape, lambda bi, i, e: (0,) * len(shape), pipeline_mode=pl.Buffered(1))
    grouped = lambda w: w.reshape((N_GROUPS, GROUP_SIZE) + w.shape[1:])
    idx = np.arange(tt)
    tri = jnp.asarray((idx[:, None] <= idx[None, :]).astype(np.float32), BF16)
    return pl.pallas_call(
        _moe_body,
        grid=(b, s // tt, N_GROUPS),
        in_specs=[pl.BlockSpec((1, tt, d), lambda bi, i, e: (bi, i, 0), pipeline_mode=pl.Buffered(1)),
                  full((N_EXPERTS, d)), full((N_EXPERTS, 1)), once((tt, tt)),
                  pl.BlockSpec((1, GROUP_SIZE, d, D_EXPERT), lambda bi, i, e: (e, 0, 0, 0)),
                  pl.BlockSpec((1, GROUP_SIZE, d, D_EXPERT), lambda bi, i, e: (e, 0, 0, 0)),
                  pl.BlockSpec((1, GROUP_SIZE, D_EXPERT, d), lambda bi, i, e: (e, 0, 0, 0)),
                  full((1, d)), full((1, d))],
        out_specs=pl.BlockSpec((1, tt, d), lambda bi, i, e: (bi, i, 0)),
        out_shape=jax.ShapeDtypeStruct((b, s, d), F32),
        scratch_shapes=[pltpu.VMEM((tt, d), BF16), pltpu.VMEM((N_GROUPS, 8, tt), F32), pltpu.VMEM((8, tt), F32),
                        pltpu.VMEM((tt, d), F32)],
        compiler_params=_cparams(("parallel", "parallel", "arbitrary")),
        name="moe_ln2",
    )(x, rwt, rbias, tri, grouped(wg), grouped(wu), grouped(wd), g, bb)


def _cols(w, name):
    o, n = _OFF[name]
    return w[:, o:o + n]


def _pack_inproj(w_in_l, w_vres_l):
    d = w_in_l.shape[0]
    w16 = jnp.concatenate([_cols(w_in_l, "gates"), _cols(w_in_l, "gla_q"), _cols(w_in_l, "gla_k"),
                           _cols(w_in_l, "gla_v"), _cols(w_in_l, "gla_og"), _cols(w_in_l, "gdn_qkv"),
                           _cols(w_in_l, "gdn_z"), _cols(w_in_l, "rwkv_rkv")], axis=1).astype(BF16)
    vres = jnp.zeros((d, RWKV_V_RANK), F32) if w_vres_l is None else w_vres_l
    misc = jnp.concatenate([_cols(w_in_l, "k_rope"), _cols(w_in_l, "gla_gl"), _cols(w_in_l, "gdn_beta"),
                            _cols(w_in_l, "gdn_a"), vres], axis=1)
    misc = jnp.pad(misc, ((0, 0), (0, LANES - misc.shape[1])))
    w32 = jnp.concatenate([_cols(w_in_l, "q_lat"), _cols(w_in_l, "rwkv_lat"), _cols(w_in_l, "c_kv"), misc],
                          axis=1).astype(BF16)
    return w16, w32


def _row_pad(w, start, total=LANES):
    return jnp.pad(w, ((start, total - start - w.shape[0]), (0, 0)))


def _lane_row(vec, start, total=LANES):
    return jnp.pad(vec, (start, total - start - vec.shape[0])).reshape(1, total)


def _pack_mla(w_uq, w_ukv):
    half = MLA_ROPE // 2
    wq = w_uq.reshape(MLA_Q_RANK, MLA_HEADS, MLA_NOPE + MLA_ROPE)
    nope, r1, r2 = wq[..., :MLA_NOPE], wq[..., MLA_NOPE:MLA_NOPE + half], wq[..., MLA_NOPE + half:]
    pad = jnp.zeros((MLA_Q_RANK, MLA_HEADS, LANES - MLA_NOPE - MLA_ROPE), F32)
    wqa = jnp.concatenate([nope, r1, r2, pad], axis=-1).reshape(MLA_Q_RANK, MLA_HEADS * LANES)
    wqb = jnp.concatenate([jnp.zeros_like(nope), -r2, r1, pad], axis=-1).reshape(MLA_Q_RANK, MLA_HEADS * LANES)
    wkv = w_ukv.reshape(MLA_KV_RANK, MLA_HEADS, MLA_NOPE + MLA_V)
    wuk = jnp.pad(wkv[..., :MLA_NOPE], ((0, 0), (0, 0), (0, LANES - MLA_NOPE))).reshape(MLA_KV_RANK, MLA_HEADS * LANES)
    wuv = wkv[..., MLA_NOPE:].reshape(MLA_KV_RANK, MLA_HEADS * MLA_V).T
    ra = np.zeros((LANES, LANES), np.float32)
    rb = np.zeros((LANES, LANES), np.float32)
    for j in range(half):
        ra[MISC_KROPE + j, MLA_NOPE + j] = 1.0
        ra[MISC_KROPE + half + j, MLA_NOPE + half + j] = 1.0
        rb[MISC_KROPE + half + j, MLA_NOPE + j] = -1.0
        rb[MISC_KROPE + j, MLA_NOPE + half + j] = 1.0
    return (wqa.astype(BF16), wqb.astype(BF16), wuk.astype(BF16), wuv.astype(BF16),
            jnp.asarray(ra, BF16), jnp.asarray(rb, BF16))


def _rope_freq_row():
    half = MLA_ROPE // 2
    inv_freq = ROPE_THETA ** (-jnp.arange(half, dtype=F32) / half)
    return jnp.concatenate([jnp.zeros((MLA_NOPE,), F32), inv_freq, inv_freq,
                            jnp.zeros((LANES - MLA_NOPE - MLA_ROPE,), F32)]).reshape(1, LANES)


def _head_sum_matrix():
    idx = np.arange(RWKV_WIDTH) // RWKV_N
    return jnp.asarray((idx[:, None] == idx[None, :]).astype(np.float32), BF16)


def _tile(s, pref):
    t = min(pref, s)
    assert s % t == 0 and t % CHUNK == 0
    return t


def kernel(x, positions, router_w, w_in, w_in_vres, mla_q_norm, mla_w_uq, mla_kv_norm, mla_w_ukv,
           gla_w_gate_up, gla_b_gate, gla_norm, gdn_conv, gdn_a_log, gdn_dt_bias, gdn_norm,
           rwkv_mu, rwkv_w0, rwkv_w_up, rwkv_a0, rwkv_a_up, rwkv_g_up, rwkv_k_k, rwkv_k_a, rwkv_r_k,
           rwkv_ln_g, rwkv_ln_b, rwkv_vres_mu, rwkv_v0, rwkv_v_up, w_branch, w_out,
           ln1_g, ln1_b, ln2_g, ln2_b, router_bias, moe_w_gate, moe_w_up, moe_w_down):
    b, s, d = x.shape
    assert d == D_MODEL and w_in.shape[2] == IN_COLS
    pos3 = positions.reshape(b, s, 1)
    rwt = router_w.T
    hsum = _head_sum_matrix()
    row = lambda v: v.reshape(1, -1)
    t_proj = _tile(s, 1024)
    t_scan = _tile(s, 256)
    t_rwkv = _tile(s, 256 * RWKV_SPLIT)
    t_attn = _tile(s, 512)
    t_merge = _tile(s, 512)
    t_moe = _tile(s, 1024)

    cos_tab, sin_tab = _rope_tables(pos3, _rope_freq_row(), t_attn)
    v_first = None
    for l in range(DEPTH):
        w16, w32 = _pack_inproj(w_in[l], None if l == 0 else w_in_vres[l - 1])
        p16 = _inproj(x, w16, BF16, t_proj, 1024)
        p32 = _inproj(x, w32, F32, t_proj, P32_WIDTH)

        wqa, wqb, wuk, wuv, wkra, wkrb = _pack_mla(mla_w_uq[l], mla_w_ukv[l])
        q, k, v = _mla_prep(p32, cos_tab, sin_tab, row(mla_q_norm[l]), wqa, wqb, row(mla_kv_norm[l]), wuk, wuv,
                            wkra, wkrb, t_attn)
        y_a = _mla_attn(q, k, v, t_attn)

        wgu = _row_pad(gla_w_gate_up[l], MISC_GLA_GL)
        y_b = _gla(p16, p32, wgu, row(gla_b_gate[l]), row(gla_norm[l]), t_scan)

        y_c = _gdn(p16, p32, gdn_conv[l], _lane_row(gdn_a_log[l], MISC_GDN_A), _lane_row(gdn_dt_bias[l], MISC_GDN_A),
                   row(gdn_norm[l]), t_scan)

        mu = rwkv_mu[l]
        weights = [row(mu[:1536]), row(mu[1536:]), row(rwkv_w0[l]),
                   _row_pad(rwkv_w_up[l], 0).astype(BF16), row(rwkv_a0[l]),
                   _row_pad(rwkv_a_up[l], RWKV_W_RANK).astype(BF16), rwkv_g_up[l].astype(BF16),
                   row(rwkv_k_k[l]), row(rwkv_k_a[l]), row(rwkv_r_k[l]), row(rwkv_ln_g[l]), row(rwkv_ln_b[l])]
        if l == 0:
            y_d, v_first = _rwkv(p16, p32, None, weights + [hsum], t_rwkv, False)
        else:
            weights += [_lane_row(rwkv_vres_mu[l - 1], MISC_VRES), row(rwkv_v0[l - 1]),
                        _row_pad(rwkv_v_up[l - 1], MISC_VRES).astype(BF16)]
            (y_d,) = _rwkv(p16, p32, v_first, weights + [hsum], t_rwkv, True)

        x = _merge((y_a, y_b, y_c, y_d), p16, x, w_branch[l].astype(BF16), w_out[l].astype(BF16),
                   row(ln1_g[l]), row(ln1_b[l]), t_merge)
        x = _moe(x, rwt, router_bias[l].reshape(N_EXPERTS, 1), moe_w_gate[l].astype(BF16),
                 moe_w_up[l].astype(BF16), moe_w_down[l].astype(BF16), row(ln2_g[l]), row(ln2_b[l]), t_moe)
    return x
```

```python
import functools
import math

import jax
import jax.numpy as jnp
import numpy as np
from jax import lax
from jax.experimental import pallas as pl
from jax.experimental.pallas import tpu as pltpu

F32 = jnp.float32
BF16 = jnp.bfloat16

LANES = 128
VMEM_LIMIT = 56 * 1024 * 1024

D_MODEL = 1024
DEPTH = 2
MLA_HEADS, MLA_NOPE, MLA_ROPE, MLA_V = 8, 64, 32, 64
MLA_Q_RANK, MLA_KV_RANK = 256, 128
MLA_GROUP = 4
MLA_VT_ROWS = MLA_V + 16
ROPE_THETA = 10000.0
GLA_HEADS, GLA_DK, GLA_DV, GLA_GATE_RANK, GLA_TAU = 4, 64, 128, 16, 16.0
GDN_HEADS, GDN_DK, GDN_DV, GDN_CONV = 4, 128, 128, 4
RWKV_HEADS, RWKV_N = 8, 64
RWKV_W_RANK, RWKV_A_RANK, RWKV_V_RANK, RWKV_G_RANK = 64, 64, 32, 128
RWKV_LN_EPS = 64e-5
CHUNK = 64
N_BRANCH, BRANCH_WIDTH = 4, 512
N_EXPERTS, N_GROUPS, TOP_K, D_EXPERT = 16, 4, 2, 512
GROUP_SIZE = N_EXPERTS // N_GROUPS
MOE_ROWS = 320
DEEPNORM_ALPHA = (2.0 * DEPTH) ** 0.25
LN_EPS = 1e-5
NORM_EPS = 1e-6
RWKV_WIDTH = RWKV_HEADS * RWKV_N

_OFF = {}
_o = 0
for _name, _w in (("q_lat", MLA_Q_RANK), ("c_kv", MLA_KV_RANK), ("k_rope", MLA_ROPE),
                  ("gla_q", 256), ("gla_k", 256), ("gla_v", 512), ("gla_gl", GLA_GATE_RANK), ("gla_og", 512),
                  ("gdn_qkv", 1536), ("gdn_beta", GDN_HEADS), ("gdn_a", GDN_HEADS), ("gdn_z", 512),
                  ("rwkv_rkv", 1536), ("rwkv_lat", 256), ("gates", N_BRANCH * D_MODEL)):
    _OFF[_name] = (_o, _w)
    _o += _w
IN_COLS = _o

MISC_KROPE = 0
MISC_GLA_GL = 32
MISC_GDN_BETA = 48
MISC_GDN_A = 52
MISC_VRES = 56

P16_WIDTH = 9216
P16_GATES = 0
P16_GLA_QK = 8
P16_GLA_V = 9
P16_GLA_OG = 10
P16_GDN_Q = 11
P16_GDN_Z = 14
P16_RWKV_R = 15
P32_WIDTH = 768
P32_QLAT = 0
P32_RWKV_LAT = 1
P32_CKV = 4
P32_MISC = 5


def _cparams(sem):
    return pltpu.CompilerParams(dimension_semantics=sem, vmem_limit_bytes=VMEM_LIMIT)


def _dot(a, b):
    return jnp.dot(a, b, preferred_element_type=F32)


def _dot_nt(a, b):
    return lax.dot_general(a, b, (((1,), (1,)), ((), ())), preferred_element_type=F32)


def _dot_tn(a, b):
    return lax.dot_general(a, b, (((0,), (0,)), ((), ())), preferred_element_type=F32)


def _bmm(a, b):
    return jnp.einsum('bij,bjk->bik', a, b, preferred_element_type=F32)


def _bf(x):
    return x.astype(BF16)


def _split2(x):
    hi = x.astype(BF16)
    lo = (x - hi.astype(F32)).astype(BF16)
    return hi, lo


def _dot_sel(m_bf, x):
    hi, lo = _split2(x)
    return _dot(m_bf, hi) + _dot(m_bf, lo)


def _dot_sel_r(x, m_bf):
    hi, lo = _split2(x)
    return _dot(hi, m_bf) + _dot(lo, m_bf)


def _dot3(a, b):
    ah, al = _split2(a)
    bh, bl = _split2(b)
    return _dot(ah, bh) + _dot(al, bh) + _dot(ah, bl)


def _dot3_nt(a, b):
    ah, al = _split2(a)
    bh, bl = _split2(b)
    return _dot_nt(ah, bh) + _dot_nt(al, bh) + _dot_nt(ah, bl)


def _sigmoid(x):
    return 1.0 / (1.0 + jnp.exp(-x))


def _silu(x):
    return x * _sigmoid(x)


def _softplus(x):
    return jnp.maximum(x, 0.0) + jnp.log(1.0 + jnp.exp(-jnp.abs(x)))


def _iota2(shape, dim):
    return lax.broadcasted_iota(jnp.int32, shape, dim)


def _tri_masks(c):
    r = _iota2((c, c), 0)
    q = _iota2((c, c), 1)
    return r >= q, r > q


def _unit_lower_inverse(l_mat):
    c = l_mat.shape[-1]
    r = _iota2((c, c), 0)
    q = _iota2((c, c), 1)
    pair = jnp.where((r >> 1) == (q >> 1), 1.0, 0.0).astype(F32)
    t = jnp.where(r == q, 1.0, 0.0).astype(F32)[None] - l_mat * pair[None]
    s = 2
    while s < c:
        sh = s.bit_length()
        same = (r >> sh) == (q >> sh)
        m = jnp.where(same, jnp.where((r & s) != 0, jnp.where((q & s) == 0, 1.0, 0.0), 0.0), 0.0)
        ls = l_mat * m[None]
        tb = _bf(t)
        x = _bmm(_bf(_bmm(tb, _bf(ls))), tb)
        t = t - x
        s *= 2
    return t


def _inproj_body(x_ref, w_ref, o_ref, xb_ref):
    @pl.when(pl.program_id(2) == 0)
    def _():
        xb_ref[...] = x_ref[0].astype(BF16)

    o_ref[0] = _dot(xb_ref[...], w_ref[...]).astype(o_ref.dtype)


def _inproj(x, w, out_dtype, tm, tn):
    b, s, d = x.shape
    n = w.shape[1]
    return pl.pallas_call(
        _inproj_body,
        grid=(b, s // tm, n // tn),
        in_specs=[pl.BlockSpec((1, tm, d), lambda bi, i, j: (bi, i, 0)),
                  pl.BlockSpec((d, tn), lambda bi, i, j: (0, j))],
        out_specs=pl.BlockSpec((1, tm, tn), lambda bi, i, j: (bi, i, j)),
        out_shape=jax.ShapeDtypeStruct((b, s, n), out_dtype),
        scratch_shapes=[pltpu.VMEM((tm, d), BF16)],
        compiler_params=_cparams(("parallel", "parallel", "arbitrary")),
        name="inproj",
    )(x, w)


def _rope_table_body(pos_ref, freq_ref, cos_ref, sin_ref):
    ang = pos_ref[0].astype(F32) * freq_ref[...]
    cos_ref[0] = jnp.cos(ang)
    sin_ref[0] = jnp.sin(ang)


def _rope_tables(pos3, freq, tt):
    b, s, _ = pos3.shape
    out = pl.BlockSpec((1, tt, LANES), lambda bi, i: (bi, i, 0))
    return pl.pallas_call(
        _rope_table_body,
        grid=(b, s // tt),
        in_specs=[pl.BlockSpec((1, tt, 1), lambda bi, i: (bi, i, 0)), pl.BlockSpec((1, LANES), lambda bi, i: (0, 0))],
        out_specs=[out, out],
        out_shape=[jax.ShapeDtypeStruct((b, s, LANES), F32)] * 2,
        compiler_params=_cparams(("parallel", "parallel")),
        name="rope_tables",
    )(pos3, freq)


def _mla_prep_body(qlat_ref, ckv_ref, misc_ref, cos_ref, sin_ref, qnorm_ref, wqa_ref, wqb_ref, kvnorm_ref,
                   wuk_ref, wuv_ref, wkra_ref, wkrb_ref, q_ref, k_ref, v_ref):
    scale = (MLA_NOPE + MLA_ROPE) ** -0.5 * math.log2(math.e)
    cos_t = cos_ref[0]
    sin_t = sin_ref[0]

    ql = qlat_ref[0]
    qn = ql * lax.rsqrt(jnp.mean(ql * ql, axis=-1, keepdims=True) + NORM_EPS) * qnorm_ref[...]
    qnb = _bf(qn)
    qa = _dot(qnb, wqa_ref[...])
    qb = _dot(qnb, wqb_ref[...])

    ck = ckv_ref[0]
    kvn = ck * lax.rsqrt(jnp.mean(ck * ck, axis=-1, keepdims=True) + NORM_EPS) * kvnorm_ref[...]
    kvb = _bf(kvn)
    kn = _dot(kvb, wuk_ref[...])
    vt = _dot_nt(wuv_ref[...], kvb)
    misc = misc_ref[0]
    kr = _dot_sel_r(misc, wkra_ref[...]) * cos_t + _dot_sel_r(misc, wkrb_ref[...]) * sin_t

    for h in range(MLA_HEADS):
        sl = slice(h * LANES, (h + 1) * LANES)
        q_ref[0, h] = ((qa[:, sl] * cos_t + qb[:, sl] * sin_t) * scale).astype(BF16)
        k_ref[0, h] = (kn[:, sl] + kr).astype(BF16)
    t = vt.shape[1]
    ones = jnp.ones((MLA_VT_ROWS - MLA_V, t), F32)
    for h in range(MLA_HEADS):
        v_ref[0, h, 0] = jnp.concatenate([vt[h * MLA_V:(h + 1) * MLA_V], ones], axis=0).astype(BF16)


def _mla_prep(p32, cos_tab, sin_tab, qnorm, wqa, wqb, kvnorm, wuk, wuv, wkra, wkrb, tt):
    b, s, _ = p32.shape
    full = lambda shape: pl.BlockSpec(shape, lambda bi, i: (0,) * len(shape))
    return pl.pallas_call(
        _mla_prep_body,
        grid=(b, s // tt),
        in_specs=[pl.BlockSpec((1, tt, 256), lambda bi, i: (bi, i, P32_QLAT)),
                  pl.BlockSpec((1, tt, 128), lambda bi, i: (bi, i, P32_CKV)),
                  pl.BlockSpec((1, tt, 128), lambda bi, i: (bi, i, P32_MISC)),
                  pl.BlockSpec((1, tt, LANES), lambda bi, i: (bi, i, 0)),
                  pl.BlockSpec((1, tt, LANES), lambda bi, i: (bi, i, 0)),
                  full((1, 256)), full((256, 1024)), full((256, 1024)), full((1, 128)),
                  full((128, 1024)), full((512, 128)), full((128, 128)), full((128, 128))],
        out_specs=[pl.BlockSpec((1, MLA_HEADS, tt, 128), lambda bi, i: (bi, 0, i, 0)),
                   pl.BlockSpec((1, MLA_HEADS, tt, 128), lambda bi, i: (bi, 0, i, 0)),
                   pl.BlockSpec((1, MLA_HEADS, 1, MLA_VT_ROWS, tt), lambda bi, i: (bi, 0, i, 0, 0))],
        out_shape=[jax.ShapeDtypeStruct((b, MLA_HEADS, s, 128), BF16),
                   jax.ShapeDtypeStruct((b, MLA_HEADS, s, 128), BF16),
                   jax.ShapeDtypeStruct((b, MLA_HEADS, s // tt, MLA_VT_ROWS, tt), BF16)],
        compiler_params=_cparams(("parallel", "parallel")),
        name="mla_prep",
    )(p32, p32, p32, cos_tab, sin_tab, qnorm, wqa, wqb, kvnorm, wuk, wuv, wkra, wkrb)


def _mla_attn_body(q_ref, k_ref, v_ref, o_ref, m_ref, acc_ref, s_ref, *, tq):
    nh = MLA_GROUP
    i = pl.program_id(2)
    for hh in range(nh):
        m_ref[hh] = jnp.full((1, tq), -jnp.inf, F32)
        acc_ref[hh] = jnp.zeros((MLA_VT_ROWS, tq), F32)

    def scores(hh, j):
        start = pl.multiple_of(j * tq, tq)
        return _dot_nt(k_ref[0, hh, pl.ds(start, tq), :], q_ref[0, hh])

    def consume(hh, j, masked):
        st = s_ref[hh]
        if masked:
            st = jnp.where(_iota2((tq, tq), 0) <= _iota2((tq, tq), 1), st, -jnp.inf)
        m_old = m_ref[hh]
        m_new = jnp.maximum(m_old, jnp.max(st, axis=0, keepdims=True))
        p = jnp.exp2(st - m_new)
        acc_ref[hh] = acc_ref[hh] * jnp.exp2(m_old - m_new) + _dot(v_ref[0, hh, j], _bf(p))
        m_ref[hh] = m_new

    s_ref[0] = scores(0, 0)

    def step(j):
        for hh in range(nh):
            if hh + 1 < nh:
                s_ref[hh + 1] = scores(hh + 1, j)
            else:
                s_ref[0] = scores(0, j + 1)
            consume(hh, j, False)

    def loop_body(t, carry):
        step(2 * t)
        step(2 * t + 1)
        return carry

    lax.fori_loop(0, i // 2, loop_body, 0)

    @pl.when(i % 2 == 1)
    def _():
        step(i - 1)

    for hh in range(nh):
        if hh + 1 < nh:
            s_ref[hh + 1] = scores(hh + 1, i)
        consume(hh, i, True)
    outs = []
    for hh in range(nh):
        a = acc_ref[hh]
        outs.append(a[0:MLA_V] / a[MLA_V:MLA_V + 1])
    o_ref[0] = jnp.concatenate(outs, axis=0).T.astype(o_ref.dtype)


def _mla_attn(q, k, v, tq):
    b, h, s, _ = q.shape
    nh = MLA_GROUP
    once = pl.Buffered(1)
    return pl.pallas_call(
        functools.partial(_mla_attn_body, tq=tq),
        grid=(b, h // nh, s // tq),
        in_specs=[pl.BlockSpec((1, nh, tq, 128), lambda bi, p, i: (bi, p, i, 0)),
                  pl.BlockSpec((1, nh, s, 128), lambda bi, p, i: (bi, p, 0, 0), pipeline_mode=once),
                  pl.BlockSpec((1, nh, s // tq, MLA_VT_ROWS, tq), lambda bi, p, i: (bi, p, 0, 0, 0),
                               pipeline_mode=once)],
        out_specs=pl.BlockSpec((1, tq, nh * MLA_V), lambda bi, p, i: (bi, i, p)),
        out_shape=jax.ShapeDtypeStruct((b, s, h * MLA_V), BF16),
        scratch_shapes=[pltpu.VMEM((nh, 1, tq), F32), pltpu.VMEM((nh, MLA_VT_ROWS, tq), F32),
                        pltpu.VMEM((nh, tq, tq), F32)],
        compiler_params=_cparams(("parallel", "parallel", "arbitrary")),
        name="mla_attn",
    )(q, k, v)


def _gla_body(qk_ref, v_ref, og_ref, misc_ref, wgu_ref, bgate_ref, norm_ref, o_ref, st_ref, *, tt):
    c = CHUNK

    @pl.when(pl.program_id(1) == 0)
    def _():
        st_ref[...] = jnp.zeros_like(st_ref)

    causal, _ = _tri_masks(c)
    tri = jnp.where(causal, 1.0, 0.0).astype(BF16)
    lane = _iota2((1, LANES), 1)
    qk = qk_ref[0].astype(F32)
    q_all = qk[:, :256] * (GLA_DK ** -0.5)
    k_all = qk[:, 256:]
    ga = _dot3(misc_ref[0], wgu_ref[...]) + bgate_ref[...]
    log_a = (jnp.minimum(ga, 0.0) - jnp.log(1.0 + jnp.exp(-jnp.abs(ga)))) * (1.0 / GLA_TAU)

    units = []
    for ci in range(tt // c):
        rows = slice(ci * c, (ci + 1) * c)
        bcum = _dot_sel(tri, log_a[rows])
        b_last = bcum[c - 1:c]
        q_dec = q_all[rows] * jnp.exp(bcum)
        k_inv = k_all[rows] * jnp.exp(-bcum)
        k_end = k_all[rows] * jnp.exp(b_last - bcum)
        e_last = jnp.exp(b_last)
        for p in range(2):
            ls = slice(p * LANES, (p + 1) * LANES)
            qd_p, ki_p, ke_p, el_p = q_dec[:, ls], _bf(k_inv[:, ls]), _bf(k_end[:, ls]), e_last[:, ls]
            for hh in range(2):
                h = 2 * p + hh
                own = (lane >= hh * GLA_DK) & (lane < (hh + 1) * GLA_DK)
                qd_h = _bf(jnp.where(own, qd_p, 0.0))
                v_h = v_ref[0, rows, h * GLA_DV:(h + 1) * GLA_DV]
                units.append(dict(h=h, rows=rows, qd=qd_h, v=v_h, el=el_p,
                                  att=_bf(jnp.where(causal, _dot_nt(qd_h, ki_p), 0.0)),
                                  kv=jnp.where(own, _dot_tn(v_h, ke_p), 0.0)))
    o_intra = _bmm(jnp.stack([u['att'] for u in units], axis=0), jnp.stack([u['v'] for u in units], axis=0))

    for n, u in enumerate(units):
        h, rows = u['h'], u['rows']
        st = st_ref[h]
        o = o_intra[n] + _dot_nt(u['qd'], _bf(st))
        st_ref[h] = st * u['el'] + u['kv']
        o = o * lax.rsqrt(jnp.mean(o * o, axis=-1, keepdims=True) + NORM_EPS) * norm_ref[...]
        og = og_ref[0, rows, h * GLA_DV:(h + 1) * GLA_DV].astype(F32)
        o_ref[0, rows, h * GLA_DV:(h + 1) * GLA_DV] = (o * _silu(og)).astype(o_ref.dtype)


def _gla(p16, p32, wgu, bgate, norm, tt):
    b, s, _ = p16.shape
    full = lambda shape: pl.BlockSpec(shape, lambda bi, i: (0,) * len(shape))
    return pl.pallas_call(
        functools.partial(_gla_body, tt=tt),
        grid=(b, s // tt),
        in_specs=[pl.BlockSpec((1, tt, 512), lambda bi, i: (bi, i, P16_GLA_QK)),
                  pl.BlockSpec((1, tt, 512), lambda bi, i: (bi, i, P16_GLA_V)),
                  pl.BlockSpec((1, tt, 512), lambda bi, i: (bi, i, P16_GLA_OG)),
                  pl.BlockSpec((1, tt, 128), lambda bi, i: (bi, i, P32_MISC)),
                  full((128, 256)), full((1, 256)), full((1, 128))],
        out_specs=pl.BlockSpec((1, tt, 512), lambda bi, i: (bi, i, 0)),
        out_shape=jax.ShapeDtypeStruct((b, s, 512), BF16),
        scratch_shapes=[pltpu.VMEM((GLA_HEADS, GLA_DV, LANES), F32)],
        compiler_params=_cparams(("parallel", "arbitrary")),
        name="gla",
    )(p16, p16, p16, p32, wgu, bgate, norm)


def _gdn_body(q_ref, k_ref, v_ref, z_ref, misc_ref, conv_ref, alog_ref, dtb_ref, norm_ref, o_ref,
              ext_ref, st_ref, *, tt):
    c = CHUNK
    nc = tt // c

    @pl.when(pl.program_id(1) == 0)
    def _():
        st_ref[...] = jnp.zeros_like(st_ref)
        ext_ref[0:8, :] = jnp.zeros((8, 1536), F32)

    causal, strict = _tri_masks(c)
    tri = jnp.where(causal, 1.0, 0.0).astype(BF16)
    ones_cc = jnp.ones((c, c), BF16)
    eye = jnp.where(_iota2((c, c), 0) == _iota2((c, c), 1), 1.0, 0.0).astype(F32)

    ext_ref[8:8 + tt, 0:512] = q_ref[0].astype(F32)
    ext_ref[8:8 + tt, 512:1024] = k_ref[0].astype(F32)
    ext_ref[8:8 + tt, 1024:1536] = v_ref[0].astype(F32)
    conv = None
    for j in range(GDN_CONV):
        term = ext_ref[8 - (GDN_CONV - 1) + j:8 - (GDN_CONV - 1) + j + tt, :] * conv_ref[j:j + 1, :]
        conv = term if conv is None else conv + term
    ext_ref[0:8, :] = ext_ref[tt:tt + 8, :]
    qkv = _silu(conv)

    misc = misc_ref[0]
    beta_all = _sigmoid(misc)
    g_all = -jnp.exp(alog_ref[...]) * _softplus(misc + dtb_ref[...])

    qn, kn = [], []
    for h in range(GDN_HEADS):
        q = qkv[:, h * LANES:(h + 1) * LANES]
        k = qkv[:, 512 + h * LANES:512 + (h + 1) * LANES]
        qn.append(q * lax.rsqrt(jnp.sum(q * q, axis=-1, keepdims=True) + NORM_EPS) * (GDN_DK ** -0.5))
        kn.append(k * lax.rsqrt(jnp.sum(k * k, axis=-1, keepdims=True) + NORM_EPS))
    gc_chunks = [_dot_sel(tri, g_all[ci * c:(ci + 1) * c]) for ci in range(nc)]
    units = []
    for ci in range(nc):
        rows = slice(ci * c, (ci + 1) * c)
        for h in range(GDN_HEADS):
            q, k = qn[h][rows], kn[h][rows]
            v = qkv[rows, 1024 + h * LANES:1024 + (h + 1) * LANES]
            beta = beta_all[rows, MISC_GDN_BETA + h:MISC_GDN_BETA + h + 1]
            gc = gc_chunks[ci][:, MISC_GDN_A + h:MISC_GDN_A + h + 1]
            kb = k * beta
            eg = jnp.exp(gc)
            g_last = gc[c - 1:c]
            units.append(dict(gc=gc, q=_bf(q), k=_bf(k), kb=_bf(kb),
                              rhs=_bf(jnp.concatenate([v * beta, kb * eg], axis=1)), q_dec=q * eg,
                              k_end=_bf(k * jnp.exp(g_last - gc)), e_last=jnp.exp(g_last)))
    for u in units:
        u['gc_row'] = _dot_sel(ones_cc, eye * u['gc'])
    for u in units:
        u['kk'] = _dot_nt(u['kb'], u['k'])
    for u in units:
        u['qk'] = _dot_nt(u['q'], u['k'])
    l_list, pre = [], []
    for u in units:
        decay = jnp.where(causal, jnp.exp(jnp.minimum(u['gc'] - u['gc_row'], 0.0)), 0.0)
        l_list.append(jnp.where(strict, u['kk'] * decay, 0.0))
        pre.append(dict(rhs=u['rhs'], att=_bf(u['qk'] * decay), q_dec=u['q_dec'], k_end=u['k_end'],
                        e_last=u['e_last']))
    t_all = _unit_lower_inverse(jnp.stack(l_list, axis=0))

    stack = lambda key: jnp.stack([d[key] for d in pre], axis=0)
    uw = _bf(_bmm(_bf(t_all), stack('rhs')))
    ab = _bmm(stack('att'), uw)
    seq = []
    for n, d in enumerate(pre):
        kw = _dot_tn(d['k_end'], uw[n])
        seq.append(dict(h=_bf(d['q_dec'] - ab[n, :, LANES:]), z=ab[n, :, :LANES], m=_bf(kw[:, LANES:]),
                        n=kw[:, :LANES], e_last=d['e_last']))

    for ci in range(nc):
        rows = slice(ci * c, (ci + 1) * c)
        for h in range(GDN_HEADS):
            d = seq[ci * GDN_HEADS + h]
            st = st_ref[h]
            stb = _bf(st)
            o = _dot(d['h'], stb) + d['z']
            st_ref[h] = st * d['e_last'] - _dot(d['m'], stb) + d['n']
            o = o * lax.rsqrt(jnp.mean(o * o, axis=-1, keepdims=True) + NORM_EPS) * norm_ref[...]
            z = z_ref[0, rows, h * LANES:(h + 1) * LANES].astype(F32)
            o_ref[0, rows, h * LANES:(h + 1) * LANES] = (o * _silu(z)).astype(o_ref.dtype)


def _gdn(p16, p32, conv_w, alog, dtb, norm, tt):
    b, s, _ = p16.shape
    full = lambda shape: pl.BlockSpec(shape, lambda bi, i: (0,) * len(shape))
    return pl.pallas_call(
        functools.partial(_gdn_body, tt=tt),
        grid=(b, s // tt),
        in_specs=[pl.BlockSpec((1, tt, 512), lambda bi, i: (bi, i, P16_GDN_Q)),
                  pl.BlockSpec((1, tt, 512), lambda bi, i: (bi, i, P16_GDN_Q + 1)),
                  pl.BlockSpec((1, tt, 512), lambda bi, i: (bi, i, P16_GDN_Q + 2)),
                  pl.BlockSpec((1, tt, 512), lambda bi, i: (bi, i, P16_GDN_Z)),
                  pl.BlockSpec((1, tt, 128), lambda bi, i: (bi, i, P32_MISC)),
                  full((GDN_CONV, 1536)), full((1, 128)), full((1, 128)), full((1, 128))],
        out_specs=pl.BlockSpec((1, tt, 512), lambda bi, i: (bi, i, 0)),
        out_shape=jax.ShapeDtypeStruct((b, s, 512), BF16),
        scratch_shapes=[pltpu.VMEM((tt + 8, 1536), F32), pltpu.VMEM((GDN_HEADS, GDN_DK, GDN_DV), F32)],
        compiler_params=_cparams(("parallel", "arbitrary")),
        name="gdn",
    )(p16, p16, p16, p16, p32, conv_w, alog, dtb, norm)


def _rwkv_unpack(refs, has_vres):
    if has_vres:
        (r_ref, k_ref, v_ref, lat_ref, misc_ref, vfirst_ref, mu_ref, mulat_ref, w0_ref, wup_ref, a0_ref, aup_ref,
         gup_ref, kk_ref, ka_ref, rk_ref, lng_ref, lnb_ref, vmu_ref, v0_ref, vup_ref, hsum_ref,
         o_ref, ext_ref, extl_ref, extm_ref, y_ref, st_ref) = refs
        vout_ref = None
    else:
        (r_ref, k_ref, v_ref, lat_ref, mu_ref, mulat_ref, w0_ref, wup_ref, a0_ref, aup_ref,
         gup_ref, kk_ref, ka_ref, rk_ref, lng_ref, lnb_ref, hsum_ref,
         o_ref, vout_ref, ext_ref, extl_ref, y_ref, st_ref) = refs
        misc_ref = vfirst_ref = vmu_ref = v0_ref = vup_ref = extm_ref = None
    return (r_ref, k_ref, v_ref, lat_ref, misc_ref, vfirst_ref, mu_ref, mulat_ref, w0_ref, wup_ref, a0_ref, aup_ref,
            gup_ref, kk_ref, ka_ref, rk_ref, lng_ref, lnb_ref, vmu_ref, v0_ref, vup_ref, hsum_ref,
            o_ref, vout_ref, ext_ref, extl_ref, extm_ref, y_ref, st_ref)


def _rwkv_body(*refs, tt, has_vres):
    (r_ref, k_ref, v_ref, lat_ref, misc_ref, _, _, _, _, _, _, _, _, _, _, _, _, _, _, _, _, _,
     _, _, ext_ref, extl_ref, extm_ref, _, st_ref) = _rwkv_unpack(refs, has_vres)

    @pl.when(pl.program_id(1) == 0)
    def _():
        st_ref[...] = jnp.zeros_like(st_ref)
        ext_ref[0:8, :] = jnp.zeros((8, 1536), F32)
        extl_ref[0:8, :] = jnp.zeros((8, 256), F32)
        if has_vres:
            extm_ref[0:8, :] = jnp.zeros((8, 128), F32)

    ext_ref[8:8 + tt, 0:512] = r_ref[0].astype(F32)
    ext_ref[8:8 + tt, 512:1024] = k_ref[0].astype(F32)
    ext_ref[8:8 + tt, 1024:1536] = v_ref[0].astype(F32)
    extl_ref[8:8 + tt, :] = lat_ref[0]
    if has_vres:
        extm_ref[8:8 + tt, :] = misc_ref[0]
    _rwkv_rows(refs, 0, tt, has_vres)
    ext_ref[0:8, :] = ext_ref[tt:tt + 8, :]
    extl_ref[0:8, :] = extl_ref[tt:tt + 8, :]
    if has_vres:
        extm_ref[0:8, :] = extm_ref[tt:tt + 8, :]


def _rwkv_rows(refs, lo, hi, has_vres):
    (r_ref, k_ref, v_ref, lat_ref, misc_ref, vfirst_ref, mu_ref, mulat_ref, w0_ref, wup_ref, a0_ref, aup_ref,
     gup_ref, kk_ref, ka_ref, rk_ref, lng_ref, lnb_ref, vmu_ref, v0_ref, vup_ref, hsum_ref,
     o_ref, vout_ref, ext_ref, extl_ref, extm_ref, y_ref, st_ref) = _rwkv_unpack(refs, has_vres)
    c = CHUNK
    nc = (hi - lo) // c
    n_pairs = RWKV_HEADS // 2

    cur = ext_ref[8 + lo:8 + hi, :]
    rkv = cur + (ext_ref[7 + lo:7 + hi, :] - cur) * mu_ref[...]
    curl = extl_ref[8 + lo:8 + hi, :]
    lat = curl + (extl_ref[7 + lo:7 + hi, :] - curl) * mulat_ref[...]

    r = rkv[:, 0:512]
    k = rkv[:, 512:1024]
    v = rkv[:, 1024:1536]
    lat_wa = lat[:, 0:128]
    lat_g = lat[:, 128:256]
    w_pre = w0_ref[...] + _dot(_bf(jnp.tanh(lat_wa)), wup_ref[...])
    w = -_softplus(-w_pre) - 0.5
    logw = -jnp.exp(w)
    a = _sigmoid(a0_ref[...] + _dot(_bf(lat_wa), aup_ref[...]))
    g = _dot(_bf(_sigmoid(lat_g)), gup_ref[...])
    if has_vres:
        curm = extm_ref[8 + lo:8 + hi, :]
        vlat = curm + (extm_ref[7 + lo:7 + hi, :] - curm) * vmu_ref[...]
        v_gate = _sigmoid(v0_ref[...] + _dot(_bf(vlat), vup_ref[...]))
        v = v + (vfirst_ref[0, lo:hi, :] - v) * v_gate
    else:
        vout_ref[0, lo:hi, :] = v

    hsum = hsum_ref[...]
    kkv = k * kk_ref[...]
    kk = kkv * lax.rsqrt(_dot_sel_r(kkv * kkv, hsum) + NORM_EPS)
    k = k * (1.0 + (a - 1.0) * ka_ref[...])
    bonus = _dot(_bf(r * k * rk_ref[...]), hsum) * v
    bvec = kk * a

    lane = _iota2((1, LANES), 1)
    first = lane < RWKV_N
    rowh = _iota2((LANES, LANES), 0) < RWKV_N
    colh = _iota2((LANES, LANES), 1) < RWKV_N
    blockdiag = rowh == colh
    causal, strict = _tri_masks(c)
    tri = jnp.where(causal, 1.0, 0.0).astype(BF16)

    l_list, pre = [], []
    for ci in range(nc):
        rows = slice(ci * c, (ci + 1) * c)
        lw = logw[rows]
        cum = _dot_sel(tri, lw)
        gam = jnp.exp(cum)
        inv = jnp.exp(-cum)
        gam_ex = jnp.exp(cum - lw)
        gam_c = gam[c - 1:c]
        a_t = -kk[rows] * gam_ex
        r_t = r[rows] * gam
        b_t = bvec[rows] * inv
        k_t = k[rows] * inv
        for p in range(n_pairs):
            ls = slice(p * LANES, (p + 1) * LANES)
            a_p, r_p, b_p, k_p = a_t[:, ls], r_t[:, ls], b_t[:, ls], k_t[:, ls]
            a_h = [jnp.where(first, a_p, 0.0), jnp.where(first, 0.0, a_p)]
            r_h = [jnp.where(first, r_p, 0.0), jnp.where(first, 0.0, r_p)]
            lhs = _bf(jnp.concatenate(a_h + r_h, axis=0))
            m_b = _dot_nt(lhs, _bf(b_p))
            m_k = _dot_nt(lhs, _bf(k_p))
            heads = []
            for hh in range(2):
                l_list.append(jnp.where(strict, -m_b[hh * c:(hh + 1) * c], 0.0))
                heads.append(dict(a_h=a_h[hh], r_h=r_h[hh],
                                  a_ak=_bf(jnp.where(strict, m_k[hh * c:(hh + 1) * c], 0.0)),
                                  a_rb=_bf(jnp.where(causal, m_b[(2 + hh) * c:(3 + hh) * c], 0.0)),
                                  a_rk=_bf(jnp.where(causal, m_k[(2 + hh) * c:(3 + hh) * c], 0.0))))
            pre.append(dict(heads=heads, gam_c=gam_c[:, ls],
                            b_end=_bf(b_p * gam_c[:, ls]), k_end=_bf(k_p * gam_c[:, ls]),
                            v_p=_bf(v[rows, ls])))
    t_all = _unit_lower_inverse(jnp.stack(l_list, axis=0))

    heads_all = [hd for d in pre for hd in d['heads']]
    stack = lambda key: jnp.stack([hd[key] for hd in heads_all], axis=0)
    v_all = jnp.stack([d['v_p'] for d in pre for _ in range(2)], axis=0)
    akv = _bmm(stack('a_ak'), v_all)
    gq = _bf(_bmm(_bf(t_all), _bf(jnp.concatenate([stack('a_h'), akv], axis=2))))
    rb = _bmm(stack('a_rb'), gq)
    h_all = stack('r_h') + rb[:, :, :LANES]
    z_all = rb[:, :, LANES:] + _bmm(stack('a_rk'), v_all)
    seq = []
    for n, d in enumerate(pre):
        g_p = gq[2 * n, :, :LANES] + gq[2 * n + 1, :, :LANES]
        q1_p = jnp.where(first, gq[2 * n, :, LANES:], gq[2 * n + 1, :, LANES:])
        m_p = jnp.where(blockdiag, _dot_tn(g_p, d['b_end']), 0.0)
        n_p = jnp.where(blockdiag, _dot_tn(q1_p, d['b_end']) + _dot_tn(d['v_p'], d['k_end']), 0.0)
        seq.append(dict(h=_bf(h_all[2 * n] + h_all[2 * n + 1]), z=jnp.where(first, z_all[2 * n], z_all[2 * n + 1]),
                        m=_bf(m_p), n=n_p, gam_c=d['gam_c']))

    for ci in range(nc):
        for p in range(n_pairs):
            d = seq[ci * n_pairs + p]
            st = st_ref[p]
            stb = _bf(st)
            y = _dot_nt(d['h'], stb) + d['z']
            st_ref[p] = st * d['gam_c'] + _dot(stb, d['m']) + d['n']
            y_ref[lo + ci * c:lo + (ci + 1) * c, p * LANES:(p + 1) * LANES] = y

    y = y_ref[lo:hi, :]
    mean = _dot_sel_r(y, hsum) * (1.0 / RWKV_N)
    yc = y - mean
    var = _dot(_bf(yc * yc), hsum) * (1.0 / RWKV_N)
    yn = yc * lax.rsqrt(var + RWKV_LN_EPS) * lng_ref[...] + lnb_ref[...]
    o_ref[0, lo:hi, :] = ((yn + bonus) * g).astype(o_ref.dtype)


def _rwkv(p16, p32, v_first, weights, tt, has_vres):
    b, s, _ = p16.shape
    full = lambda shape: pl.BlockSpec(shape, lambda bi, i: (0,) * len(shape))
    tok = lambda w, blk: pl.BlockSpec((1, tt, w), lambda bi, i: (bi, i, blk))
    in_specs = [tok(512, P16_RWKV_R), tok(512, P16_RWKV_R + 1), tok(512, P16_RWKV_R + 2), tok(256, P32_RWKV_LAT)]
    args = [p16, p16, p16, p32]
    if has_vres:
        in_specs += [tok(128, P32_MISC), tok(512, 0)]
        args += [p32, v_first]
    in_specs += [full(w.shape) for w in weights]
    args += list(weights)
    out_specs = [pl.BlockSpec((1, tt, 512), lambda bi, i: (bi, i, 0))]
    out_shape = [jax.ShapeDtypeStruct((b, s, 512), BF16)]
    scratch = [pltpu.VMEM((tt + 8, 1536), F32), pltpu.VMEM((tt + 8, 256), F32)]
    if has_vres:
        scratch.append(pltpu.VMEM((tt + 8, 128), F32))
    else:
        out_specs.append(pl.BlockSpec((1, tt, 512), lambda bi, i: (bi, i, 0)))
        out_shape.append(jax.ShapeDtypeStruct((b, s, 512), F32))
    scratch.append(pltpu.VMEM((tt, RWKV_WIDTH), F32))
    scratch.append(pltpu.VMEM((RWKV_HEADS // 2, LANES, LANES), F32))
    return pl.pallas_call(
        functools.partial(_rwkv_body, tt=tt, has_vres=has_vres),
        grid=(b, s // tt),
        in_specs=in_specs,
        out_specs=out_specs,
        out_shape=out_shape,
        scratch_shapes=scratch,
        compiler_params=_cparams(("parallel", "arbitrary")),
        name="rwkv7",
    )(*args)


def _layernorm(h, g, b):
    mu = jnp.mean(h, axis=-1, keepdims=True)
    hc = h - mu
    var = jnp.mean(hc * hc, axis=-1, keepdims=True)
    return hc * lax.rsqrt(var + LN_EPS) * g + b


def _merge_body(ya_ref, yb_ref, yc_ref, yd_ref, gate_ref, x_ref, wb_ref, wo_ref, g_ref, b_ref, o_ref):
    merged = None
    for n, y_ref in enumerate((ya_ref, yb_ref, yc_ref, yd_ref)):
        gate = _sigmoid(gate_ref[0, :, n * D_MODEL:(n + 1) * D_MODEL].astype(F32))
        term = gate * _dot(y_ref[0], wb_ref[n])
        merged = term if merged is None else merged + term
    mix = _dot(_bf(merged), wo_ref[...])
    o_ref[0] = _layernorm(DEEPNORM_ALPHA * x_ref[0] + mix, g_ref[...], b_ref[...])


def _merge(ys, p16, x, wb, wo, g, bb, tt):
    b, s, d = x.shape
    full = lambda shape: pl.BlockSpec(shape, lambda bi, i: (0,) * len(shape))
    ytok = pl.BlockSpec((1, tt, 512), lambda bi, i: (bi, i, 0))
    return pl.pallas_call(
        _merge_body,
        grid=(b, s // tt),
        in_specs=[ytok, ytok, ytok, ytok,
                  pl.BlockSpec((1, tt, N_BRANCH * D_MODEL), lambda bi, i: (bi, i, P16_GATES)),
                  pl.BlockSpec((1, tt, d), lambda bi, i: (bi, i, 0)),
                  full((N_BRANCH, BRANCH_WIDTH, D_MODEL)), full((D_MODEL, D_MODEL)), full((1, d)), full((1, d))],
        out_specs=pl.BlockSpec((1, tt, d), lambda bi, i: (bi, i, 0)),
        out_shape=jax.ShapeDtypeStruct((b, s, d), F32),
        compiler_params=_cparams(("parallel", "parallel")),
        name="merge_ln1",
    )(*ys, p16, x, wb, wo, g, bb)


def _moe_body(x_ref, rwt_ref, rbias_ref, tri_ref, wg_ref, wu_ref, wd_ref, g_ref, b_ref, o_ref,
              xb_ref, comb_ref, grp_ref, acc_ref):
    gidx = pl.program_id(2)
    tt = x_ref.shape[1]

    @pl.when(gidx == 0)
    def _():
        x = x_ref[0]
        xb_ref[...] = x.astype(BF16)
        acc_ref[...] = jnp.zeros_like(acc_ref)
        scores = _sigmoid(_dot3_nt(rwt_ref[...], x))
        biased = scores + rbias_ref[...]
        row = _iota2((N_EXPERTS, tt), 0).astype(F32)
        row_group = (_iota2((N_EXPERTS, tt), 0) >> 2).astype(F32)
        neg = -jnp.inf
        gs = []
        for gi in range(N_GROUPS):
            rows = [biased[gi * GROUP_SIZE + j:gi * GROUP_SIZE + j + 1] for j in range(GROUP_SIZE)]
            best = None
            for i0 in range(GROUP_SIZE):
                for i1 in range(i0 + 1, GROUP_SIZE):
                    pair = rows[i0] + rows[i1]
                    best = pair if best is None else jnp.maximum(best, pair)
            gs.append(best)
        gmax = functools.reduce(jnp.maximum, gs)
        best_group = jnp.full((1, tt), float(N_GROUPS), F32)
        for gi in reversed(range(N_GROUPS)):
            best_group = jnp.where(gs[gi] == gmax, float(gi), best_group)
        masked = jnp.where(row_group == best_group, biased, neg)
        m1 = jnp.max(masked, axis=0, keepdims=True)
        i1 = jnp.min(jnp.where(masked == m1, row, float(N_EXPERTS)), axis=0, keepdims=True)
        masked2 = jnp.where(row == i1, neg, masked)
        m2 = jnp.max(masked2, axis=0, keepdims=True)
        i2 = jnp.min(jnp.where(masked2 == m2, row, float(N_EXPERTS)), axis=0, keepdims=True)
        s1 = jnp.sum(jnp.where(row == i1, scores, 0.0), axis=0, keepdims=True)
        s2 = jnp.sum(jnp.where(row == i2, scores, 0.0), axis=0, keepdims=True)
        tot = s1 + s2
        comb_t = jnp.where(row == i1, s1 / tot, 0.0) + jnp.where(row == i2, s2 / tot, 0.0)
        for gi in range(N_GROUPS):
            comb_ref[gi, 0:GROUP_SIZE, :] = comb_t[gi * GROUP_SIZE:(gi + 1) * GROUP_SIZE]
            comb_ref[gi, GROUP_SIZE:8, :] = jnp.zeros((8 - GROUP_SIZE, tt), F32)
        grp_ref[...] = jnp.broadcast_to(best_group, (8, tt))

    member = grp_ref[...] == gidx.astype(F32)
    cnt = _dot(jnp.where(member, 1.0, 0.0).astype(BF16), tri_ref[...])
    pos = jnp.where(member, cnt - 1.0, -1.0)
    n_tok = cnt[0:1, tt - 1:tt][0, 0].astype(jnp.int32)
    eye8 = jnp.where(_iota2((8, 8), 0) == _iota2((8, 8), 1), 1.0, 0.0).astype(BF16)
    p_hi, p_lo = _split2(pos)
    pos_col = (_dot_tn(p_hi, eye8) + _dot_tn(p_lo, eye8))[:, 0:1]
    pos_row = pos[0:1]
    c_hi, c_lo = _split2(comb_ref[gidx])
    xb = xb_ref[...]
    r = MOE_ROWS

    def sub_block(k, carry):
        base = (k * r).astype(F32)
        sel = jnp.where(pos_row == base + _iota2((r, 1), 0).astype(F32), 1.0, 0.0).astype(BF16)
        sel_t = jnp.where(pos_col == base + _iota2((1, r), 1).astype(F32), 1.0, 0.0).astype(BF16)
        xc = _bf(_dot(sel, xb))
        cw = _dot_nt(sel, c_hi) + _dot_nt(sel, c_lo)
        y = None
        for j in range(GROUP_SIZE):
            hid = _silu(_dot(xc, wg_ref[0, j])) * _dot(xc, wu_ref[0, j])
            term = cw[:, j:j + 1] * _dot(_bf(hid), wd_ref[0, j])
            y = term if y is None else y + term
        acc_ref[...] += _dot(sel_t, _bf(y))
        return carry

    lax.fori_loop(0, (n_tok + (r - 1)) // r, sub_block, 0)

    @pl.when(gidx == N_GROUPS - 1)
    def _():
        o_ref[0] = _layernorm(DEEPNORM_ALPHA * x_ref[0] + acc_ref[...], g_ref[...], b_ref[...])


def _moe(x, rwt, rbias, wg, wu, wd, g, bb, tt):
    b, s, d = x.shape
    full = lambda shape: pl.BlockSpec(shape, lambda bi, i, e: (0,) * len(shape))
    once = lambda shape: pl.BlockSpec(shape, lambda bi, i, e: (0,) * len(shape), pipeline_mode=pl.Buffered(1))
    grouped = lambda w: w.reshape((N_GROUPS, GROUP_SIZE) + w.shape[1:])
    idx = np.arange(tt)
    tri = jnp.asarray((idx[:, None] <= idx[None, :]).astype(np.float32), BF16)
    return pl.pallas_call(
        _moe_body,
        grid=(b, s // tt, N_GROUPS),
        in_specs=[pl.BlockSpec((1, tt, d), lambda bi, i, e: (bi, i, 0), pipeline_mode=pl.Buffered(1)),
                  full((N_EXPERTS, d)), full((N_EXPERTS, 1)), once((tt, tt)),
                  pl.BlockSpec((1, GROUP_SIZE, d, D_EXPERT), lambda bi, i, e: (e, 0, 0, 0)),
                  pl.BlockSpec((1, GROUP_SIZE, d, D_EXPERT), lambda bi, i, e: (e, 0, 0, 0)),
                  pl.BlockSpec((1, GROUP_SIZE, D_EXPERT, d), lambda bi, i, e: (e, 0, 0, 0)),
                  full((1, d)), full((1, d))],
        out_specs=pl.BlockSpec((1, tt, d), lambda bi, i, e: (bi, i, 0)),
        out_shape=jax.ShapeDtypeStruct((b, s, d), F32),
        scratch_shapes=[pltpu.VMEM((tt, d), BF16), pltpu.VMEM((N_GROUPS, 8, tt), F32), pltpu.VMEM((8, tt), F32),
                        pltpu.VMEM((tt, d), F32)],
        compiler_params=_cparams(("parallel", "parallel", "arbitrary")),
        name="moe_ln2",
    )(x, rwt, rbias, tri, grouped(wg), grouped(wu), grouped(wd), g, bb)


def _cols(w, name):
    o, n = _OFF[name]
    return w[:, o:o + n]


def _pack_inproj(w_in_l, w_vres_l):
    d = w_in_l.shape[0]
    w16 = jnp.concatenate([_cols(w_in_l, "gates"), _cols(w_in_l, "gla_q"), _cols(w_in_l, "gla_k"),
                           _cols(w_in_l, "gla_v"), _cols(w_in_l, "gla_og"), _cols(w_in_l, "gdn_qkv"),
                           _cols(w_in_l, "gdn_z"), _cols(w_in_l, "rwkv_rkv")], axis=1).astype(BF16)
    vres = jnp.zeros((d, RWKV_V_RANK), F32) if w_vres_l is None else w_vres_l
    misc = jnp.concatenate([_cols(w_in_l, "k_rope"), _cols(w_in_l, "gla_gl"), _cols(w_in_l, "gdn_beta"),
                            _cols(w_in_l, "gdn_a"), vres], axis=1)
    misc = jnp.pad(misc, ((0, 0), (0, LANES - misc.shape[1])))
    w32 = jnp.concatenate([_cols(w_in_l, "q_lat"), _cols(w_in_l, "rwkv_lat"), _cols(w_in_l, "c_kv"), misc],
                          axis=1).astype(BF16)
    return w16, w32


def _row_pad(w, start, total=LANES):
    return jnp.pad(w, ((start, total - start - w.shape[0]), (0, 0)))


def _lane_row(vec, start, total=LANES):
    return jnp.pad(vec, (start, total - start - vec.shape[0])).reshape(1, total)


def _pack_mla(w_uq, w_ukv):
    half = MLA_ROPE // 2
    wq = w_uq.reshape(MLA_Q_RANK, MLA_HEADS, MLA_NOPE + MLA_ROPE)
    nope, r1, r2 = wq[..., :MLA_NOPE], wq[..., MLA_NOPE:MLA_NOPE + half], wq[..., MLA_NOPE + half:]
    pad = jnp.zeros((MLA_Q_RANK, MLA_HEADS, LANES - MLA_NOPE - MLA_ROPE), F32)
    wqa = jnp.concatenate([nope, r1, r2, pad], axis=-1).reshape(MLA_Q_RANK, MLA_HEADS * LANES)
    wqb = jnp.concatenate([jnp.zeros_like(nope), -r2, r1, pad], axis=-1).reshape(MLA_Q_RANK, MLA_HEADS * LANES)
    wkv = w_ukv.reshape(MLA_KV_RANK, MLA_HEADS, MLA_NOPE + MLA_V)
    wuk = jnp.pad(wkv[..., :MLA_NOPE], ((0, 0), (0, 0), (0, LANES - MLA_NOPE))).reshape(MLA_KV_RANK, MLA_HEADS * LANES)
    wuv = wkv[..., MLA_NOPE:].reshape(MLA_KV_RANK, MLA_HEADS * MLA_V).T
    ra = np.zeros((LANES, LANES), np.float32)
    rb = np.zeros((LANES, LANES), np.float32)
    for j in range(half):
        ra[MISC_KROPE + j, MLA_NOPE + j] = 1.0
        ra[MISC_KROPE + half + j, MLA_NOPE + half + j] = 1.0
        rb[MISC_KROPE + half + j, MLA_NOPE + j] = -1.0
        rb[MISC_KROPE + j, MLA_NOPE + half + j] = 1.0
    return (wqa.astype(BF16), wqb.astype(BF16), wuk.astype(BF16), wuv.astype(BF16),
            jnp.asarray(ra, BF16), jnp.asarray(rb, BF16))


def _rope_freq_row():
    half = MLA_ROPE // 2
    inv_freq = ROPE_THETA ** (-jnp.arange(half, dtype=F32) / half)
    return jnp.concatenate([jnp.zeros((MLA_NOPE,), F32), inv_freq, inv_freq,
                            jnp.zeros((LANES - MLA_NOPE - MLA_ROPE,), F32)]).reshape(1, LANES)


def _head_sum_matrix():
    idx = np.arange(RWKV_WIDTH) // RWKV_N
    return jnp.asarray((idx[:, None] == idx[None, :]).astype(np.float32), BF16)


def _tile(s, pref):
    t = min(pref, s)
    assert s % t == 0 and t % CHUNK == 0
    return t


def kernel(x, positions, router_w, w_in, w_in_vres, mla_q_norm, mla_w_uq, mla_kv_norm, mla_w_ukv,
           gla_w_gate_up, gla_b_gate, gla_norm, gdn_conv, gdn_a_log, gdn_dt_bias, gdn_norm,
           rwkv_mu, rwkv_w0, rwkv_w_up, rwkv_a0, rwkv_a_up, rwkv_g_up, rwkv_k_k, rwkv_k_a, rwkv_r_k,
           rwkv_ln_g, rwkv_ln_b, rwkv_vres_mu, rwkv_v0, rwkv_v_up, w_branch, w_out,
           ln1_g, ln1_b, ln2_g, ln2_b, router_bias, moe_w_gate, moe_w_up, moe_w_down):
    b, s, d = x.shape
    assert d == D_MODEL and w_in.shape[2] == IN_COLS
    pos3 = positions.reshape(b, s, 1)
    rwt = router_w.T
    hsum = _head_sum_matrix()
    row = lambda v: v.reshape(1, -1)
    t_proj = _tile(s, 1024)
    t_scan = _tile(s, 256)
    t_rwkv = _tile(s, 256)
    t_attn = _tile(s, 512)
    t_merge = _tile(s, 512)
    t_moe = _tile(s, 1024)

    cos_tab, sin_tab = _rope_tables(pos3, _rope_freq_row(), t_attn)
    v_first = None
    for l in range(DEPTH):
        w16, w32 = _pack_inproj(w_in[l], None if l == 0 else w_in_vres[l - 1])
        p16 = _inproj(x, w16, BF16, t_proj, 1024)
        p32 = _inproj(x, w32, F32, t_proj, P32_WIDTH)

        wqa, wqb, wuk, wuv, wkra, wkrb = _pack_mla(mla_w_uq[l], mla_w_ukv[l])
        q, k, v = _mla_prep(p32, cos_tab, sin_tab, row(mla_q_norm[l]), wqa, wqb, row(mla_kv_norm[l]), wuk, wuv,
                            wkra, wkrb, t_attn)
        y_a = _mla_attn(q, k, v, t_attn)

        wgu = _row_pad(gla_w_gate_up[l], MISC_GLA_GL)
        y_b = _gla(p16, p32, wgu, row(gla_b_gate[l]), row(gla_norm[l]), t_scan)

        y_c = _gdn(p16, p32, gdn_conv[l], _lane_row(gdn_a_log[l], MISC_GDN_A), _lane_row(gdn_dt_bias[l], MISC_GDN_A),
                   row(gdn_norm[l]), t_scan)

        mu = rwkv_mu[l]
        weights = [row(mu[:1536]), row(mu[1536:]), row(rwkv_w0[l]),
                   _row_pad(rwkv_w_up[l], 0).astype(BF16), row(rwkv_a0[l]),
                   _row_pad(rwkv_a_up[l], RWKV_W_RANK).astype(BF16), rwkv_g_up[l].astype(BF16),
                   row(rwkv_k_k[l]), row(rwkv_k_a[l]), row(rwkv_r_k[l]), row(rwkv_ln_g[l]), row(rwkv_ln_b[l])]
        if l == 0:
            y_d, v_first = _rwkv(p16, p32, None, weights + [hsum], t_rwkv, False)
        else:
            weights += [_lane_row(rwkv_vres_mu[l - 1], MISC_VRES), row(rwkv_v0[l - 1]),
                        _row_pad(rwkv_v_up[l - 1], MISC_VRES).astype(BF16)]
            (y_d,) = _rwkv(p16, p32, v_first, weights + [hsum], t_rwkv, True)

        x = _merge((y_a, y_b, y_c, y_d), p16, x, w_branch[l].astype(BF16), w_out[l].astype(BF16),
                   row(ln1_g[l]), row(ln1_b[l]), t_merge)
        x = _moe(x, rwt, router_bias[l].reshape(N_EXPERTS, 1), moe_w_gate[l].astype(BF16),
                 moe_w_up[l].astype(BF16), moe_w_down[l].astype(BF16), row(ln2_g[l]), row(ln2_b[l]), t_moe)
    return x
```

```python
import functools
import math

import jax
import jax.numpy as jnp
import numpy as np
from jax import lax
from jax.experimental import pallas as pl
from jax.experimental.pallas import tpu as pltpu

F32 = jnp.float32
BF16 = jnp.bfloat16

LANES = 128
VMEM_LIMIT = 56 * 1024 * 1024

D_MODEL = 1024
DEPTH = 2
MLA_HEADS, MLA_NOPE, MLA_ROPE, MLA_V = 8, 64, 32, 64
MLA_Q_RANK, MLA_KV_RANK = 256, 128
MLA_GROUP = 4
MLA_VT_ROWS = MLA_V + 16
ROPE_THETA = 10000.0
GLA_HEADS, GLA_DK, GLA_DV, GLA_GATE_RANK, GLA_TAU = 4, 64, 128, 16, 16.0
GDN_HEADS, GDN_DK, GDN_DV, GDN_CONV = 4, 128, 128, 4
RWKV_HEADS, RWKV_N = 8, 64
RWKV_W_RANK, RWKV_A_RANK, RWKV_V_RANK, RWKV_G_RANK = 64, 64, 32, 128
RWKV_LN_EPS = 64e-5
CHUNK = 64
N_BRANCH, BRANCH_WIDTH = 4, 512
N_EXPERTS, N_GROUPS, TOP_K, D_EXPERT = 16, 4, 2, 512
GROUP_SIZE = N_EXPERTS // N_GROUPS
MOE_ROWS = 320
DEEPNORM_ALPHA = (2.0 * DEPTH) ** 0.25
LN_EPS = 1e-5
NORM_EPS = 1e-6
RWKV_WIDTH = RWKV_HEADS * RWKV_N

_OFF = {}
_o = 0
for _name, _w in (("q_lat", MLA_Q_RANK), ("c_kv", MLA_KV_RANK), ("k_rope", MLA_ROPE),
                  ("gla_q", 256), ("gla_k", 256), ("gla_v", 512), ("gla_gl", GLA_GATE_RANK), ("gla_og", 512),
                  ("gdn_qkv", 1536), ("gdn_beta", GDN_HEADS), ("gdn_a", GDN_HEADS), ("gdn_z", 512),
                  ("rwkv_rkv", 1536), ("rwkv_lat", 256), ("gates", N_BRANCH * D_MODEL)):
    _OFF[_name] = (_o, _w)
    _o += _w
IN_COLS = _o

MISC_KROPE = 0
MISC_GLA_GL = 32
MISC_GDN_BETA = 48
MISC_GDN_A = 52
MISC_VRES = 56

P16_WIDTH = 9216
P16_GATES = 0
P16_GLA_QK = 8
P16_GLA_V = 9
P16_GLA_OG = 10
P16_GDN_Q = 11
P16_GDN_Z = 14
P16_RWKV_R = 15
P32_WIDTH = 768
P32_QLAT = 0
P32_RWKV_LAT = 1
P32_CKV = 4
P32_MISC = 5


def _cparams(sem):
    return pltpu.CompilerParams(dimension_semantics=sem, vmem_limit_bytes=VMEM_LIMIT)


def _dot(a, b):
    return jnp.dot(a, b, preferred_element_type=F32)


def _dot_nt(a, b):
    return lax.dot_general(a, b, (((1,), (1,)), ((), ())), preferred_element_type=F32)


def _dot_tn(a, b):
    return lax.dot_general(a, b, (((0,), (0,)), ((), ())), preferred_element_type=F32)


def _bmm(a, b):
    return jnp.einsum('bij,bjk->bik', a, b, preferred_element_type=F32)


def _bf(x):
    return x.astype(BF16)


def _split2(x):
    hi = x.astype(BF16)
    lo = (x - hi.astype(F32)).astype(BF16)
    return hi, lo


def _dot_sel(m_bf, x):
    hi, lo = _split2(x)
    return _dot(m_bf, hi) + _dot(m_bf, lo)


def _dot_sel_r(x, m_bf):
    hi, lo = _split2(x)
    return _dot(hi, m_bf) + _dot(lo, m_bf)


def _dot3(a, b):
    ah, al = _split2(a)
    bh, bl = _split2(b)
    return _dot(ah, bh) + _dot(al, bh) + _dot(ah, bl)


def _dot3_nt(a, b):
    ah, al = _split2(a)
    bh, bl = _split2(b)
    return _dot_nt(ah, bh) + _dot_nt(al, bh) + _dot_nt(ah, bl)


def _sigmoid(x):
    return 1.0 / (1.0 + jnp.exp(-x))


def _silu(x):
    return x * _sigmoid(x)


def _softplus(x):
    return jnp.maximum(x, 0.0) + jnp.log(1.0 + jnp.exp(-jnp.abs(x)))


def _iota2(shape, dim):
    return lax.broadcasted_iota(jnp.int32, shape, dim)


def _tri_masks(c):
    r = _iota2((c, c), 0)
    q = _iota2((c, c), 1)
    return r >= q, r > q


def _unit_lower_inverse(l_mat):
    c = l_mat.shape[-1]
    r = _iota2((c, c), 0)
    q = _iota2((c, c), 1)
    pair = jnp.where((r >> 1) == (q >> 1), 1.0, 0.0).astype(F32)
    t = jnp.where(r == q, 1.0, 0.0).astype(F32)[None] - l_mat * pair[None]
    s = 2
    while s < c:
        sh = s.bit_length()
        same = (r >> sh) == (q >> sh)
        m = jnp.where(same, jnp.where((r & s) != 0, jnp.where((q & s) == 0, 1.0, 0.0), 0.0), 0.0)
        ls = l_mat * m[None]
        tb = _bf(t)
        x = _bmm(_bf(_bmm(tb, _bf(ls))), tb)
        t = t - x
        s *= 2
    return t


def _inproj_body(x_ref, w16_ref, w32_ref, o16_ref, o32_ref, xb_ref, *, n16_tiles):
    j = pl.program_id(2)

    @pl.when(j == 0)
    def _():
        xb_ref[...] = x_ref[0].astype(BF16)

    @pl.when(j < n16_tiles)
    def _():
        o16_ref[0] = _dot(xb_ref[...], w16_ref[...]).astype(o16_ref.dtype)

    @pl.when(j == n16_tiles)
    def _():
        o32_ref[0] = _dot(xb_ref[...], w32_ref[...])


def _inproj(x, w16, w32, tm, tn):
    b, s, d = x.shape
    n16, n32 = w16.shape[1], w32.shape[1]
    n16_tiles = n16 // tn
    col = lambda j: jnp.minimum(j, n16_tiles - 1)
    return pl.pallas_call(
        functools.partial(_inproj_body, n16_tiles=n16_tiles),
        grid=(b, s // tm, n16_tiles + 1),
        in_specs=[pl.BlockSpec((1, tm, d), lambda bi, i, j: (bi, i, 0)),
                  pl.BlockSpec((d, tn), lambda bi, i, j: (0, col(j))),
                  pl.BlockSpec((d, n32), lambda bi, i, j: (0, 0))],
        out_specs=[pl.BlockSpec((1, tm, tn), lambda bi, i, j: (bi, i, col(j))),
                   pl.BlockSpec((1, tm, n32), lambda bi, i, j: (bi, i, 0))],
        out_shape=[jax.ShapeDtypeStruct((b, s, n16), BF16), jax.ShapeDtypeStruct((b, s, n32), F32)],
        scratch_shapes=[pltpu.VMEM((tm, d), BF16)],
        compiler_params=_cparams(("parallel", "parallel", "arbitrary")),
        name="inproj",
    )(x, w16, w32)


def _rope_table_body(pos_ref, freq_ref, cos_ref, sin_ref):
    ang = pos_ref[0].astype(F32) * freq_ref[...]
    cos_ref[0] = jnp.cos(ang)
    sin_ref[0] = jnp.sin(ang)


def _rope_tables(pos3, freq, tt):
    b, s, _ = pos3.shape
    out = pl.BlockSpec((1, tt, LANES), lambda bi, i: (bi, i, 0))
    return pl.pallas_call(
        _rope_table_body,
        grid=(b, s // tt),
        in_specs=[pl.BlockSpec((1, tt, 1), lambda bi, i: (bi, i, 0)), pl.BlockSpec((1, LANES), lambda bi, i: (0, 0))],
        out_specs=[out, out],
        out_shape=[jax.ShapeDtypeStruct((b, s, LANES), F32)] * 2,
        compiler_params=_cparams(("parallel", "parallel")),
        name="rope_tables",
    )(pos3, freq)


def _mla_prep_body(qlat_ref, ckv_ref, misc_ref, cos_ref, sin_ref, qnorm_ref, wqa_ref, wqb_ref, kvnorm_ref,
                   wuk_ref, wuv_ref, wkra_ref, wkrb_ref, q_ref, k_ref, v_ref):
    scale = (MLA_NOPE + MLA_ROPE) ** -0.5 * math.log2(math.e)
    cos_t = cos_ref[0]
    sin_t = sin_ref[0]

    ql = qlat_ref[0]
    qn = ql * lax.rsqrt(jnp.mean(ql * ql, axis=-1, keepdims=True) + NORM_EPS) * qnorm_ref[...]
    qnb = _bf(qn)
    qa = _dot(qnb, wqa_ref[...])
    qb = _dot(qnb, wqb_ref[...])

    ck = ckv_ref[0]
    kvn = ck * lax.rsqrt(jnp.mean(ck * ck, axis=-1, keepdims=True) + NORM_EPS) * kvnorm_ref[...]
    kvb = _bf(kvn)
    kn = _dot(kvb, wuk_ref[...])
    vt = _dot_nt(wuv_ref[...], kvb)
    misc = misc_ref[0]
    kr = _dot_sel_r(misc, wkra_ref[...]) * cos_t + _dot_sel_r(misc, wkrb_ref[...]) * sin_t

    for h in range(MLA_HEADS):
        sl = slice(h * LANES, (h + 1) * LANES)
        q_ref[0, h] = ((qa[:, sl] * cos_t + qb[:, sl] * sin_t) * scale).astype(BF16)
        k_ref[0, h] = (kn[:, sl] + kr).astype(BF16)
    t = vt.shape[1]
    ones = jnp.ones((MLA_VT_ROWS - MLA_V, t), F32)
    for h in range(MLA_HEADS):
        v_ref[0, h, 0] = jnp.concatenate([vt[h * MLA_V:(h + 1) * MLA_V], ones], axis=0).astype(BF16)


def _mla_prep(p32, cos_tab, sin_tab, qnorm, wqa, wqb, kvnorm, wuk, wuv, wkra, wkrb, tt):
    b, s, _ = p32.shape
    full = lambda shape: pl.BlockSpec(shape, lambda bi, i: (0,) * len(shape))
    return pl.pallas_call(
        _mla_prep_body,
        grid=(b, s // tt),
        in_specs=[pl.BlockSpec((1, tt, 256), lambda bi, i: (bi, i, P32_QLAT)),
                  pl.BlockSpec((1, tt, 128), lambda bi, i: (bi, i, P32_CKV)),
                  pl.BlockSpec((1, tt, 128), lambda bi, i: (bi, i, P32_MISC)),
                  pl.BlockSpec((1, tt, LANES), lambda bi, i: (bi, i, 0)),
                  pl.BlockSpec((1, tt, LANES), lambda bi, i: (bi, i, 0)),
                  full((1, 256)), full((256, 1024)), full((256, 1024)), full((1, 128)),
                  full((128, 1024)), full((512, 128)), full((128, 128)), full((128, 128))],
        out_specs=[pl.BlockSpec((1, MLA_HEADS, tt, 128), lambda bi, i: (bi, 0, i, 0)),
                   pl.BlockSpec((1, MLA_HEADS, tt, 128), lambda bi, i: (bi, 0, i, 0)),
                   pl.BlockSpec((1, MLA_HEADS, 1, MLA_VT_ROWS, tt), lambda bi, i: (bi, 0, i, 0, 0))],
        out_shape=[jax.ShapeDtypeStruct((b, MLA_HEADS, s, 128), BF16),
                   jax.ShapeDtypeStruct((b, MLA_HEADS, s, 128), BF16),
                   jax.ShapeDtypeStruct((b, MLA_HEADS, s // tt, MLA_VT_ROWS, tt), BF16)],
        compiler_params=_cparams(("parallel", "parallel")),
        name="mla_prep",
    )(p32, p32, p32, cos_tab, sin_tab, qnorm, wqa, wqb, kvnorm, wuk, wuv, wkra, wkrb)


def _mla_attn_body(q_ref, k_ref, v_ref, o_ref, m_ref, acc_ref, s_ref, *, tq):
    nh = MLA_GROUP
    i = pl.program_id(2)
    for hh in range(nh):
        m_ref[hh] = jnp.full((1, tq), -jnp.inf, F32)
        acc_ref[hh] = jnp.zeros((MLA_VT_ROWS, tq), F32)

    def scores(hh, j):
        start = pl.multiple_of(j * tq, tq)
        return _dot_nt(k_ref[0, hh, pl.ds(start, tq), :], q_ref[0, hh])

    def consume(hh, j, masked):
        st = s_ref[hh]
        if masked:
            st = jnp.where(_iota2((tq, tq), 0) <= _iota2((tq, tq), 1), st, -jnp.inf)
        m_old = m_ref[hh]
        m_new = jnp.maximum(m_old, jnp.max(st, axis=0, keepdims=True))
        p = jnp.exp2(st - m_new)
        acc_ref[hh] = acc_ref[hh] * jnp.exp2(m_old - m_new) + _dot(v_ref[0, hh, j], _bf(p))
        m_ref[hh] = m_new

    s_ref[0] = scores(0, 0)

    def step(j):
        for hh in range(nh):
            if hh + 1 < nh:
                s_ref[hh + 1] = scores(hh + 1, j)
            else:
                s_ref[0] = scores(0, j + 1)
            consume(hh, j, False)

    def loop_body(t, carry):
        step(2 * t)
        step(2 * t + 1)
        return carry

    lax.fori_loop(0, i // 2, loop_body, 0)

    @pl.when(i % 2 == 1)
    def _():
        step(i - 1)

    for hh in range(nh):
        if hh + 1 < nh:
            s_ref[hh + 1] = scores(hh + 1, i)
        consume(hh, i, True)
    outs = []
    for hh in range(nh):
        a = acc_ref[hh]
        outs.append(a[0:MLA_V] / a[MLA_V:MLA_V + 1])
    o_ref[0] = jnp.concatenate(outs, axis=0).T.astype(o_ref.dtype)


def _mla_attn(q, k, v, tq):
    b, h, s, _ = q.shape
    nh = MLA_GROUP
    once = pl.Buffered(1)
    return pl.pallas_call(
        functools.partial(_mla_attn_body, tq=tq),
        grid=(b, h // nh, s // tq),
        in_specs=[pl.BlockSpec((1, nh, tq, 128), lambda bi, p, i: (bi, p, i, 0)),
                  pl.BlockSpec((1, nh, s, 128), lambda bi, p, i: (bi, p, 0, 0), pipeline_mode=once),
                  pl.BlockSpec((1, nh, s // tq, MLA_VT_ROWS, tq), lambda bi, p, i: (bi, p, 0, 0, 0),
                               pipeline_mode=once)],
        out_specs=pl.BlockSpec((1, tq, nh * MLA_V), lambda bi, p, i: (bi, i, p)),
        out_shape=jax.ShapeDtypeStruct((b, s, h * MLA_V), BF16),
        scratch_shapes=[pltpu.VMEM((nh, 1, tq), F32), pltpu.VMEM((nh, MLA_VT_ROWS, tq), F32),
                        pltpu.VMEM((nh, tq, tq), F32)],
        compiler_params=_cparams(("parallel", "parallel", "arbitrary")),
        name="mla_attn",
    )(q, k, v)


def _gla_body(qk_ref, v_ref, og_ref, misc_ref, wgu_ref, bgate_ref, norm_ref, o_ref, st_ref, *, tt):
    c = CHUNK

    @pl.when(pl.program_id(1) == 0)
    def _():
        st_ref[...] = jnp.zeros_like(st_ref)

    causal, _ = _tri_masks(c)
    tri = jnp.where(causal, 1.0, 0.0).astype(BF16)
    lane = _iota2((1, LANES), 1)
    qk = qk_ref[0].astype(F32)
    q_all = qk[:, :256] * (GLA_DK ** -0.5)
    k_all = qk[:, 256:]
    ga = _dot3(misc_ref[0], wgu_ref[...]) + bgate_ref[...]
    log_a = (jnp.minimum(ga, 0.0) - jnp.log(1.0 + jnp.exp(-jnp.abs(ga)))) * (1.0 / GLA_TAU)

    units = []
    for ci in range(tt // c):
        rows = slice(ci * c, (ci + 1) * c)
        bcum = _dot_sel(tri, log_a[rows])
        b_last = bcum[c - 1:c]
        q_dec = q_all[rows] * jnp.exp(bcum)
        k_inv = k_all[rows] * jnp.exp(-bcum)
        k_end = k_all[rows] * jnp.exp(b_last - bcum)
        e_last = jnp.exp(b_last)
        for p in range(2):
            ls = slice(p * LANES, (p + 1) * LANES)
            qd_p, ki_p, ke_p, el_p = q_dec[:, ls], _bf(k_inv[:, ls]), _bf(k_end[:, ls]), e_last[:, ls]
            for hh in range(2):
                h = 2 * p + hh
                own = (lane >= hh * GLA_DK) & (lane < (hh + 1) * GLA_DK)
                qd_h = _bf(jnp.where(own, qd_p, 0.0))
                v_h = v_ref[0, rows, h * GLA_DV:(h + 1) * GLA_DV]
                units.append(dict(h=h, rows=rows, qd=qd_h, v=v_h, el=el_p,
                                  att=_bf(jnp.where(causal, _dot_nt(qd_h, ki_p), 0.0)),
                                  kv=jnp.where(own, _dot_tn(v_h, ke_p), 0.0)))
    o_intra = _bmm(jnp.stack([u['att'] for u in units], axis=0), jnp.stack([u['v'] for u in units], axis=0))

    for n, u in enumerate(units):
        h, rows = u['h'], u['rows']
        st = st_ref[h]
        o = o_intra[n] + _dot_nt(u['qd'], _bf(st))
        st_ref[h] = st * u['el'] + u['kv']
        o = o * lax.rsqrt(jnp.mean(o * o, axis=-1, keepdims=True) + NORM_EPS) * norm_ref[...]
        og = og_ref[0, rows, h * GLA_DV:(h + 1) * GLA_DV].astype(F32)
        o_ref[0, rows, h * GLA_DV:(h + 1) * GLA_DV] = (o * _silu(og)).astype(o_ref.dtype)


def _gla(p16, p32, wgu, bgate, norm, tt):
    b, s, _ = p16.shape
    full = lambda shape: pl.BlockSpec(shape, lambda bi, i: (0,) * len(shape))
    return pl.pallas_call(
        functools.partial(_gla_body, tt=tt),
        grid=(b, s // tt),
        in_specs=[pl.BlockSpec((1, tt, 512), lambda bi, i: (bi, i, P16_GLA_QK)),
                  pl.BlockSpec((1, tt, 512), lambda bi, i: (bi, i, P16_GLA_V)),
                  pl.BlockSpec((1, tt, 512), lambda bi, i: (bi, i, P16_GLA_OG)),
                  pl.BlockSpec((1, tt, 128), lambda bi, i: (bi, i, P32_MISC)),
                  full((128, 256)), full((1, 256)), full((1, 128))],
        out_specs=pl.BlockSpec((1, tt, 512), lambda bi, i: (bi, i, 0)),
        out_shape=jax.ShapeDtypeStruct((b, s, 512), BF16),
        scratch_shapes=[pltpu.VMEM((GLA_HEADS, GLA_DV, LANES), F32)],
        compiler_params=_cparams(("parallel", "arbitrary")),
        name="gla",
    )(p16, p16, p16, p32, wgu, bgate, norm)


def _gdn_body(q_ref, k_ref, v_ref, z_ref, misc_ref, conv_ref, alog_ref, dtb_ref, norm_ref, o_ref,
              ext_ref, st_ref, *, tt):
    c = CHUNK
    nc = tt // c

    @pl.when(pl.program_id(1) == 0)
    def _():
        st_ref[...] = jnp.zeros_like(st_ref)
        ext_ref[0:8, :] = jnp.zeros((8, 1536), F32)

    causal, strict = _tri_masks(c)
    tri = jnp.where(causal, 1.0, 0.0).astype(BF16)
    ones_cc = jnp.ones((c, c), BF16)
    eye = jnp.where(_iota2((c, c), 0) == _iota2((c, c), 1), 1.0, 0.0).astype(F32)

    ext_ref[8:8 + tt, 0:512] = q_ref[0].astype(F32)
    ext_ref[8:8 + tt, 512:1024] = k_ref[0].astype(F32)
    ext_ref[8:8 + tt, 1024:1536] = v_ref[0].astype(F32)
    conv = None
    for j in range(GDN_CONV):
        term = ext_ref[8 - (GDN_CONV - 1) + j:8 - (GDN_CONV - 1) + j + tt, :] * conv_ref[j:j + 1, :]
        conv = term if conv is None else conv + term
    ext_ref[0:8, :] = ext_ref[tt:tt + 8, :]
    qkv = _silu(conv)

    misc = misc_ref[0]
    beta_all = _sigmoid(misc)
    g_all = -jnp.exp(alog_ref[...]) * _softplus(misc + dtb_ref[...])

    qn, kn = [], []
    for h in range(GDN_HEADS):
        q = qkv[:, h * LANES:(h + 1) * LANES]
        k = qkv[:, 512 + h * LANES:512 + (h + 1) * LANES]
        qn.append(q * lax.rsqrt(jnp.sum(q * q, axis=-1, keepdims=True) + NORM_EPS) * (GDN_DK ** -0.5))
        kn.append(k * lax.rsqrt(jnp.sum(k * k, axis=-1, keepdims=True) + NORM_EPS))
    gc_chunks = [_dot_sel(tri, g_all[ci * c:(ci + 1) * c]) for ci in range(nc)]
    units = []
    for ci in range(nc):
        rows = slice(ci * c, (ci + 1) * c)
        for h in range(GDN_HEADS):
            q, k = qn[h][rows], kn[h][rows]
            v = qkv[rows, 1024 + h * LANES:1024 + (h + 1) * LANES]
            beta = beta_all[rows, MISC_GDN_BETA + h:MISC_GDN_BETA + h + 1]
            gc = gc_chunks[ci][:, MISC_GDN_A + h:MISC_GDN_A + h + 1]
            kb = k * beta
            eg = jnp.exp(gc)
            g_last = gc[c - 1:c]
            units.append(dict(gc=gc, q=_bf(q), k=_bf(k), kb=_bf(kb),
                              rhs=_bf(jnp.concatenate([v * beta, kb * eg], axis=1)), q_dec=q * eg,
                              k_end=_bf(k * jnp.exp(g_last - gc)), e_last=jnp.exp(g_last)))
    for u in units:
        u['gc_row'] = _dot_sel(ones_cc, eye * u['gc'])
    for u in units:
        u['kk'] = _dot_nt(u['kb'], u['k'])
    for u in units:
        u['qk'] = _dot_nt(u['q'], u['k'])
    l_list, pre = [], []
    for u in units:
        decay = jnp.where(causal, jnp.exp(jnp.minimum(u['gc'] - u['gc_row'], 0.0)), 0.0)
        l_list.append(jnp.where(strict, u['kk'] * decay, 0.0))
        pre.append(dict(rhs=u['rhs'], att=_bf(u['qk'] * decay), q_dec=u['q_dec'], k_end=u['k_end'],
                        e_last=u['e_last']))
    t_all = _unit_lower_inverse(jnp.stack(l_list, axis=0))

    stack = lambda key: jnp.stack([d[key] for d in pre], axis=0)
    uw = _bf(_bmm(_bf(t_all), stack('rhs')))
    ab = _bmm(stack('att'), uw)
    seq = []
    for n, d in enumerate(pre):
        kw = _dot_tn(d['k_end'], uw[n])
        seq.append(dict(h=_bf(d['q_dec'] - ab[n, :, LANES:]), z=ab[n, :, :LANES], m=_bf(kw[:, LANES:]),
                        n=kw[:, :LANES], e_last=d['e_last']))

    for ci in range(nc):
        rows = slice(ci * c, (ci + 1) * c)
        for h in range(GDN_HEADS):
            d = seq[ci * GDN_HEADS + h]
            st = st_ref[h]
            stb = _bf(st)
            o = _dot(d['h'], stb) + d['z']
            st_ref[h] = st * d['e_last'] - _dot(d['m'], stb) + d['n']
            o = o * lax.rsqrt(jnp.mean(o * o, axis=-1, keepdims=True) + NORM_EPS) * norm_ref[...]
            z = z_ref[0, rows, h * LANES:(h + 1) * LANES].astype(F32)
            o_ref[0, rows, h * LANES:(h + 1) * LANES] = (o * _silu(z)).astype(o_ref.dtype)


def _gdn(p16, p32, conv_w, alog, dtb, norm, tt):
    b, s, _ = p16.shape
    full = lambda shape: pl.BlockSpec(shape, lambda bi, i: (0,) * len(shape))
    return pl.pallas_call(
        functools.partial(_gdn_body, tt=tt),
        grid=(b, s // tt),
        in_specs=[pl.BlockSpec((1, tt, 512), lambda bi, i: (bi, i, P16_GDN_Q)),
                  pl.BlockSpec((1, tt, 512), lambda bi, i: (bi, i, P16_GDN_Q + 1)),
                  pl.BlockSpec((1, tt, 512), lambda bi, i: (bi, i, P16_GDN_Q + 2)),
                  pl.BlockSpec((1, tt, 512), lambda bi, i: (bi, i, P16_GDN_Z)),
                  pl.BlockSpec((1, tt, 128), lambda bi, i: (bi, i, P32_MISC)),
                  full((GDN_CONV, 1536)), full((1, 128)), full((1, 128)), full((1, 128))],
        out_specs=pl.BlockSpec((1, tt, 512), lambda bi, i: (bi, i, 0)),
        out_shape=jax.ShapeDtypeStruct((b, s, 512), BF16),
        scratch_shapes=[pltpu.VMEM((tt + 8, 1536), F32), pltpu.VMEM((GDN_HEADS, GDN_DK, GDN_DV), F32)],
        compiler_params=_cparams(("parallel", "arbitrary")),
        name="gdn",
    )(p16, p16, p16, p16, p32, conv_w, alog, dtb, norm)


def _rwkv_unpack(refs, has_vres):
    if has_vres:
        (r_ref, k_ref, v_ref, lat_ref, misc_ref, vfirst_ref, mu_ref, mulat_ref, w0_ref, wup_ref, a0_ref, aup_ref,
         gup_ref, kk_ref, ka_ref, rk_ref, lng_ref, lnb_ref, vmu_ref, v0_ref, vup_ref, hsum_ref,
         o_ref, ext_ref, extl_ref, extm_ref, y_ref, st_ref) = refs
        vout_ref = None
    else:
        (r_ref, k_ref, v_ref, lat_ref, mu_ref, mulat_ref, w0_ref, wup_ref, a0_ref, aup_ref,
         gup_ref, kk_ref, ka_ref, rk_ref, lng_ref, lnb_ref, hsum_ref,
         o_ref, vout_ref, ext_ref, extl_ref, y_ref, st_ref) = refs
        misc_ref = vfirst_ref = vmu_ref = v0_ref = vup_ref = extm_ref = None
    return (r_ref, k_ref, v_ref, lat_ref, misc_ref, vfirst_ref, mu_ref, mulat_ref, w0_ref, wup_ref, a0_ref, aup_ref,
            gup_ref, kk_ref, ka_ref, rk_ref, lng_ref, lnb_ref, vmu_ref, v0_ref, vup_ref, hsum_ref,
            o_ref, vout_ref, ext_ref, extl_ref, extm_ref, y_ref, st_ref)


def _rwkv_body(*refs, tt, has_vres):
    (r_ref, k_ref, v_ref, lat_ref, misc_ref, _, _, _, _, _, _, _, _, _, _, _, _, _, _, _, _, _,
     _, _, ext_ref, extl_ref, extm_ref, _, st_ref) = _rwkv_unpack(refs, has_vres)

    @pl.when(pl.program_id(1) == 0)
    def _():
        st_ref[...] = jnp.zeros_like(st_ref)
        ext_ref[0:8, :] = jnp.zeros((8, 1536), F32)
        extl_ref[0:8, :] = jnp.zeros((8, 256), F32)
        if has_vres:
            extm_ref[0:8, :] = jnp.zeros((8, 128), F32)

    ext_ref[8:8 + tt, 0:512] = r_ref[0].astype(F32)
    ext_ref[8:8 + tt, 512:1024] = k_ref[0].astype(F32)
    ext_ref[8:8 + tt, 1024:1536] = v_ref[0].astype(F32)
    extl_ref[8:8 + tt, :] = lat_ref[0]
    if has_vres:
        extm_ref[8:8 + tt, :] = misc_ref[0]
    _rwkv_rows(refs, 0, tt, has_vres)
    ext_ref[0:8, :] = ext_ref[tt:tt + 8, :]
    extl_ref[0:8, :] = extl_ref[tt:tt + 8, :]
    if has_vres:
        extm_ref[0:8, :] = extm_ref[tt:tt + 8, :]


def _rwkv_rows(refs, lo, hi, has_vres):
    (r_ref, k_ref, v_ref, lat_ref, misc_ref, vfirst_ref, mu_ref, mulat_ref, w0_ref, wup_ref, a0_ref, aup_ref,
     gup_ref, kk_ref, ka_ref, rk_ref, lng_ref, lnb_ref, vmu_ref, v0_ref, vup_ref, hsum_ref,
     o_ref, vout_ref, ext_ref, extl_ref, extm_ref, y_ref, st_ref) = _rwkv_unpack(refs, has_vres)
    c = CHUNK
    nc = (hi - lo) // c
    n_pairs = RWKV_HEADS // 2

    cur = ext_ref[8 + lo:8 + hi, :]
    rkv = cur + (ext_ref[7 + lo:7 + hi, :] - cur) * mu_ref[...]
    curl = extl_ref[8 + lo:8 + hi, :]
    lat = curl + (extl_ref[7 + lo:7 + hi, :] - curl) * mulat_ref[...]

    r = rkv[:, 0:512]
    k = rkv[:, 512:1024]
    v = rkv[:, 1024:1536]
    lat_wa = lat[:, 0:128]
    lat_g = lat[:, 128:256]
    w_pre = w0_ref[...] + _dot(_bf(jnp.tanh(lat_wa)), wup_ref[...])
    w = -_softplus(-w_pre) - 0.5
    logw = -jnp.exp(w)
    a = _sigmoid(a0_ref[...] + _dot(_bf(lat_wa), aup_ref[...]))
    g = _dot(_bf(_sigmoid(lat_g)), gup_ref[...])
    if has_vres:
        curm = extm_ref[8 + lo:8 + hi, :]
        vlat = curm + (extm_ref[7 + lo:7 + hi, :] - curm) * vmu_ref[...]
        v_gate = _sigmoid(v0_ref[...] + _dot(_bf(vlat), vup_ref[...]))
        v = v + (vfirst_ref[0, lo:hi, :] - v) * v_gate
    else:
        vout_ref[0, lo:hi, :] = v

    hsum = hsum_ref[...]
    kkv = k * kk_ref[...]
    kk = kkv * lax.rsqrt(_dot_sel_r(kkv * kkv, hsum) + NORM_EPS)
    k = k * (1.0 + (a - 1.0) * ka_ref[...])
    bonus = _dot(_bf(r * k * rk_ref[...]), hsum) * v
    bvec = kk * a

    lane = _iota2((1, LANES), 1)
    first = lane < RWKV_N
    rowh = _iota2((LANES, LANES), 0) < RWKV_N
    colh = _iota2((LANES, LANES), 1) < RWKV_N
    blockdiag = rowh == colh
    causal, strict = _tri_masks(c)
    tri = jnp.where(causal, 1.0, 0.0).astype(BF16)

    l_list, pre = [], []
    for ci in range(nc):
        rows = slice(ci * c, (ci + 1) * c)
        lw = logw[rows]
        cum = _dot_sel(tri, lw)
        gam = jnp.exp(cum)
        inv = jnp.exp(-cum)
        gam_ex = jnp.exp(cum - lw)
        gam_c = gam[c - 1:c]
        a_t = -kk[rows] * gam_ex
        r_t = r[rows] * gam
        b_t = bvec[rows] * inv
        k_t = k[rows] * inv
        for p in range(n_pairs):
            ls = slice(p * LANES, (p + 1) * LANES)
            a_p, r_p, b_p, k_p = a_t[:, ls], r_t[:, ls], b_t[:, ls], k_t[:, ls]
            a_h = [jnp.where(first, a_p, 0.0), jnp.where(first, 0.0, a_p)]
            r_h = [jnp.where(first, r_p, 0.0), jnp.where(first, 0.0, r_p)]
            lhs = _bf(jnp.concatenate(a_h + r_h, axis=0))
            m_b = _dot_nt(lhs, _bf(b_p))
            m_k = _dot_nt(lhs, _bf(k_p))
            heads = []
            for hh in range(2):
                l_list.append(jnp.where(strict, -m_b[hh * c:(hh + 1) * c], 0.0))
                heads.append(dict(a_h=a_h[hh], r_h=r_h[hh],
                                  a_ak=_bf(jnp.where(strict, m_k[hh * c:(hh + 1) * c], 0.0)),
                                  a_rb=_bf(jnp.where(causal, m_b[(2 + hh) * c:(3 + hh) * c], 0.0)),
                                  a_rk=_bf(jnp.where(causal, m_k[(2 + hh) * c:(3 + hh) * c], 0.0))))
            pre.append(dict(heads=heads, gam_c=gam_c[:, ls],
                            b_end=_bf(b_p * gam_c[:, ls]), k_end=_bf(k_p * gam_c[:, ls]),
                            v_p=_bf(v[rows, ls])))
    t_all = _unit_lower_inverse(jnp.stack(l_list, axis=0))

    heads_all = [hd for d in pre for hd in d['heads']]
    stack = lambda key: jnp.stack([hd[key] for hd in heads_all], axis=0)
    v_all = jnp.stack([d['v_p'] for d in pre for _ in range(2)], axis=0)
    akv = _bmm(stack('a_ak'), v_all)
    gq = _bf(_bmm(_bf(t_all), _bf(jnp.concatenate([stack('a_h'), akv], axis=2))))
    rb = _bmm(stack('a_rb'), gq)
    h_all = stack('r_h') + rb[:, :, :LANES]
    z_all = rb[:, :, LANES:] + _bmm(stack('a_rk'), v_all)
    seq = []
    for n, d in enumerate(pre):
        g_p = gq[2 * n, :, :LANES] + gq[2 * n + 1, :, :LANES]
        q1_p = jnp.where(first, gq[2 * n, :, LANES:], gq[2 * n + 1, :, LANES:])
        m_p = jnp.where(blockdiag, _dot_tn(g_p, d['b_end']), 0.0)
        n_p = jnp.where(blockdiag, _dot_tn(q1_p, d['b_end']) + _dot_tn(d['v_p'], d['k_end']), 0.0)
        seq.append(dict(h=_bf(h_all[2 * n] + h_all[2 * n + 1]), z=jnp.where(first, z_all[2 * n], z_all[2 * n + 1]),
                        m=_bf(m_p), n=n_p, gam_c=d['gam_c']))

    for ci in range(nc):
        for p in range(n_pairs):
            d = seq[ci * n_pairs + p]
            st = st_ref[p]
            stb = _bf(st)
            y = _dot_nt(d['h'], stb) + d['z']
            st_ref[p] = st * d['gam_c'] + _dot(stb, d['m']) + d['n']
            y_ref[lo + ci * c:lo + (ci + 1) * c, p * LANES:(p + 1) * LANES] = y

    y = y_ref[lo:hi, :]
    mean = _dot_sel_r(y, hsum) * (1.0 / RWKV_N)
    yc = y - mean
    var = _dot(_bf(yc * yc), hsum) * (1.0 / RWKV_N)
    yn = yc * lax.rsqrt(var + RWKV_LN_EPS) * lng_ref[...] + lnb_ref[...]
    o_ref[0, lo:hi, :] = ((yn + bonus) * g).astype(o_ref.dtype)


def _rwkv(p16, p32, v_first, weights, tt, has_vres):
    b, s, _ = p16.shape
    full = lambda shape: pl.BlockSpec(shape, lambda bi, i: (0,) * len(shape))
    tok = lambda w, blk: pl.BlockSpec((1, tt, w), lambda bi, i: (bi, i, blk))
    in_specs = [tok(512, P16_RWKV_R), tok(512, P16_RWKV_R + 1), tok(512, P16_RWKV_R + 2), tok(256, P32_RWKV_LAT)]
    args = [p16, p16, p16, p32]
    if has_vres:
        in_specs += [tok(128, P32_MISC), tok(512, 0)]
        args += [p32, v_first]
    in_specs += [full(w.shape) for w in weights]
    args += list(weights)
    out_specs = [pl.BlockSpec((1, tt, 512), lambda bi, i: (bi, i, 0))]
    out_shape = [jax.ShapeDtypeStruct((b, s, 512), BF16)]
    scratch = [pltpu.VMEM((tt + 8, 1536), F32), pltpu.VMEM((tt + 8, 256), F32)]
    if has_vres:
        scratch.append(pltpu.VMEM((tt + 8, 128), F32))
    else:
        out_specs.append(pl.BlockSpec((1, tt, 512), lambda bi, i: (bi, i, 0)))
        out_shape.append(jax.ShapeDtypeStruct((b, s, 512), F32))
    scratch.append(pltpu.VMEM((tt, RWKV_WIDTH), F32))
    scratch.append(pltpu.VMEM((RWKV_HEADS // 2, LANES, LANES), F32))
    return pl.pallas_call(
        functools.partial(_rwkv_body, tt=tt, has_vres=has_vres),
        grid=(b, s // tt),
        in_specs=in_specs,
        out_specs=out_specs,
        out_shape=out_shape,
        scratch_shapes=scratch,
        compiler_params=_cparams(("parallel", "arbitrary")),
        name="rwkv7",
    )(*args)


def _layernorm(h, g, b):
    mu = jnp.mean(h, axis=-1, keepdims=True)
    hc = h - mu
    var = jnp.mean(hc * hc, axis=-1, keepdims=True)
    return hc * lax.rsqrt(var + LN_EPS) * g + b


def _merge_body(ya_ref, yb_ref, yc_ref, yd_ref, gate_ref, x_ref, wb_ref, wo_ref, g_ref, b_ref, o_ref):
    merged = None
    for n, y_ref in enumerate((ya_ref, yb_ref, yc_ref, yd_ref)):
        gate = _sigmoid(gate_ref[0, :, n * D_MODEL:(n + 1) * D_MODEL].astype(F32))
        term = gate * _dot(y_ref[0], wb_ref[n])
        merged = term if merged is None else merged + term
    mix = _dot(_bf(merged), wo_ref[...])
    o_ref[0] = _layernorm(DEEPNORM_ALPHA * x_ref[0] + mix, g_ref[...], b_ref[...])


def _merge(ys, p16, x, wb, wo, g, bb, tt):
    b, s, d = x.shape
    full = lambda shape: pl.BlockSpec(shape, lambda bi, i: (0,) * len(shape))
    ytok = pl.BlockSpec((1, tt, 512), lambda bi, i: (bi, i, 0))
    return pl.pallas_call(
        _merge_body,
        grid=(b, s // tt),
        in_specs=[ytok, ytok, ytok, ytok,
                  pl.BlockSpec((1, tt, N_BRANCH * D_MODEL), lambda bi, i: (bi, i, P16_GATES)),
                  pl.BlockSpec((1, tt, d), lambda bi, i: (bi, i, 0)),
                  full((N_BRANCH, BRANCH_WIDTH, D_MODEL)), full((D_MODEL, D_MODEL)), full((1, d)), full((1, d))],
        out_specs=pl.BlockSpec((1, tt, d), lambda bi, i: (bi, i, 0)),
        out_shape=jax.ShapeDtypeStruct((b, s, d), F32),
        compiler_params=_cparams(("parallel", "parallel")),
        name="merge_ln1",
    )(*ys, p16, x, wb, wo, g, bb)


def _moe_body(x_ref, rwt_ref, rbias_ref, tri_ref, wg_ref, wu_ref, wd_ref, g_ref, b_ref, o_ref,
              xb_ref, comb_ref, grp_ref, acc_ref):
    gidx = pl.program_id(2)
    tt = x_ref.shape[1]

    @pl.when(gidx == 0)
    def _():
        x = x_ref[0]
        xb_ref[...] = x.astype(BF16)
        acc_ref[...] = jnp.zeros_like(acc_ref)
        scores = _sigmoid(_dot3_nt(rwt_ref[...], x))
        biased = scores + rbias_ref[...]
        row = _iota2((N_EXPERTS, tt), 0).astype(F32)
        row_group = (_iota2((N_EXPERTS, tt), 0) >> 2).astype(F32)
        neg = -jnp.inf
        gs = []
        for gi in range(N_GROUPS):
            rows = [biased[gi * GROUP_SIZE + j:gi * GROUP_SIZE + j + 1] for j in range(GROUP_SIZE)]
            best = None
            for i0 in range(GROUP_SIZE):
                for i1 in range(i0 + 1, GROUP_SIZE):
                    pair = rows[i0] + rows[i1]
                    best = pair if best is None else jnp.maximum(best, pair)
            gs.append(best)
        gmax = functools.reduce(jnp.maximum, gs)
        best_group = jnp.full((1, tt), float(N_GROUPS), F32)
        for gi in reversed(range(N_GROUPS)):
            best_group = jnp.where(gs[gi] == gmax, float(gi), best_group)
        masked = jnp.where(row_group == best_group, biased, neg)
        m1 = jnp.max(masked, axis=0, keepdims=True)
        i1 = jnp.min(jnp.where(masked == m1, row, float(N_EXPERTS)), axis=0, keepdims=True)
        masked2 = jnp.where(row == i1, neg, masked)
        m2 = jnp.max(masked2, axis=0, keepdims=True)
        i2 = jnp.min(jnp.where(masked2 == m2, row, float(N_EXPERTS)), axis=0, keepdims=True)
        s1 = jnp.sum(jnp.where(row == i1, scores, 0.0), axis=0, keepdims=True)
        s2 = jnp.sum(jnp.where(row == i2, scores, 0.0), axis=0, keepdims=True)
        tot = s1 + s2
        comb_t = jnp.where(row == i1, s1 / tot, 0.0) + jnp.where(row == i2, s2 / tot, 0.0)
        for gi in range(N_GROUPS):
            comb_ref[gi, 0:GROUP_SIZE, :] = comb_t[gi * GROUP_SIZE:(gi + 1) * GROUP_SIZE]
            comb_ref[gi, GROUP_SIZE:8, :] = jnp.zeros((8 - GROUP_SIZE, tt), F32)
        grp_ref[...] = jnp.broadcast_to(best_group, (8, tt))

    member = grp_ref[...] == gidx.astype(F32)
    cnt = _dot(jnp.where(member, 1.0, 0.0).astype(BF16), tri_ref[...])
    pos = jnp.where(member, cnt - 1.0, -1.0)
    n_tok = cnt[0:1, tt - 1:tt][0, 0].astype(jnp.int32)
    eye8 = jnp.where(_iota2((8, 8), 0) == _iota2((8, 8), 1), 1.0, 0.0).astype(BF16)
    p_hi, p_lo = _split2(pos)
    pos_col = (_dot_tn(p_hi, eye8) + _dot_tn(p_lo, eye8))[:, 0:1]
    pos_row = pos[0:1]
    c_hi, c_lo = _split2(comb_ref[gidx])
    xb = xb_ref[...]
    r = MOE_ROWS

    def sub_block(k, carry):
        base = (k * r).astype(F32)
        sel = jnp.where(pos_row == base + _iota2((r, 1), 0).astype(F32), 1.0, 0.0).astype(BF16)
        sel_t = jnp.where(pos_col == base + _iota2((1, r), 1).astype(F32), 1.0, 0.0).astype(BF16)
        xc = _bf(_dot(sel, xb))
        cw = _dot_nt(sel, c_hi) + _dot_nt(sel, c_lo)
        y = None
        for j in range(GROUP_SIZE):
            hid = _silu(_dot(xc, wg_ref[0, j])) * _dot(xc, wu_ref[0, j])
            term = cw[:, j:j + 1] * _dot(_bf(hid), wd_ref[0, j])
            y = term if y is None else y + term
        acc_ref[...] += _dot(sel_t, _bf(y))
        return carry

    lax.fori_loop(0, (n_tok + (r - 1)) // r, sub_block, 0)

    @pl.when(gidx == N_GROUPS - 1)
    def _():
        o_ref[0] = _layernorm(DEEPNORM_ALPHA * x_ref[0] + acc_ref[...], g_ref[...], b_ref[...])


def _moe(x, rwt, rbias, wg, wu, wd, g, bb, tt):
    b, s, d = x.shape
    full = lambda shape: pl.BlockSpec(shape, lambda bi, i, e: (0,) * len(shape))
    once = lambda shape: pl.BlockSpec(shape, lambda bi, i, e: (0,) * len(shape), pipeline_mode=pl.Buffered(1))
    grouped = lambda w: w.reshape((N_GROUPS, GROUP_SIZE) + w.shape[1:])
    idx = np.arange(tt)
    tri = jnp.asarray((idx[:, None] <= idx[None, :]).astype(np.float32), BF16)
    return pl.pallas_call(
        _moe_body,
        grid=(b, s // tt, N_GROUPS),
        in_specs=[pl.BlockSpec((1, tt, d), lambda bi, i, e: (bi, i, 0), pipeline_mode=pl.Buffered(1)),
                  full((N_EXPERTS, d)), full((N_EXPERTS, 1)), once((tt, tt)),
                  pl.BlockSpec((1, GROUP_SIZE, d, D_EXPERT), lambda bi, i, e: (e, 0, 0, 0)),
                  pl.BlockSpec((1, GROUP_SIZE, d, D_EXPERT), lambda bi, i, e: (e, 0, 0, 0)),
                  pl.BlockSpec((1, GROUP_SIZE, D_EXPERT, d), lambda bi, i, e: (e, 0, 0, 0)),
                  full((1, d)), full((1, d))],
        out_specs=pl.BlockSpec((1, tt, d), lambda bi, i, e: (bi, i, 0)),
        out_shape=jax.ShapeDtypeStruct((b, s, d), F32),
        scratch_shapes=[pltpu.VMEM((tt, d), BF16), pltpu.VMEM((N_GROUPS, 8, tt), F32), pltpu.VMEM((8, tt), F32),
                        pltpu.VMEM((tt, d), F32)],
        compiler_params=_cparams(("parallel", "parallel", "arbitrary")),
        name="moe_ln2",
    )(x, rwt, rbias, tri, grouped(wg), grouped(wu), grouped(wd), g, bb)


def _cols(w, name):
    o, n = _OFF[name]
    return w[:, o:o + n]


def _pack_inproj(w_in_l, w_vres_l):
    d = w_in_l.shape[0]
    w16 = jnp.concatenate([_cols(w_in_l, "gates"), _cols(w_in_l, "gla_q"), _cols(w_in_l, "gla_k"),
                           _cols(w_in_l, "gla_v"), _cols(w_in_l, "gla_og"), _cols(w_in_l, "gdn_qkv"),
                           _cols(w_in_l, "gdn_z"), _cols(w_in_l, "rwkv_rkv")], axis=1).astype(BF16)
    vres = jnp.zeros((d, RWKV_V_RANK), F32) if w_vres_l is None else w_vres_l
    misc = jnp.concatenate([_cols(w_in_l, "k_rope"), _cols(w_in_l, "gla_gl"), _cols(w_in_l, "gdn_beta"),
                            _cols(w_in_l, "gdn_a"), vres], axis=1)
    misc = jnp.pad(misc, ((0, 0), (0, LANES - misc.shape[1])))
    w32 = jnp.concatenate([_cols(w_in_l, "q_lat"), _cols(w_in_l, "rwkv_lat"), _cols(w_in_l, "c_kv"), misc],
                          axis=1).astype(BF16)
    return w16, w32


def _row_pad(w, start, total=LANES):
    return jnp.pad(w, ((start, total - start - w.shape[0]), (0, 0)))


def _lane_row(vec, start, total=LANES):
    return jnp.pad(vec, (start, total - start - vec.shape[0])).reshape(1, total)


def _pack_mla(w_uq, w_ukv):
    half = MLA_ROPE // 2
    wq = w_uq.reshape(MLA_Q_RANK, MLA_HEADS, MLA_NOPE + MLA_ROPE)
    nope, r1, r2 = wq[..., :MLA_NOPE], wq[..., MLA_NOPE:MLA_NOPE + half], wq[..., MLA_NOPE + half:]
    pad = jnp.zeros((MLA_Q_RANK, MLA_HEADS, LANES - MLA_NOPE - MLA_ROPE), F32)
    wqa = jnp.concatenate([nope, r1, r2, pad], axis=-1).reshape(MLA_Q_RANK, MLA_HEADS * LANES)
    wqb = jnp.concatenate([jnp.zeros_like(nope), -r2, r1, pad], axis=-1).reshape(MLA_Q_RANK, MLA_HEADS * LANES)
    wkv = w_ukv.reshape(MLA_KV_RANK, MLA_HEADS, MLA_NOPE + MLA_V)
    wuk = jnp.pad(wkv[..., :MLA_NOPE], ((0, 0), (0, 0), (0, LANES - MLA_NOPE))).reshape(MLA_KV_RANK, MLA_HEADS * LANES)
    wuv = wkv[..., MLA_NOPE:].reshape(MLA_KV_RANK, MLA_HEADS * MLA_V).T
    ra = np.zeros((LANES, LANES), np.float32)
    rb = np.zeros((LANES, LANES), np.float32)
    for j in range(half):
        ra[MISC_KROPE + j, MLA_NOPE + j] = 1.0
        ra[MISC_KROPE + half + j, MLA_NOPE + half + j] = 1.0
        rb[MISC_KROPE + half + j, MLA_NOPE + j] = -1.0
        rb[MISC_KROPE + j, MLA_NOPE + half + j] = 1.0
    return (wqa.astype(BF16), wqb.astype(BF16), wuk.astype(BF16), wuv.astype(BF16),
            jnp.asarray(ra, BF16), jnp.asarray(rb, BF16))


def _rope_freq_row():
    half = MLA_ROPE // 2
    inv_freq = ROPE_THETA ** (-jnp.arange(half, dtype=F32) / half)
    return jnp.concatenate([jnp.zeros((MLA_NOPE,), F32), inv_freq, inv_freq,
                            jnp.zeros((LANES - MLA_NOPE - MLA_ROPE,), F32)]).reshape(1, LANES)


def _head_sum_matrix():
    idx = np.arange(RWKV_WIDTH) // RWKV_N
    return jnp.asarray((idx[:, None] == idx[None, :]).astype(np.float32), BF16)


def _tile(s, pref):
    t = min(pref, s)
    assert s % t == 0 and t % CHUNK == 0
    return t


def kernel(x, positions, router_w, w_in, w_in_vres, mla_q_norm, mla_w_uq, mla_kv_norm, mla_w_ukv,
           gla_w_gate_up, gla_b_gate, gla_norm, gdn_conv, gdn_a_log, gdn_dt_bias, gdn_norm,
           rwkv_mu, rwkv_w0, rwkv_w_up, rwkv_a0, rwkv_a_up, rwkv_g_up, rwkv_k_k, rwkv_k_a, rwkv_r_k,
           rwkv_ln_g, rwkv_ln_b, rwkv_vres_mu, rwkv_v0, rwkv_v_up, w_branch, w_out,
           ln1_g, ln1_b, ln2_g, ln2_b, router_bias, moe_w_gate, moe_w_up, moe_w_down):
    b, s, d = x.shape
    assert d == D_MODEL and w_in.shape[2] == IN_COLS
    pos3 = positions.reshape(b, s, 1)
    rwt = router_w.T
    hsum = _head_sum_matrix()
    row = lambda v: v.reshape(1, -1)
    t_proj = _tile(s, 1024)
    t_scan = _tile(s, 256)
    t_rwkv = _tile(s, 256)
    t_attn = _tile(s, 512)
    t_merge = _tile(s, 512)
    t_moe = _tile(s, 1024)

    cos_tab, sin_tab = _rope_tables(pos3, _rope_freq_row(), t_attn)
    v_first = None
    for l in range(DEPTH):
        w16, w32 = _pack_inproj(w_in[l], None if l == 0 else w_in_vres[l - 1])
        p16, p32 = _inproj(x, w16, w32, t_proj, 1024)

        wqa, wqb, wuk, wuv, wkra, wkrb = _pack_mla(mla_w_uq[l], mla_w_ukv[l])
        q, k, v = _mla_prep(p32, cos_tab, sin_tab, row(mla_q_norm[l]), wqa, wqb, row(mla_kv_norm[l]), wuk, wuv,
                            wkra, wkrb, t_attn)
        y_a = _mla_attn(q, k, v, t_attn)

        wgu = _row_pad(gla_w_gate_up[l], MISC_GLA_GL)
        y_b = _gla(p16, p32, wgu, row(gla_b_gate[l]), row(gla_norm[l]), t_scan)

        y_c = _gdn(p16, p32, gdn_conv[l], _lane_row(gdn_a_log[l], MISC_GDN_A), _lane_row(gdn_dt_bias[l], MISC_GDN_A),
                   row(gdn_norm[l]), t_scan)

        mu = rwkv_mu[l]
        weights = [row(mu[:1536]), row(mu[1536:]), row(rwkv_w0[l]),
                   _row_pad(rwkv_w_up[l], 0).astype(BF16), row(rwkv_a0[l]),
                   _row_pad(rwkv_a_up[l], RWKV_W_RANK).astype(BF16), rwkv_g_up[l].astype(BF16),
                   row(rwkv_k_k[l]), row(rwkv_k_a[l]), row(rwkv_r_k[l]), row(rwkv_ln_g[l]), row(rwkv_ln_b[l])]
        if l == 0:
            y_d, v_first = _rwkv(p16, p32, None, weights + [hsum], t_rwkv, False)
        else:
            weights += [_lane_row(rwkv_vres_mu[l - 1], MISC_VRES), row(rwkv_v0[l - 1]),
                        _row_pad(rwkv_v_up[l - 1], MISC_VRES).astype(BF16)]
            (y_d,) = _rwkv(p16, p32, v_first, weights + [hsum], t_rwkv, True)

        x = _merge((y_a, y_b, y_c, y_d), p16, x, w_branch[l].astype(BF16), w_out[l].astype(BF16),
                   row(ln1_g[l]), row(ln1_b[l]), t_merge)
        x = _moe(x, rwt, router_bias[l].reshape(N_EXPERTS, 1), moe_w_gate[l].astype(BF16),
                 moe_w_up[l].astype(BF16), moe_w_down[l].astype(BF16), row(ln2_g[l]), row(ln2_b[l]), t_moe)
    return x
```
